```python
import math
import jax
import jax.numpy as jnp
from jax import lax
import numpy as np

D_MODEL = 1024
BATCH = 1
SEQ = 16384
DEPTH = 4

GRID_W = 64
CTX_LEN = 256
N_EVEN = (DEPTH + 1) // 2
N_ODD = DEPTH // 2
EPS = 1e-6

HG_WIDTH = D_MODEL // 2
HG_DK = 128
HG_HEADS = HG_WIDTH // HG_DK
HG_DV = HG_WIDTH // HG_HEADS
HG_CHUNK = 64
SG_WIDTH = D_MODEL // 2
SG_CHUNK = 128
SG_GROUPS = 4
SG_GDIM = SG_WIDTH // SG_GROUPS
EVEN_IN = 5 * HG_WIDTH + 2 * SG_WIDTH
EVEN_OUT = HG_WIDTH + SG_WIDTH
DA_HEADS = 8
DA_DH = D_MODEL // DA_HEADS // 2
DA_DV = 2 * DA_DH
DA_QKV = 2 * DA_HEADS * 2 * DA_DH + DA_HEADS * DA_DV
Q_BLOCK = 128
ROPE_BASE = 10000.0
ROPE_AXIS = DA_DH // 2
D_FF = 256 * ((8 * D_MODEL // 3 + 255) // 256)
N_EXPERTS = 8
TOP_K = 2
E_FF = 7 * D_MODEL // 2
MOE_BLOCK = 128

kernel_name = "hybrid_hgrn2_sgu_diffattn_moe_dit"

F32 = jnp.float32


def _rms(x, eps=EPS):
    xf = x.astype(F32)
    return xf * lax.rsqrt(jnp.mean(xf * xf, axis=-1, keepdims=True) + eps)


def _layernorm(x, g, b, eps=1e-5):
    xf = x.astype(F32)
    mu = jnp.mean(xf, axis=-1, keepdims=True)
    var = jnp.mean(jnp.square(xf - mu), axis=-1, keepdims=True)
    return (xf - mu) * lax.rsqrt(var + eps) * g.astype(F32) + b.astype(F32)


def _ada(cvec, w, b):
    m = jax.nn.silu(cvec) @ w + b
    return jnp.split(m, 6, axis=-1)


def _modulate(x, shift, scale):
    return (_rms(x) * (1.0 + scale.astype(F32)) + shift.astype(F32)).astype(x.dtype)


def _hgrn_gates(f_pre, lb):
    b, n, _ = f_pre.shape
    f = f_pre.astype(F32)
    lb = lb.astype(F32)
    log_f = jnp.log(lb + (1.0 - lb) * jax.nn.sigmoid(f))
    k = (1.0 - lb) * jax.nn.sigmoid(-f)
    shp = (b, n, HG_HEADS, HG_DK)
    return log_f.reshape(shp), k.reshape(shp)


def _hgrn_chunk_scan(q, log_f, k, v, s0):
    b, n, h, _ = q.shape
    dv = v.shape[-1]
    nc = n // HG_CHUNK

    def chunks(a):
        return a.reshape(b, nc, HG_CHUNK, h, a.shape[-1]).transpose(1, 0, 3, 2, 4)

    lower = jnp.tril(jnp.ones((HG_CHUNK, HG_CHUNK), dtype=bool))[:, :, None]

    def step(s, inp):
        qc, lc, kc, vc = inp
        cum = jnp.cumsum(lc, axis=2)
        rel = cum[:, :, :, None, :] - cum[:, :, None, :, :]
        decay = jnp.where(lower, jnp.exp(jnp.where(lower, rel, 0.0)), 0.0)
        scores = jnp.einsum('bhtk,bhtsk,bhsk->bhts', qc, decay, kc)
        tot = cum[:, :, -1, :]
        o = scores @ vc + jnp.einsum('bhtk,bhkv->bhtv', qc * jnp.exp(cum), s)
        s_new = jnp.exp(tot)[..., None] * s + jnp.einsum(
            'bhsk,bhsv->bhkv', kc * jnp.exp(tot[:, :, None, :] - cum), vc)
        return s_new, o

    s_fin, o = lax.scan(step, s0, (chunks(q), chunks(log_f), chunks(k), chunks(v)))
    return o.transpose(1, 0, 3, 2, 4).reshape(b, n, h, dv), s_fin


def _hgrn_bidir(q, f_fw, f_bw, i, lb_f, lb_b, s0_f, s0_b):
    b, n, _ = q.shape
    qh = jax.nn.silu(q.astype(F32)).reshape(b, n, HG_HEADS, HG_DK)
    vh = i.astype(F32).reshape(b, n, HG_HEADS, HG_DV)
    lf_f, k_f = _hgrn_gates(f_fw, lb_f)
    lf_b, k_b = _hgrn_gates(f_bw, lb_b)
    o_f, s_f = _hgrn_chunk_scan(qh, lf_f, k_f, vh, s0_f)
    rev = lambda a: jnp.flip(a, axis=1)
    o_b, s_b = _hgrn_chunk_scan(rev(qh), rev(lf_b), rev(k_b), rev(vh), s0_b)
    return o_f + rev(o_b), s_f, s_b


def _hgrn_readout(o, g, gain):
    b, n = o.shape[:2]
    gate = jax.nn.silu(g.astype(F32)).reshape(b, n, HG_HEADS, HG_DV)
    return (_rms(o) * gain.astype(F32) * gate).reshape(b, n, HG_WIDTH)


def _spatial_gate(u, v, ln_g, ln_b, sg_w, sg_b):
    b, n, _ = u.shape
    u = jax.nn.gelu(u.astype(F32), approximate=False)
    v = _layernorm(jax.nn.gelu(v.astype(F32), approximate=False), ln_g, ln_b)
    vg = v.reshape(b, n // SG_CHUNK, SG_CHUNK, SG_GROUPS, SG_GDIM)
    mixed = jnp.einsum('gts,bnsgc->bntgc', sg_w.astype(F32), vg)
    mixed = mixed + sg_b.astype(F32).T[None, None, :, :, None]
    return u * mixed.reshape(b, n, SG_WIDTH)


def _even_mixer(hx, hc, w_in, lb_f, lb_b, hg_g, ln_g, ln_b, sg_w, sg_b, w_out, need_ctx):
    splits = [HG_WIDTH * i for i in range(1, 6)] + [5 * HG_WIDTH + SG_WIDTH]
    qx, ffx, fbx, ix, gx, ux, vx = jnp.split(hx @ w_in, splits, axis=-1)
    qc, ffc, fbc, ic, gc, uc, vc = jnp.split(hc @ w_in, splits, axis=-1)
    zero = jnp.zeros((hx.shape[0], HG_HEADS, HG_DK, HG_DV), F32)
    o_ctx, s_f, s_b = _hgrn_bidir(qc, ffc, fbc, ic, lb_f, lb_b, zero, zero)
    o_lat, _, _ = _hgrn_bidir(qx, ffx, fbx, ix, lb_f, lb_b, s_f, s_b)

    def merge(o_hg, g, u, v):
        a = _hgrn_readout(o_hg, g, hg_g)
        s = _spatial_gate(u, v, ln_g, ln_b, sg_w, sg_b)
        return jnp.concatenate([a, s], axis=-1).astype(hx.dtype) @ w_out

    ox = merge(o_lat, gx, ux, vx)
    oc = merge(o_ctx, gc, uc, vc) if need_ctx else None
    return ox, oc


def _axial_rope_tables(n):
    rows = n // GRID_W
    row = jnp.repeat(jnp.arange(rows, dtype=F32), GRID_W)
    col = jnp.tile(jnp.arange(GRID_W, dtype=F32), rows)
    inv = 1.0 / (ROPE_BASE ** (jnp.arange(0, ROPE_AXIS, 2, dtype=F32) / ROPE_AXIS))
    ar = row[:, None] * inv
    ac = col[:, None] * inv
    ang = jnp.concatenate([ar, ar, ac, ac], axis=-1)
    return jnp.cos(ang), jnp.sin(ang)


def _apply_rope(x, cos, sin):
    xr = x.reshape(x.shape[:-1] + (2, 2, ROPE_AXIS // 2))
    rot = jnp.stack([-xr[..., 1, :], xr[..., 0, :]], axis=-2).reshape(x.shape)
    c = cos[None, :, None, None, :]
    s = sin[None, :, None, None, :]
    return (x.astype(F32) * c + rot.astype(F32) * s).astype(x.dtype)


def _diff_attend(q, k, v, lam):
    b, nq = q.shape[:2]
    nb = nq // Q_BLOCK
    scale = DA_DH ** -0.5
    qb = q.reshape(b, nb, Q_BLOCK, DA_HEADS, 2, DA_DH).transpose(1, 0, 2, 3, 4, 5)

    def one(qblk):
        s = jnp.einsum('bqhjd,bkhjd->bhjqk', qblk, k).astype(F32) * scale
        p = jax.nn.softmax(s, axis=-1)
        a = p[:, :, 0] - lam * p[:, :, 1]
        return jnp.einsum('bhqk,bkhv->bqhv', a.astype(v.dtype), v)

    o = lax.map(one, qb)
    return o.transpose(1, 0, 2, 3, 4).reshape(b, nq, DA_HEADS, DA_DV)


def _diff_attention(hx, hc, w_qkv, lam, subln_g, w_out, lam_init, cos, sin, need_ctx):
    def proj(h):
        b, n, _ = h.shape
        q, k, v = jnp.split(h @ w_qkv, [DA_HEADS * 2 * DA_DH, 2 * DA_HEADS * 2 * DA_DH], axis=-1)
        return (q.reshape(b, n, DA_HEADS, 2, DA_DH), k.reshape(b, n, DA_HEADS, 2, DA_DH),
                v.reshape(b, n, DA_HEADS, DA_DV))

    qx, kx, vx = proj(hx)
    qx = _apply_rope(qx, cos, sin)
    kx = _apply_rope(kx, cos, sin)
    qc, kc, vc = proj(hc)
    lf = lam.astype(F32)
    lam_full = jnp.exp(jnp.sum(lf[0] * lf[1])) - jnp.exp(jnp.sum(lf[2] * lf[3])) + lam_init
    k_all = jnp.concatenate([kc, kx], axis=1)
    v_all = jnp.concatenate([vc, vx], axis=1)

    def finish(o):
        b, n = o.shape[:2]
        o = _rms(o, 1e-5) * subln_g.astype(F32) * (1.0 - lam_init)
        return o.reshape(b, n, D_MODEL).astype(hx.dtype) @ w_out

    ox = finish(_diff_attend(qx, k_all, v_all, lam_full))
    oc = finish(_diff_attend(qc, kc, vc, lam_full)) if need_ctx else None
    return ox, oc


def _swiglu(h, wg, wu, wd):
    return (jax.nn.silu(h @ wg) * (h @ wu)) @ wd


def _moe_swiglu(h, w_router, w_gate, w_up, w_down):
    b, n, d = h.shape
    t = b * n
    hf = h.reshape(t, d)
    logits = (hf @ w_router).astype(F32)
    top_v, top_i = lax.top_k(logits, TOP_K)
    top_w = jax.nn.softmax(top_v, axis=-1)
    flat_e = top_i.reshape(-1)
    flat_w = top_w.reshape(-1)
    flat_tok = jnp.repeat(jnp.arange(t, dtype=jnp.int32), TOP_K)
    order = jnp.argsort(flat_e)
    sorted_e = flat_e[order]
    counts = jnp.bincount(flat_e, length=N_EXPERTS)
    padded = ((counts + MOE_BLOCK - 1) // MOE_BLOCK) * MOE_BLOCK
    grp_start = jnp.cumsum(counts) - counts
    pad_end = jnp.cumsum(padded)
    pad_start = pad_end - padded
    dest = pad_start[sorted_e] + jnp.arange(t * TOP_K) - grp_start[sorted_e]
    n_rows = ((t * TOP_K + MOE_BLOCK - 1) // MOE_BLOCK) * MOE_BLOCK + N_EXPERTS * MOE_BLOCK
    n_blocks = n_rows // MOE_BLOCK
    row_tok = jnp.zeros((n_rows,), jnp.int32).at[dest].set(flat_tok[order])
    row_w = jnp.zeros((n_rows,), F32).at[dest].set(flat_w[order])
    blk_start = jnp.arange(n_blocks) * MOE_BLOCK
    blk_e = jnp.minimum(jnp.sum(pad_end[None, :] <= blk_start[:, None], axis=1), N_EXPERTS - 1)

    def one(args):
        tok, e = args
        xb = hf[tok]
        return (jax.nn.silu(xb @ w_gate[e]) * (xb @ w_up[e])) @ w_down[e]

    out = lax.map(one, (row_tok.reshape(n_blocks, MOE_BLOCK), blk_e))
    out = out.reshape(n_rows, d) * row_w[:, None].astype(out.dtype)
    y = jnp.zeros((t, d), h.dtype).at[row_tok].add(out)
    return y.reshape(b, n, d)


def setup_inputs(seed: int = 0) -> dict:
    key = jax.random.key(seed)
    ks = jax.random.split(key, 26)
    D = D_MODEL

    def nrm(k, shape, scale):
        return jax.random.normal(k, shape, F32) * scale

    return {
        "x": nrm(ks[0], (BATCH, SEQ, D), 1.0),
        "c": nrm(ks[1], (BATCH, D), 1.0),
        "ctx": nrm(ks[2], (BATCH, CTX_LEN, D), 1.0),
        "c_ctx": nrm(ks[3], (D,), 1.0),
        "mod_w": nrm(ks[4], (DEPTH, D, 6 * D), D ** -0.5),
        "mod_b": nrm(ks[5], (DEPTH, 6 * D), 0.01),
        "hg_lb": nrm(ks[6], (2, N_EVEN, HG_WIDTH), 1.0),
        "ev_w_in": nrm(ks[7], (N_EVEN, D, EVEN_IN), D ** -0.5),
        "hg_norm_g": 1.0 + nrm(ks[8], (N_EVEN, HG_DV), 0.01),
        "sg_ln_g": 1.0 + nrm(ks[9], (N_EVEN, SG_WIDTH), 0.01),
        "sg_ln_b": nrm(ks[10], (N_EVEN, SG_WIDTH), 0.01),
        "sg_w": nrm(ks[11], (N_EVEN, SG_GROUPS, SG_CHUNK, SG_CHUNK), SG_CHUNK ** -0.5),
        "sg_b": 1.0 + nrm(ks[12], (N_EVEN, SG_GROUPS, SG_CHUNK), 0.01),
        "ev_w_out": nrm(ks[13], (N_EVEN, EVEN_OUT, D), EVEN_OUT ** -0.5),
        "da_w_qkv": nrm(ks[14], (N_ODD, D, DA_QKV), D ** -0.5),
        "da_lam": nrm(ks[15], (N_ODD, 4, DA_DH), 0.1),
        "da_subln_g": 1.0 + nrm(ks[16], (N_ODD, DA_DV), 0.01),
        "da_w_out": nrm(ks[17], (N_ODD, DA_HEADS * DA_DV, D), (DA_HEADS * DA_DV) ** -0.5),
        "ffn_w_gate": nrm(ks[18], (N_EVEN, D, D_FF), D ** -0.5),
        "ffn_w_up": nrm(ks[19], (N_EVEN, D, D_FF), D ** -0.5),
        "ffn_w_down": nrm(ks[20], (N_EVEN, D_FF, D), D_FF ** -0.5),
        "moe_w_router": nrm(ks[21], (N_ODD, D, N_EXPERTS), D ** -0.5),
        "moe_w_gate": nrm(ks[22], (N_ODD, N_EXPERTS, D, E_FF), D ** -0.5),
        "moe_w_up": nrm(ks[23], (N_ODD, N_EXPERTS, D, E_FF), D ** -0.5),
        "moe_w_down": nrm(ks[24], (N_ODD, N_EXPERTS, E_FF, D), E_FF ** -0.5),
        "final_g": 1.0 + nrm(ks[25], (D,), 0.01),
    }


def reference(x, c, ctx, c_ctx, mod_w, mod_b, hg_lb, ev_w_in, hg_norm_g, sg_ln_g, sg_ln_b,
              sg_w, sg_b, ev_w_out, da_w_qkv, da_lam, da_subln_g, da_w_out, ffn_w_gate,
              ffn_w_up, ffn_w_down, moe_w_router, moe_w_gate, moe_w_up, moe_w_down, final_g):
    n = x.shape[1]
    cos, sin = _axial_rope_tables(n)
    lb_sm = jax.nn.softmax(hg_lb.astype(F32), axis=1)
    lb_all = jnp.cumsum(lb_sm, axis=1) - lb_sm[:, :1]
    cx = ctx
    for layer in range(DEPTH):
        j = layer // 2
        last = layer == DEPTH - 1
        sh1, sc1, g1, sh2, sc2, g2 = _ada(c, mod_w[layer], mod_b[layer])
        csh1, csc1, cg1, csh2, csc2, cg2 = _ada(c_ctx, mod_w[layer], mod_b[layer])
        hx = _modulate(x, sh1[:, None], sc1[:, None])
        hc = _modulate(cx, csh1, csc1)
        if layer % 2 == 0:
            ox, oc = _even_mixer(hx, hc, ev_w_in[j], lb_all[0, j], lb_all[1, j], hg_norm_g[j],
                                 sg_ln_g[j], sg_ln_b[j], sg_w[j], sg_b[j], ev_w_out[j],
                                 not last)
            ffn = lambda h: _swiglu(h, ffn_w_gate[j], ffn_w_up[j], ffn_w_down[j])
        else:
            lam_init = 0.8 - 0.6 * math.exp(-0.3 * layer)
            ox, oc = _diff_attention(hx, hc, da_w_qkv[j], da_lam[j], da_subln_g[j], da_w_out[j],
                                     lam_init, cos, sin, not last)
            ffn = lambda h: _moe_swiglu(h, moe_w_router[j], moe_w_gate[j], moe_w_up[j],
                                        moe_w_down[j])
        x = x + g1[:, None] * ox
        x = x + g2[:, None] * ffn(_modulate(x, sh2[:, None], sc2[:, None]))
        if not last:
            cx = cx + cg1 * oc
            cx = cx + cg2 * ffn(_modulate(cx, csh2, csc2))
    return (_rms(x) * final_g.astype(F32)).astype(x.dtype)
```

```python
import functools
import math

import jax
import jax.numpy as jnp
from jax import lax
from jax.experimental import pallas as pl
from jax.experimental.pallas import tpu as pltpu

F32 = jnp.float32
BF16 = jnp.bfloat16

D_MODEL = 1024
DEPTH = 4
GRID_W = 64
CTX_LEN = 256
EPS = 1e-6

HG_WIDTH = 512
HG_HEADS = 4
HG_D = 128
HG_SUB = 16
SG_WIDTH = 512
SG_CHUNK = 128
SG_GROUPS = 4
EVEN_IN = 5 * HG_WIDTH + 2 * SG_WIDTH

DA_HEADS = 8
DA_DH = 64
DA_DV = 128
ROPE_BASE = 10000.0
ROPE_AXIS = DA_DH // 2

D_FF = 2816
N_EXPERTS = 8
TOP_K = 2
E_FF = 3584

LANES = 128
ROW_TILE = CTX_LEN
ATT_KV_TILE = 1280
MOE_TILE = 512
MOE_FF_TILE = 512
FFN_FF_TILE = 256
VMEM_LIMIT = 56 * 1024 * 1024


def _cparams(sem):
    return pltpu.CompilerParams(dimension_semantics=sem, vmem_limit_bytes=VMEM_LIMIT)


def _sigmoid(x):
    return 1.0 / (1.0 + jnp.exp(-x))


def _silu(x):
    return x * _sigmoid(x)


def _gelu(x):
    return 0.5 * x * (1.0 + lax.erf(x * (2.0 ** -0.5)))


def _modulated(x, sh, sc):
    ms = jnp.mean(x * x, axis=-1, keepdims=True)
    return x * lax.rsqrt(ms + EPS) * (1.0 + sc) + sh


def _mod_spec(width=D_MODEL):
    return pl.BlockSpec((None, 1, width), lambda i, *_: (jnp.minimum(i, 1), 0, 0))


def _ada_kernel(c_ref, w_ref, b_ref, o_ref):
    s = _silu(c_ref[...])
    w = w_ref[...]
    r0 = jnp.sum(s[:, 0:1] * w, axis=0, keepdims=True)
    r1 = jnp.sum(s[:, 1:2] * w, axis=0, keepdims=True)
    o_ref[0:1, :] = r0 + b_ref[...]
    o_ref[1:2, :] = r1 + b_ref[...]


def _ada_all(cc_t, mod_w, mod_b):
    n = 6 * D_MODEL
    tn = 1536
    return pl.pallas_call(
        _ada_kernel,
        grid=(DEPTH, n // tn),
        in_specs=[
            pl.BlockSpec((D_MODEL, 2), lambda l, j: (0, 0)),
            pl.BlockSpec((None, D_MODEL, tn), lambda l, j: (l, 0, j)),
            pl.BlockSpec((None, 1, tn), lambda l, j: (l, 0, j)),
        ],
        out_specs=pl.BlockSpec((None, 2, tn), lambda l, j: (l, 0, j)),
        out_shape=jax.ShapeDtypeStruct((DEPTH, 2, n), F32),
        compiler_params=_cparams(("arbitrary", "arbitrary")),
        name="ada",
    )(cc_t, mod_w, mod_b.reshape(DEPTH, 1, n))


def _even_in_kernel(x_ref, sh_ref, sc_ref, w_ref, o_ref):
    h = _modulated(x_ref[...], sh_ref[...], sc_ref[...]).astype(BF16)
    for n in range(0, EVEN_IN, 512):
        o_ref[:, n:n + 512] = jnp.dot(h, w_ref[:, n:n + 512], preferred_element_type=F32)


def _even_in(x, sh, sc, w):
    t = x.shape[0]
    return pl.pallas_call(
        _even_in_kernel,
        grid=(t // ROW_TILE,),
        in_specs=[
            pl.BlockSpec((ROW_TILE, D_MODEL), lambda i: (i, 0)),
            _mod_spec(), _mod_spec(),
            pl.BlockSpec((D_MODEL, EVEN_IN), lambda i: (0, 0)),
        ],
        out_specs=pl.BlockSpec((ROW_TILE, EVEN_IN), lambda i: (i, 0)),
        out_shape=jax.ShapeDtypeStruct((t, EVEN_IN), F32),
        compiler_params=_cparams(("arbitrary",)),
        name="even_in",
    )(x, sh, sc, w)


def _group_scan(x, r_in_group, forward):
    rows = x.shape[0]
    for sh in (1, 2, 4, 8):
        if forward:
            rolled = pltpu.roll(x, sh, 0)
            ok = r_in_group >= sh
        else:
            rolled = pltpu.roll(x, rows - sh, 0)
            ok = r_in_group < HG_SUB - sh
        x = x + jnp.where(ok, rolled, 0.0)
    return x


def _hgrn_kernel(*refs, rev, readout):
    if readout:
        q_ref, f_ref, i_ref, lb_ref, of_ref, g_ref, gain_ref, o_ref, st_ref = refs
    else:
        q_ref, f_ref, i_ref, lb_ref, o_ref, st_ref = refs
    rows = q_ref.shape[0]
    n_groups = rows // HG_SUB

    @pl.when(pl.program_id(1) == 0)
    def _():
        st_ref[...] = jnp.zeros_like(st_ref)

    f = f_ref[...]
    lb = lb_ref[...]
    log_f = jnp.log(lb + (1.0 - lb) * _sigmoid(f))
    kk = (1.0 - lb) * _sigmoid(-f)
    qs = _silu(q_ref[...])
    v = i_ref[...]

    row = lax.broadcasted_iota(jnp.int32, (rows, HG_D), 0)
    r16 = row & (HG_SUB - 1)
    grp = row // HG_SUB
    pfx = _group_scan(log_f, r16, True)
    sfx = _group_scan(log_f, r16, False)
    cum, other = (sfx, pfx) if rev else (pfx, sfx)
    qd = (qs * jnp.exp(cum)).astype(BF16)
    kd = (kk * jnp.exp(other - log_f)).astype(BF16)

    ones = jnp.ones((HG_D, HG_D), BF16)
    acc = jnp.zeros((rows, HG_D), F32)
    for d in range(HG_SUB):
        if d == 0:
            kk_s, cum_s, v_s = kk, cum, v
            p = qs * kk_s
        else:
            shift = rows - d if rev else d
            kk_s = pltpu.roll(kk, shift, 0)
            cum_s = pltpu.roll(cum, shift, 0)
            v_s = pltpu.roll(v, shift, 0)
            ok = (r16 < HG_SUB - d) if rev else (r16 >= d)
            p = jnp.where(ok, qs * kk_s * jnp.exp(cum - cum_s), 0.0)
        rsum = jnp.dot(p.astype(BF16), ones, preferred_element_type=F32)
        acc = acc + rsum * v_s

    v_t = v.T.astype(BF16)
    bd = jnp.concatenate(
        [jnp.where(grp == g, kd, jnp.zeros_like(kd)) for g in range(n_groups)], axis=1)
    u = jnp.dot(v_t, bd, preferred_element_type=F32)

    st = st_ref[...]
    outs = [None] * n_groups
    for g in (range(n_groups - 1, -1, -1) if rev else range(n_groups)):
        lo = g * HG_SUB
        o_inter = lax.dot_general(qd[lo:lo + HG_SUB], st.astype(BF16),
                                  (((1,), (1,)), ((), ())), preferred_element_type=F32)
        outs[g] = acc[lo:lo + HG_SUB] + o_inter
        edge = lo if rev else lo + HG_SUB - 1
        st = st * jnp.exp(cum[edge:edge + 1, :]) + u[:, g * HG_D:(g + 1) * HG_D]
    st_ref[...] = st
    o = jnp.concatenate(outs, axis=0)

    if readout:
        o = o + of_ref[...]
        ms = jnp.mean(o * o, axis=-1, keepdims=True)
        o = o * lax.rsqrt(ms + EPS) * gain_ref[...] * _silu(g_ref[...])
    o_ref[...] = o.astype(o_ref.dtype)


def _hgrn(y, lb, rev, o_fwd=None, gain=None):
    t = y.shape[0]
    nblk = t // ROW_TILE
    readout = o_fwd is not None
    if rev:
        blk = lambda c: jnp.where(c == 0, 0, nblk - c)
    else:
        blk = lambda c: c
    col = lambda slab: pl.BlockSpec((ROW_TILE, HG_D), lambda h, c: (blk(c), slab * HG_HEADS + h))
    in_specs = [col(0), col(2 if rev else 1), col(3),
                pl.BlockSpec((None, 1, HG_D), lambda h, c: (h, 0, 0))]
    args = [y, y, y, lb.reshape(HG_HEADS, 1, HG_D)]
    if readout:
        in_specs += [pl.BlockSpec((ROW_TILE, HG_D), lambda h, c: (blk(c), h)), col(4),
                     pl.BlockSpec((1, HG_D), lambda h, c: (0, 0))]
        args += [o_fwd, y, gain.reshape(1, HG_D)]
    return pl.pallas_call(
        functools.partial(_hgrn_kernel, rev=rev, readout=readout),
        grid=(HG_HEADS, nblk),
        in_specs=in_specs,
        out_specs=pl.BlockSpec((ROW_TILE, HG_D), lambda h, c: (blk(c), h)),
        out_shape=jax.ShapeDtypeStruct((t, HG_WIDTH), BF16 if readout else F32),
        scratch_shapes=[pltpu.VMEM((HG_D, HG_D), F32)],
        compiler_params=_cparams(("arbitrary", "arbitrary")),
        name="hgrn_bwd" if rev else "hgrn_fwd",
    )(*args)


def _even_out_kernel(a_ref, u_ref, v_ref, lng_ref, lnb_ref, sgw_ref, sgb_ref, w_ref, x_ref,
                     gate_ref, o_ref):
    u = _gelu(u_ref[...])
    v = _gelu(v_ref[...])
    mu = jnp.mean(v, axis=-1, keepdims=True)
    var = jnp.mean(jnp.square(v - mu), axis=-1, keepdims=True)
    vn = ((v - mu) * lax.rsqrt(var + 1e-5) * lng_ref[...] + lnb_ref[...]).astype(BF16)
    rows = u.shape[0]
    mixed_rows = []
    for c in range(rows // SG_CHUNK):
        lo = c * SG_CHUNK
        parts = []
        for g in range(SG_GROUPS):
            m = jnp.dot(sgw_ref[g], vn[lo:lo + SG_CHUNK, g * LANES:(g + 1) * LANES],
                        preferred_element_type=F32)
            parts.append(m + sgb_ref[:, g:g + 1])
        mixed_rows.append(jnp.concatenate(parts, axis=1))
    s = (u * jnp.concatenate(mixed_rows, axis=0)).astype(BF16)
    y = jnp.dot(a_ref[...], w_ref[0:HG_WIDTH, :], preferred_element_type=F32)
    y = y + jnp.dot(s, w_ref[HG_WIDTH:, :], preferred_element_type=F32)
    o_ref[...] = x_ref[...] + gate_ref[...] * y


def _even_out(a, y_proj, ln_g, ln_b, sg_w, sg_b_t, w_out, x, gate):
    t = x.shape[0]
    full = lambda shape: pl.BlockSpec(shape, lambda i: (0,) * len(shape))
    return pl.pallas_call(
        _even_out_kernel,
        grid=(t // ROW_TILE,),
        in_specs=[
            pl.BlockSpec((ROW_TILE, HG_WIDTH), lambda i: (i, 0)),
            pl.BlockSpec((ROW_TILE, SG_WIDTH), lambda i: (i, 5)),
            pl.BlockSpec((ROW_TILE, SG_WIDTH), lambda i: (i, 6)),
            full((1, SG_WIDTH)), full((1, SG_WIDTH)),
            full((SG_GROUPS, SG_CHUNK, SG_CHUNK)), full((SG_CHUNK, SG_GROUPS)),
            full((D_MODEL, D_MODEL)),
            pl.BlockSpec((ROW_TILE, D_MODEL), lambda i: (i, 0)),
            _mod_spec(),
        ],
        out_specs=pl.BlockSpec((ROW_TILE, D_MODEL), lambda i: (i, 0)),
        out_shape=jax.ShapeDtypeStruct((t, D_MODEL), F32),
        compiler_params=_cparams(("arbitrary",)),
        name="even_out",
    )(a, y_proj, y_proj, ln_g.reshape(1, -1), ln_b.reshape(1, -1), sg_w, sg_b_t, w_out, x, gate)


def _ffn_kernel(x_ref, sh_ref, sc_ref, gate_ref, wg_ref, wu_ref, wd_ref, o_ref):
    x = x_ref[...]
    h = _modulated(x, sh_ref[...], sc_ref[...]).astype(BF16)
    y = jnp.zeros(x.shape, F32)
    for n in range(0, D_FF, FFN_FF_TILE):
        g = jnp.dot(h, wg_ref[:, n:n + FFN_FF_TILE], preferred_element_type=F32)
        u = jnp.dot(h, wu_ref[:, n:n + FFN_FF_TILE], preferred_element_type=F32)
        a = (_silu(g) * u).astype(BF16)
        y = y + jnp.dot(a, wd_ref[n:n + FFN_FF_TILE, :], preferred_element_type=F32)
    o_ref[...] = x + gate_ref[...] * y


def _ffn(x, sh, sc, gate, wg, wu, wd):
    t = x.shape[0]
    full = lambda shape: pl.BlockSpec(shape, lambda i: (0,) * len(shape))
    return pl.pallas_call(
        _ffn_kernel,
        grid=(t // ROW_TILE,),
        in_specs=[
            pl.BlockSpec((ROW_TILE, D_MODEL), lambda i: (i, 0)),
            _mod_spec(), _mod_spec(), _mod_spec(),
            full((D_MODEL, D_FF)), full((D_MODEL, D_FF)), full((D_FF, D_MODEL)),
        ],
        out_specs=pl.BlockSpec((ROW_TILE, D_MODEL), lambda i: (i, 0)),
        out_shape=jax.ShapeDtypeStruct((t, D_MODEL), F32),
        compiler_params=_cparams(("arbitrary",)),
        name="ffn",
    )(x, sh, sc, gate, wg, wu, wd)


def _qkv_kernel(x_ref, sh_ref, sc_ref, w_ref, cos_ref, sa_ref, sb_ref, q_ref, k_ref, v_ref):
    h = _modulated(x_ref[...], sh_ref[...], sc_ref[...]).astype(BF16)
    cos = cos_ref[...]
    sa = sa_ref[...]
    sb = sb_ref[...]

    def rope(z):
        return z * cos + pltpu.roll(z, LANES - 16, 1) * sa + pltpu.roll(z, 16, 1) * sb

    for n in range(0, D_MODEL, LANES):
        q = jnp.dot(h, w_ref[:, n:n + LANES], preferred_element_type=F32)
        q_ref[:, n:n + LANES] = (rope(q) * (DA_DH ** -0.5)).astype(BF16)
        k = jnp.dot(h, w_ref[:, D_MODEL + n:D_MODEL + n + LANES], preferred_element_type=F32)
        k_ref[:, n:n + LANES] = rope(k).astype(BF16)
    for n in range(0, D_MODEL, 512):
        v = jnp.dot(h, w_ref[:, 2 * D_MODEL + n:2 * D_MODEL + n + 512], preferred_element_type=F32)
        v_ref[:, n:n + 512] = v.astype(BF16)


def _qkv(x, sh, sc, w, cos, sa, sb):
    t = x.shape[0]
    row = lambda width: pl.BlockSpec((ROW_TILE, width), lambda i: (i, 0))
    return pl.pallas_call(
        _qkv_kernel,
        grid=(t // ROW_TILE,),
        in_specs=[row(D_MODEL), _mod_spec(), _mod_spec(),
                  pl.BlockSpec((D_MODEL, 3 * D_MODEL), lambda i: (0, 0)),
                  row(LANES), row(LANES), row(LANES)],
        out_specs=[row(D_MODEL)] * 3,
        out_shape=[jax.ShapeDtypeStruct((t, D_MODEL), BF16)] * 3,
        compiler_params=_cparams(("arbitrary",)),
        name="qkv",
    )(x, sh, sc, w, cos, sa, sb)


def _attn_kernel(q_ref, k_ref, v_ref, lam_ref, g_ref, o_ref, m_ref, l_ref, acc_ref, *,
                 lam_init, n_keys):
    tq = q_ref.shape[0]
    q = q_ref[...]
    lane = lax.broadcasted_iota(jnp.int32, q.shape, 1)
    zero = jnp.zeros_like(q)
    qq = jnp.concatenate([jnp.where(lane < DA_DH, q, zero), jnp.where(lane >= DA_DH, q, zero)], axis=0)

    m_ref[...] = jnp.full_like(m_ref, -jnp.inf)
    l_ref[...] = jnp.zeros_like(l_ref)
    acc_ref[...] = jnp.zeros_like(acc_ref)

    def step(off, size):
        k = k_ref[pl.ds(off, size), :]
        v = v_ref[pl.ds(off, size), :]
        s = lax.dot_general(qq, k, (((1,), (1,)), ((), ())), preferred_element_type=F32)
        m_old = m_ref[...]
        m_new = jnp.maximum(m_old, jnp.max(s, axis=-1, keepdims=True))
        alpha = jnp.exp(m_old - m_new)
        p = jnp.exp(s - m_new)
        l_ref[...] = alpha * l_ref[...] + jnp.sum(p, axis=-1, keepdims=True)
        acc_ref[...] = alpha * acc_ref[...] + jnp.dot(p.astype(BF16), v, preferred_element_type=F32)
        m_ref[...] = m_new

    is_ctx = pl.program_id(1) == 0

    @pl.when(is_ctx)
    def _():
        step(0, CTX_LEN)

    @pl.when(jnp.logical_not(is_ctx))
    def _():
        def body(j, carry):
            step(pl.multiple_of(j * ATT_KV_TILE, ATT_KV_TILE), ATT_KV_TILE)
            return carry
        lax.fori_loop(0, n_keys // ATT_KV_TILE, body, 0)

    lam = lam_ref[...]
    lam_full = (jnp.exp(jnp.sum(lam[0:1] * lam[1:2], axis=-1, keepdims=True))
                - jnp.exp(jnp.sum(lam[2:3] * lam[3:4], axis=-1, keepdims=True)) + lam_init)
    o_all = acc_ref[...] / l_ref[...]
    o = o_all[0:tq] - lam_full * o_all[tq:2 * tq]
    ms = jnp.mean(o * o, axis=-1, keepdims=True)
    o = o * lax.rsqrt(ms + 1e-5) * g_ref[...] * (1.0 - lam_init)
    o_ref[...] = o.astype(o_ref.dtype)


def _attention(q, k, v, lam, subln_g, lam_init):
    t = q.shape[0]
    head = lambda: pl.BlockSpec((t, DA_DV), lambda h, i: (0, h))
    return pl.pallas_call(
        functools.partial(_attn_kernel, lam_init=lam_init, n_keys=t),
        grid=(DA_HEADS, t // ROW_TILE),
        in_specs=[
            pl.BlockSpec((ROW_TILE, DA_DV), lambda h, i: (i, h)),
            head(), head(),
            pl.BlockSpec((4, DA_DH), lambda h, i: (0, 0)),
            pl.BlockSpec((1, DA_DV), lambda h, i: (0, 0)),
        ],
        out_specs=pl.BlockSpec((ROW_TILE, DA_DV), lambda h, i: (i, h)),
        out_shape=jax.ShapeDtypeStruct((t, D_MODEL), BF16),
        scratch_shapes=[pltpu.VMEM((2 * ROW_TILE, 1), F32), pltpu.VMEM((2 * ROW_TILE, 1), F32),
                        pltpu.VMEM((2 * ROW_TILE, DA_DV), F32)],
        compiler_params=_cparams(("arbitrary", "arbitrary")),
        name="attention",
    )(q, k, v, lam, subln_g.reshape(1, DA_DV))


def _proj_res_kernel(a_ref, w_ref, x_ref, gate_ref, o_ref):
    y = jnp.dot(a_ref[...], w_ref[...], preferred_element_type=F32)
    o_ref[...] = x_ref[...] + gate_ref[...] * y


def _proj_res(a, w, x, gate):
    t = x.shape[0]
    row = pl.BlockSpec((ROW_TILE, D_MODEL), lambda i: (i, 0))
    return pl.pallas_call(
        _proj_res_kernel,
        grid=(t // ROW_TILE,),
        in_specs=[row, pl.BlockSpec((D_MODEL, D_MODEL), lambda i: (0, 0)), row, _mod_spec()],
        out_specs=row,
        out_shape=jax.ShapeDtypeStruct((t, D_MODEL), F32),
        compiler_params=_cparams(("arbitrary",)),
        name="proj_res",
    )(a, w, x, gate)


def _moe_in_kernel(x_ref, sh_ref, sc_ref, wr_ref, h_ref, logit_ref):
    h = _modulated(x_ref[...], sh_ref[...], sc_ref[...])
    h_ref[...] = h.astype(BF16)
    logit_ref[...] = jnp.dot(h, wr_ref[...], preferred_element_type=F32,
                             precision=lax.Precision.HIGHEST)


def _moe_in(x, sh, sc, w_router_padded):
    t = x.shape[0]
    row = lambda width: pl.BlockSpec((ROW_TILE, width), lambda i: (i, 0))
    return pl.pallas_call(
        _moe_in_kernel,
        grid=(t // ROW_TILE,),
        in_specs=[row(D_MODEL), _mod_spec(), _mod_spec(),
                  pl.BlockSpec((D_MODEL, LANES), lambda i: (0, 0))],
        out_specs=[row(D_MODEL), row(LANES)],
        out_shape=[jax.ShapeDtypeStruct((t, D_MODEL), BF16), jax.ShapeDtypeStruct((t, LANES), F32)],
        compiler_params=_cparams(("arbitrary",)),
        name="moe_in",
    )(x, sh, sc, w_router_padded)


def _experts_kernel(blk_e_ref, n_used_ref, x_ref, wg_ref, wu_ref, wd_ref, o_ref):
    b = pl.program_id(0)
    f = pl.program_id(1)

    @pl.when(f == 0)
    def _():
        o_ref[...] = jnp.zeros_like(o_ref)

    @pl.when(b < n_used_ref[0])
    def _():
        x = x_ref[...]
        g = jnp.dot(x, wg_ref[...], preferred_element_type=F32)
        u = jnp.dot(x, wu_ref[...], preferred_element_type=F32)
        a = (_silu(g) * u).astype(BF16)
        o_ref[...] += jnp.dot(a, wd_ref[...], preferred_element_type=F32)


def _experts(xg, blk_e, n_used, wg, wu, wd):
    n_rows = xg.shape[0]
    n_blocks = n_rows // MOE_TILE
    n_ff = E_FF // MOE_FF_TILE
    ff = lambda b, f, nu: jnp.where(b < nu[0], f, n_ff - 1)
    grid_spec = pltpu.PrefetchScalarGridSpec(
        num_scalar_prefetch=2,
        grid=(n_blocks, n_ff),
        in_specs=[
            pl.BlockSpec((MOE_TILE, D_MODEL), lambda b, f, be, nu: (b, 0)),
            pl.BlockSpec((None, D_MODEL, MOE_FF_TILE), lambda b, f, be, nu: (be[b], 0, ff(b, f, nu))),
            pl.BlockSpec((None, D_MODEL, MOE_FF_TILE), lambda b, f, be, nu: (be[b], 0, ff(b, f, nu))),
            pl.BlockSpec((None, MOE_FF_TILE, D_MODEL), lambda b, f, be, nu: (be[b], ff(b, f, nu), 0)),
        ],
        out_specs=pl.BlockSpec((MOE_TILE, D_MODEL), lambda b, f, be, nu: (b, 0)),
    )
    return pl.pallas_call(
        _experts_kernel,
        grid_spec=grid_spec,
        out_shape=jax.ShapeDtypeStruct((n_rows, D_MODEL), F32),
        compiler_params=_cparams(("arbitrary", "arbitrary")),
        name="experts",
    )(blk_e, n_used, xg, wg, wu, wd)


def _combine_kernel(x_ref, gate_ref, y0_ref, y1_ref, w_ref, o_ref):
    w = w_ref[...]
    y = w[:, 0:1] * y0_ref[...] + w[:, 1:2] * y1_ref[...]
    o_ref[...] = x_ref[...] + gate_ref[...] * y


def _combine(x, gate, y0, y1, w):
    t = x.shape[0]
    row = pl.BlockSpec((ROW_TILE, D_MODEL), lambda i: (i, 0))
    return pl.pallas_call(
        _combine_kernel,
        grid=(t // ROW_TILE,),
        in_specs=[row, _mod_spec(), row, row, pl.BlockSpec((ROW_TILE, TOP_K), lambda i: (i, 0))],
        out_specs=row,
        out_shape=jax.ShapeDtypeStruct((t, D_MODEL), F32),
        compiler_params=_cparams(("arbitrary",)),
        name="moe_combine",
    )(x, gate, y0, y1, w)


def _moe(x, sh, sc, gate, w_router, wg, wu, wd):
    t = x.shape[0]
    wr = jnp.zeros((D_MODEL, LANES), F32).at[:, :N_EXPERTS].set(w_router)
    h, logits = _moe_in(x, sh, sc, wr)
    top_v, top_i = lax.top_k(logits[:, :N_EXPERTS], TOP_K)
    top_w = jax.nn.softmax(top_v, axis=-1)
    flat_e = top_i.reshape(-1)
    onehot = (flat_e[:, None] == jnp.arange(N_EXPERTS, dtype=flat_e.dtype)[None, :]).astype(jnp.int32)
    counts = jnp.sum(onehot, axis=0)
    rank = jnp.sum((jnp.cumsum(onehot, axis=0) - onehot) * onehot, axis=1)
    padded = ((counts + MOE_TILE - 1) // MOE_TILE) * MOE_TILE
    pad_end = jnp.cumsum(padded)
    pad_start = pad_end - padded
    dest = (pad_start[flat_e] + rank).astype(jnp.int32)
    n_rows = t * TOP_K + N_EXPERTS * MOE_TILE
    n_blocks = n_rows // MOE_TILE
    flat_tok = jnp.repeat(jnp.arange(t, dtype=jnp.int32), TOP_K)
    row_tok = jnp.zeros((n_rows,), jnp.int32).at[dest].set(flat_tok)
    blk_start = jnp.arange(n_blocks, dtype=jnp.int32) * MOE_TILE
    blk_e = jnp.minimum(jnp.sum(pad_end[None, :] <= blk_start[:, None], axis=1), N_EXPERTS - 1)
    n_used = (pad_end[-1:] // MOE_TILE).astype(jnp.int32)
    xg = jnp.take(h, row_tok, axis=0)
    out = _experts(xg, blk_e.astype(jnp.int32), n_used, wg, wu, wd)
    pos = dest.reshape(t, TOP_K)
    y0 = jnp.take(out, pos[:, 0], axis=0)
    y1 = jnp.take(out, pos[:, 1], axis=0)
    return _combine(x, gate, y0, y1, top_w)


def _final_kernel(x_ref, g_ref, o_ref):
    x = x_ref[...]
    ms = jnp.mean(x * x, axis=-1, keepdims=True)
    o_ref[...] = x * lax.rsqrt(ms + EPS) * g_ref[...]


def _final(x, g):
    n = x.shape[0] - CTX_LEN
    return pl.pallas_call(
        _final_kernel,
        grid=(n // ROW_TILE,),
        in_specs=[pl.BlockSpec((ROW_TILE, D_MODEL), lambda i: (i + 1, 0)),
                  pl.BlockSpec((1, D_MODEL), lambda i: (0, 0))],
        out_specs=pl.BlockSpec((ROW_TILE, D_MODEL), lambda i: (i, 0)),
        out_shape=jax.ShapeDtypeStruct((n, D_MODEL), F32),
        compiler_params=_cparams(("arbitrary",)),
        name="final_norm",
    )(x, g.reshape(1, D_MODEL))


def _rope_tables(n):
    pos = jnp.arange(n, dtype=jnp.int32)
    row = (pos // GRID_W).astype(F32)
    col = (pos % GRID_W).astype(F32)
    inv = 1.0 / (ROPE_BASE ** (jnp.arange(0, ROPE_AXIS, 2, dtype=F32) / ROPE_AXIS))
    ar = row[:, None] * inv
    ac = col[:, None] * inv
    ang = jnp.concatenate([ar, ar, ac, ac], axis=-1)
    cos = jnp.cos(ang)
    sin = jnp.sin(ang)
    first = (jnp.arange(DA_DH) % ROPE_AXIS) < (ROPE_AXIS // 2)
    sa = jnp.where(first, -sin, 0.0)
    sb = jnp.where(first, 0.0, sin)
    pad = lambda a, fill: jnp.concatenate(
        [jnp.full((CTX_LEN, LANES), fill, F32), jnp.tile(a, (1, LANES // DA_DH))], axis=0)
    return pad(cos, 1.0), pad(sa, 0.0), pad(sb, 0.0)


def kernel(x, c, ctx, c_ctx, mod_w, mod_b, hg_lb, ev_w_in, hg_norm_g, sg_ln_g, sg_ln_b, sg_w, sg_b, ev_w_out, da_w_qkv, da_lam, da_subln_g, da_w_out, ffn_w_gate, ffn_w_up, ffn_w_down, moe_w_router, moe_w_gate, moe_w_up, moe_w_down, final_g):
    n = x.shape[1]
    xs = jnp.concatenate([ctx[0], x[0]], axis=0)
    cc_t = jnp.stack([c_ctx, c[0]], axis=1)
    mods = _ada_all(cc_t, mod_w, mod_b).reshape(DEPTH, 2, 6, 1, D_MODEL)
    cos, sa, sb = _rope_tables(n)
    lb_sm = jax.nn.softmax(hg_lb.astype(F32), axis=1)
    lb_all = jnp.cumsum(lb_sm, axis=1) - lb_sm[:, :1]

    for layer in range(DEPTH):
        j = layer // 2
        sh1, sc1, g1, sh2, sc2, g2 = (mods[layer, :, i] for i in range(6))
        if layer % 2 == 0:
            y = _even_in(xs, sh1, sc1, ev_w_in[j].astype(BF16))
            o_f = _hgrn(y, lb_all[0, j], rev=False)
            a = _hgrn(y, lb_all[1, j], rev=True, o_fwd=o_f, gain=hg_norm_g[j])
            xs = _even_out(a, y, sg_ln_g[j], sg_ln_b[j], sg_w[j].astype(BF16), sg_b[j].T,
                           ev_w_out[j].astype(BF16), xs, g1)
            xs = _ffn(xs, sh2, sc2, g2, ffn_w_gate[j].astype(BF16), ffn_w_up[j].astype(BF16),
                      ffn_w_down[j].astype(BF16))
        else:
            lam_init = 0.8 - 0.6 * math.exp(-0.3 * layer)
            q, k, v = _qkv(xs, sh1, sc1, da_w_qkv[j].astype(BF16), cos, sa, sb)
            a = _attention(q, k, v, da_lam[j], da_subln_g[j], lam_init)
            xs = _proj_res(a, da_w_out[j].astype(BF16), xs, g1)
            xs = _moe(xs, sh2, sc2, g2, moe_w_router[j], moe_w_gate[j].astype(BF16),
                      moe_w_up[j].astype(BF16), moe_w_down[j].astype(BF16))
    return _final(xs, final_g)[None]
```

```python
import functools
import math

import jax
import jax.numpy as jnp
from jax import lax
from jax.experimental import pallas as pl
from jax.experimental.pallas import tpu as pltpu

F32 = jnp.float32
BF16 = jnp.bfloat16

D_MODEL = 1024
DEPTH = 4
GRID_W = 64
CTX_LEN = 256
EPS = 1e-6

HG_WIDTH = 512
HG_HEADS = 4
HG_D = 128
HG_SUB = 16
SG_WIDTH = 512
SG_CHUNK = 128
SG_GROUPS = 4
EVEN_IN = 5 * HG_WIDTH + 2 * SG_WIDTH

DA_HEADS = 8
DA_DH = 64
DA_DV = 128
ROPE_BASE = 10000.0
ROPE_AXIS = DA_DH // 2

D_FF = 2816
N_EXPERTS = 8
TOP_K = 2
E_FF = 3584

LOG2E = 1.4426950408889634
LANES = 128
ROW_TILE = CTX_LEN
ATT_KV_TILE = 1280
ATT_UNROLL = 4
MOE_TILE = 512
MOE_FF_TILE = 512
FFN_FF_TILE = 256
VMEM_LIMIT = 56 * 1024 * 1024


def _cparams(sem):
    return pltpu.CompilerParams(dimension_semantics=sem, vmem_limit_bytes=VMEM_LIMIT)


def _sigmoid(x):
    return 1.0 / (1.0 + jnp.exp(-x))


def _silu(x):
    return x * _sigmoid(x)


def _gelu(x):
    return 0.5 * x * (1.0 + lax.erf(x * (2.0 ** -0.5)))


def _modulated(x, sh, sc):
    ms = jnp.mean(x * x, axis=-1, keepdims=True)
    return x * lax.rsqrt(ms + EPS) * (1.0 + sc) + sh


def _mod_spec(width=D_MODEL):
    return pl.BlockSpec((None, 1, width), lambda i, *_: (jnp.minimum(i, 1), 0, 0))


def _ada_kernel(c_ref, w_ref, b_ref, o_ref):
    s = _silu(c_ref[...])
    w = w_ref[...]
    r0 = jnp.sum(s[:, 0:1] * w, axis=0, keepdims=True)
    r1 = jnp.sum(s[:, 1:2] * w, axis=0, keepdims=True)
    o_ref[0:1, :] = r0 + b_ref[...]
    o_ref[1:2, :] = r1 + b_ref[...]


def _ada_all(cc_t, mod_w, mod_b):
    n = 6 * D_MODEL
    tn = 1536
    return pl.pallas_call(
        _ada_kernel,
        grid=(DEPTH, n // tn),
        in_specs=[
            pl.BlockSpec((D_MODEL, 2), lambda l, j: (0, 0)),
            pl.BlockSpec((None, D_MODEL, tn), lambda l, j: (l, 0, j)),
            pl.BlockSpec((None, 1, tn), lambda l, j: (l, 0, j)),
        ],
        out_specs=pl.BlockSpec((None, 2, tn), lambda l, j: (l, 0, j)),
        out_shape=jax.ShapeDtypeStruct((DEPTH, 2, n), F32),
        compiler_params=_cparams(("arbitrary", "arbitrary")),
        name="ada",
    )(cc_t, mod_w, mod_b.reshape(DEPTH, 1, n))


def _even_in_kernel(x_ref, sh_ref, sc_ref, w_ref, o_ref):
    h = _modulated(x_ref[...], sh_ref[...], sc_ref[...]).astype(BF16)
    for n in range(0, EVEN_IN, 512):
        o_ref[:, n:n + 512] = jnp.dot(h, w_ref[:, n:n + 512], preferred_element_type=F32)


def _even_in(x, sh, sc, w):
    t = x.shape[0]
    return pl.pallas_call(
        _even_in_kernel,
        grid=(t // ROW_TILE,),
        in_specs=[
            pl.BlockSpec((ROW_TILE, D_MODEL), lambda i: (i, 0)),
            _mod_spec(), _mod_spec(),
            pl.BlockSpec((D_MODEL, EVEN_IN), lambda i: (0, 0)),
        ],
        out_specs=pl.BlockSpec((ROW_TILE, EVEN_IN), lambda i: (i, 0)),
        out_shape=jax.ShapeDtypeStruct((t, EVEN_IN), F32),
        compiler_params=_cparams(("arbitrary",)),
        name="even_in",
    )(x, sh, sc, w)


def _group_scan(x, r_in_group, forward):
    rows = x.shape[0]
    for sh in (1, 2, 4, 8):
        if forward:
            rolled = pltpu.roll(x, sh, 0)
            ok = r_in_group >= sh
        else:
            rolled = pltpu.roll(x, rows - sh, 0)
            ok = r_in_group < HG_SUB - sh
        x = x + jnp.where(ok, rolled, 0.0)
    return x


def _hgrn_kernel(*refs, rev, readout):
    if readout:
        q_ref, f_ref, i_ref, lb_ref, of_ref, g_ref, gain_ref, o_ref, st_ref = refs
    else:
        q_ref, f_ref, i_ref, lb_ref, o_ref, st_ref = refs
    rows = q_ref.shape[0]
    n_groups = rows // HG_SUB

    @pl.when(pl.program_id(1) == 0)
    def _():
        st_ref[...] = jnp.zeros_like(st_ref)

    f = f_ref[...]
    lb = lb_ref[...]
    log_f = jnp.log(lb + (1.0 - lb) * _sigmoid(f))
    kk = (1.0 - lb) * _sigmoid(-f)
    qs = _silu(q_ref[...])
    v = i_ref[...]

    row = lax.broadcasted_iota(jnp.int32, (rows, HG_D), 0)
    r16 = row & (HG_SUB - 1)
    grp = row // HG_SUB
    pfx = _group_scan(log_f, r16, True)
    sfx = _group_scan(log_f, r16, False)
    cum, other = (sfx, pfx) if rev else (pfx, sfx)
    qd = (qs * jnp.exp(cum)).astype(BF16)
    kd = (kk * jnp.exp(other - log_f)).astype(BF16)

    ones = jnp.ones((HG_D, HG_D), BF16)
    acc = jnp.zeros((rows, HG_D), F32)
    for d in range(HG_SUB):
        if d == 0:
            kk_s, cum_s, v_s = kk, cum, v
            p = qs * kk_s
        else:
            shift = rows - d if rev else d
            kk_s = pltpu.roll(kk, shift, 0)
            cum_s = pltpu.roll(cum, shift, 0)
            v_s = pltpu.roll(v, shift, 0)
            ok = (r16 < HG_SUB - d) if rev else (r16 >= d)
            p = jnp.where(ok, qs * kk_s * jnp.exp(cum - cum_s), 0.0)
        rsum = jnp.dot(p.astype(BF16), ones, preferred_element_type=F32)
        acc = acc + rsum * v_s

    v_t = v.T.astype(BF16)
    bd = jnp.concatenate(
        [jnp.where(grp == g, kd, jnp.zeros_like(kd)) for g in range(n_groups)], axis=1)
    u = jnp.dot(v_t, bd, preferred_element_type=F32)

    st = st_ref[...]
    outs = [None] * n_groups
    for g in (range(n_groups - 1, -1, -1) if rev else range(n_groups)):
        lo = g * HG_SUB
        o_inter = lax.dot_general(qd[lo:lo + HG_SUB], st.astype(BF16),
                                  (((1,), (1,)), ((), ())), preferred_element_type=F32)
        outs[g] = acc[lo:lo + HG_SUB] + o_inter
        edge = lo if rev else lo + HG_SUB - 1
        st = st * jnp.exp(cum[edge:edge + 1, :]) + u[:, g * HG_D:(g + 1) * HG_D]
    st_ref[...] = st
    o = jnp.concatenate(outs, axis=0)

    if readout:
        o = o + of_ref[...]
        ms = jnp.mean(o * o, axis=-1, keepdims=True)
        o = o * lax.rsqrt(ms + EPS) * gain_ref[...] * _silu(g_ref[...])
    o_ref[...] = o.astype(o_ref.dtype)


def _hgrn(y, lb, rev, o_fwd=None, gain=None):
    t = y.shape[0]
    nblk = t // ROW_TILE
    readout = o_fwd is not None
    if rev:
        blk = lambda c: jnp.where(c == 0, 0, nblk - c)
    else:
        blk = lambda c: c
    col = lambda slab: pl.BlockSpec((ROW_TILE, HG_D), lambda h, c: (blk(c), slab * HG_HEADS + h))
    in_specs = [col(0), col(2 if rev else 1), col(3),
                pl.BlockSpec((None, 1, HG_D), lambda h, c: (h, 0, 0))]
    args = [y, y, y, lb.reshape(HG_HEADS, 1, HG_D)]
    if readout:
        in_specs += [pl.BlockSpec((ROW_TILE, HG_D), lambda h, c: (blk(c), h)), col(4),
                     pl.BlockSpec((1, HG_D), lambda h, c: (0, 0))]
        args += [o_fwd, y, gain.reshape(1, HG_D)]
    return pl.pallas_call(
        functools.partial(_hgrn_kernel, rev=rev, readout=readout),
        grid=(HG_HEADS, nblk),
        in_specs=in_specs,
        out_specs=pl.BlockSpec((ROW_TILE, HG_D), lambda h, c: (blk(c), h)),
        out_shape=jax.ShapeDtypeStruct((t, HG_WIDTH), BF16 if readout else F32),
        scratch_shapes=[pltpu.VMEM((HG_D, HG_D), F32)],
        compiler_params=_cparams(("arbitrary", "arbitrary")),
        name="hgrn_bwd" if rev else "hgrn_fwd",
    )(*args)


def _even_out_kernel(a_ref, u_ref, v_ref, lng_ref, lnb_ref, sgw_ref, sgb_ref, w_ref, x_ref,
                     gate_ref, o_ref):
    u = _gelu(u_ref[...])
    v = _gelu(v_ref[...])
    mu = jnp.mean(v, axis=-1, keepdims=True)
    var = jnp.mean(jnp.square(v - mu), axis=-1, keepdims=True)
    vn = ((v - mu) * lax.rsqrt(var + 1e-5) * lng_ref[...] + lnb_ref[...]).astype(BF16)
    rows = u.shape[0]
    mixed_rows = []
    for c in range(rows // SG_CHUNK):
        lo = c * SG_CHUNK
        parts = []
        for g in range(SG_GROUPS):
            m = jnp.dot(sgw_ref[g], vn[lo:lo + SG_CHUNK, g * LANES:(g + 1) * LANES],
                        preferred_element_type=F32)
            parts.append(m + sgb_ref[:, g:g + 1])
        mixed_rows.append(jnp.concatenate(parts, axis=1))
    s = (u * jnp.concatenate(mixed_rows, axis=0)).astype(BF16)
    y = jnp.dot(a_ref[...], w_ref[0:HG_WIDTH, :], preferred_element_type=F32)
    y = y + jnp.dot(s, w_ref[HG_WIDTH:, :], preferred_element_type=F32)
    o_ref[...] = x_ref[...] + gate_ref[...] * y


def _even_out(a, y_proj, ln_g, ln_b, sg_w, sg_b_t, w_out, x, gate):
    t = x.shape[0]
    full = lambda shape: pl.BlockSpec(shape, lambda i: (0,) * len(shape))
    return pl.pallas_call(
        _even_out_kernel,
        grid=(t // ROW_TILE,),
        in_specs=[
            pl.BlockSpec((ROW_TILE, HG_WIDTH), lambda i: (i, 0)),
            pl.BlockSpec((ROW_TILE, SG_WIDTH), lambda i: (i, 5)),
            pl.BlockSpec((ROW_TILE, SG_WIDTH), lambda i: (i, 6)),
            full((1, SG_WIDTH)), full((1, SG_WIDTH)),
            full((SG_GROUPS, SG_CHUNK, SG_CHUNK)), full((SG_CHUNK, SG_GROUPS)),
            full((D_MODEL, D_MODEL)),
            pl.BlockSpec((ROW_TILE, D_MODEL), lambda i: (i, 0)),
            _mod_spec(),
        ],
        out_specs=pl.BlockSpec((ROW_TILE, D_MODEL), lambda i: (i, 0)),
        out_shape=jax.ShapeDtypeStruct((t, D_MODEL), F32),
        compiler_params=_cparams(("arbitrary",)),
        name="even_out",
    )(a, y_proj, y_proj, ln_g.reshape(1, -1), ln_b.reshape(1, -1), sg_w, sg_b_t, w_out, x, gate)


def _ffn_kernel(x_ref, sh_ref, sc_ref, gate_ref, wg_ref, wu_ref, wd_ref, o_ref):
    x = x_ref[...]
    h = _modulated(x, sh_ref[...], sc_ref[...]).astype(BF16)
    y = jnp.zeros(x.shape, F32)
    for n in range(0, D_FF, FFN_FF_TILE):
        g = jnp.dot(h, wg_ref[:, n:n + FFN_FF_TILE], preferred_element_type=F32)
        u = jnp.dot(h, wu_ref[:, n:n + FFN_FF_TILE], preferred_element_type=F32)
        a = (_silu(g) * u).astype(BF16)
        y = y + jnp.dot(a, wd_ref[n:n + FFN_FF_TILE, :], preferred_element_type=F32)
    o_ref[...] = x + gate_ref[...] * y


def _ffn(x, sh, sc, gate, wg, wu, wd):
    t = x.shape[0]
    full = lambda shape: pl.BlockSpec(shape, lambda i: (0,) * len(shape))
    return pl.pallas_call(
        _ffn_kernel,
        grid=(t // ROW_TILE,),
        in_specs=[
            pl.BlockSpec((ROW_TILE, D_MODEL), lambda i: (i, 0)),
            _mod_spec(), _mod_spec(), _mod_spec(),
            full((D_MODEL, D_FF)), full((D_MODEL, D_FF)), full((D_FF, D_MODEL)),
        ],
        out_specs=pl.BlockSpec((ROW_TILE, D_MODEL), lambda i: (i, 0)),
        out_shape=jax.ShapeDtypeStruct((t, D_MODEL), F32),
        compiler_params=_cparams(("arbitrary",)),
        name="ffn",
    )(x, sh, sc, gate, wg, wu, wd)


def _qkv_kernel(x_ref, sh_ref, sc_ref, w_ref, cos_ref, sa_ref, sb_ref, q_ref, k_ref, vt_ref):
    h = _modulated(x_ref[...], sh_ref[...], sc_ref[...]).astype(BF16)
    cos = cos_ref[...]
    sa = sa_ref[...]
    sb = sb_ref[...]

    def rope(z):
        return z * cos + pltpu.roll(z, LANES - 16, 1) * sa + pltpu.roll(z, 16, 1) * sb

    for n in range(0, D_MODEL, LANES):
        q = jnp.dot(h, w_ref[:, n:n + LANES], preferred_element_type=F32)
        q_ref[:, n:n + LANES] = (rope(q) * (DA_DH ** -0.5 * LOG2E)).astype(BF16)
        k = jnp.dot(h, w_ref[:, D_MODEL + n:D_MODEL + n + LANES], preferred_element_type=F32)
        k_ref[:, n:n + LANES] = rope(k).astype(BF16)
    for n in range(0, D_MODEL, 512):
        v = jnp.dot(h, w_ref[:, 2 * D_MODEL + n:2 * D_MODEL + n + 512], preferred_element_type=F32)
        vt_ref[n:n + 512, :] = v.T.astype(BF16)


def _qkv(x, sh, sc, w, cos, sa, sb):
    t = x.shape[0]
    row = lambda width: pl.BlockSpec((ROW_TILE, width), lambda i: (i, 0))
    return pl.pallas_call(
        _qkv_kernel,
        grid=(t // ROW_TILE,),
        in_specs=[row(D_MODEL), _mod_spec(), _mod_spec(),
                  pl.BlockSpec((D_MODEL, 3 * D_MODEL), lambda i: (0, 0)),
                  row(LANES), row(LANES), row(LANES)],
        out_specs=[row(D_MODEL), row(D_MODEL), pl.BlockSpec((D_MODEL, ROW_TILE), lambda i: (0, i))],
        out_shape=[jax.ShapeDtypeStruct((t, D_MODEL), BF16), jax.ShapeDtypeStruct((t, D_MODEL), BF16),
                   jax.ShapeDtypeStruct((D_MODEL, t), BF16)],
        compiler_params=_cparams(("arbitrary",)),
        name="qkv",
    )(x, sh, sc, w, cos, sa, sb)


def _attn_kernel(q_ref, k_ref, vt_ref, lam_ref, g_ref, o_ref, m_ref, l_ref, acc_ref,
                 s0_ref, x0_ref, s1_ref, x1_ref, *, lam_init, n_keys):
    tq = q_ref.shape[0]
    q = q_ref[...]
    lane = lax.broadcasted_iota(jnp.int32, q.shape, 1)
    zero = jnp.zeros_like(q)
    qq = jnp.concatenate([jnp.where(lane < DA_DH, q, zero), jnp.where(lane >= DA_DH, q, zero)], axis=0)

    m_ref[...] = jnp.full_like(m_ref, -jnp.inf)
    l_ref[...] = jnp.zeros_like(l_ref)
    acc_ref[...] = jnp.zeros_like(acc_ref)

    def scores(off, size):
        k = k_ref[pl.ds(off, size), :]
        s = lax.dot_general(k, qq, (((1,), (1,)), ((), ())), preferred_element_type=F32)
        return s, jnp.max(s, axis=0, keepdims=True)

    def accumulate(s, s_max, off, size):
        vt = vt_ref[:, pl.ds(off, size)]
        m_old = m_ref[...]
        m_new = jnp.maximum(m_old, s_max)
        alpha = jnp.exp2(m_old - m_new)
        p = jnp.exp2(s - m_new)
        l_ref[...] = alpha * l_ref[...] + jnp.sum(p, axis=0, keepdims=True)
        acc_ref[...] = alpha * acc_ref[...] + jnp.dot(vt, p.astype(BF16), preferred_element_type=F32)
        m_ref[...] = m_new

    is_ctx = pl.program_id(1) == 0

    @pl.when(is_ctx)
    def _():
        s, s_max = scores(0, CTX_LEN)
        accumulate(s, s_max, 0, CTX_LEN)

    @pl.when(jnp.logical_not(is_ctx))
    def _():
        tk = ATT_KV_TILE
        n_chunks = n_keys // tk
        bufs = ((s0_ref, x0_ref), (s1_ref, x1_ref))

        def stage(c, cur, nxt):
            off = pl.multiple_of(c * tk, tk)
            s_next, x_next = scores(off + tk, tk)
            nxt[0][...] = s_next
            nxt[1][...] = x_next
            accumulate(cur[0][...], cur[1][...], off, tk)

        s_first, x_first = scores(0, tk)
        s0_ref[...] = s_first
        x0_ref[...] = x_first

        def body(jj, carry):
            for u in range(ATT_UNROLL):
                stage(ATT_UNROLL * jj + u, bufs[u % 2], bufs[(u + 1) % 2])
            return carry
        lax.fori_loop(0, (n_chunks - 1) // ATT_UNROLL, body, 0)
        accumulate(s0_ref[...], x0_ref[...], (n_chunks - 1) * tk, tk)

    lam = lam_ref[...]
    lam_full = (jnp.exp(jnp.sum(lam[0:1] * lam[1:2], axis=-1, keepdims=True))
                - jnp.exp(jnp.sum(lam[2:3] * lam[3:4], axis=-1, keepdims=True)) + lam_init)
    o_all = acc_ref[...] / l_ref[...]
    o = (o_all[:, 0:tq] - lam_full * o_all[:, tq:2 * tq]).T
    ms = jnp.mean(o * o, axis=-1, keepdims=True)
    o = o * lax.rsqrt(ms + 1e-5) * g_ref[...] * (1.0 - lam_init)
    o_ref[...] = o.astype(o_ref.dtype)


def _attention(q, k, vt, lam, subln_g, lam_init):
    t = q.shape[0]
    assert t % ATT_KV_TILE == 0 and (t // ATT_KV_TILE - 1) % ATT_UNROLL == 0
    s_buf = pltpu.VMEM((ATT_KV_TILE, 2 * ROW_TILE), F32)
    x_buf = pltpu.VMEM((1, 2 * ROW_TILE), F32)
    return pl.pallas_call(
        functools.partial(_attn_kernel, lam_init=lam_init, n_keys=t),
        grid=(DA_HEADS, t // ROW_TILE),
        in_specs=[
            pl.BlockSpec((ROW_TILE, DA_DV), lambda h, i: (i, h)),
            pl.BlockSpec((t, DA_DV), lambda h, i: (0, h)),
            pl.BlockSpec((DA_DV, t), lambda h, i: (h, 0)),
            pl.BlockSpec((4, DA_DH), lambda h, i: (0, 0)),
            pl.BlockSpec((1, DA_DV), lambda h, i: (0, 0)),
        ],
        out_specs=pl.BlockSpec((ROW_TILE, DA_DV), lambda h, i: (i, h)),
        out_shape=jax.ShapeDtypeStruct((t, D_MODEL), BF16),
        scratch_shapes=[pltpu.VMEM((1, 2 * ROW_TILE), F32), pltpu.VMEM((1, 2 * ROW_TILE), F32),
                        pltpu.VMEM((DA_DV, 2 * ROW_TILE), F32), s_buf, x_buf, s_buf, x_buf],
        compiler_params=_cparams(("arbitrary", "arbitrary")),
        name="attention",
    )(q, k, vt, lam, subln_g.reshape(1, DA_DV))


def _proj_res_kernel(a_ref, w_ref, x_ref, gate_ref, o_ref):
    y = jnp.dot(a_ref[...], w_ref[...], preferred_element_type=F32)
    o_ref[...] = x_ref[...] + gate_ref[...] * y


def _proj_res(a, w, x, gate):
    t = x.shape[0]
    row = pl.BlockSpec((ROW_TILE, D_MODEL), lambda i: (i, 0))
    return pl.pallas_call(
        _proj_res_kernel,
        grid=(t // ROW_TILE,),
        in_specs=[row, pl.BlockSpec((D_MODEL, D_MODEL), lambda i: (0, 0)), row, _mod_spec()],
        out_specs=row,
        out_shape=jax.ShapeDtypeStruct((t, D_MODEL), F32),
        compiler_params=_cparams(("arbitrary",)),
        name="proj_res",
    )(a, w, x, gate)


def _moe_in_kernel(x_ref, sh_ref, sc_ref, wr_ref, h_ref, logit_ref):
    h = _modulated(x_ref[...], sh_ref[...], sc_ref[...])
    h_ref[...] = h.astype(BF16)
    logit_ref[...] = jnp.dot(h, wr_ref[...], preferred_element_type=F32,
                             precision=lax.Precision.HIGHEST)


def _moe_in(x, sh, sc, w_router_padded):
    t = x.shape[0]
    row = lambda width: pl.BlockSpec((ROW_TILE, width), lambda i: (i, 0))
    return pl.pallas_call(
        _moe_in_kernel,
        grid=(t // ROW_TILE,),
        in_specs=[row(D_MODEL), _mod_spec(), _mod_spec(),
                  pl.BlockSpec((D_MODEL, LANES), lambda i: (0, 0))],
        out_specs=[row(D_MODEL), row(LANES)],
        out_shape=[jax.ShapeDtypeStruct((t, D_MODEL), BF16), jax.ShapeDtypeStruct((t, LANES), F32)],
        compiler_params=_cparams(("arbitrary",)),
        name="moe_in",
    )(x, sh, sc, w_router_padded)


def _experts_kernel(blk_e_ref, n_used_ref, x_ref, wg_ref, wu_ref, wd_ref, o_ref):
    b = pl.program_id(0)
    f = pl.program_id(1)

    @pl.when(f == 0)
    def _():
        o_ref[...] = jnp.zeros_like(o_ref)

    @pl.when(b < n_used_ref[0])
    def _():
        x = x_ref[...]
        g = jnp.dot(x, wg_ref[...].astype(BF16), preferred_element_type=F32)
        u = jnp.dot(x, wu_ref[...].astype(BF16), preferred_element_type=F32)
        a = (_silu(g) * u).astype(BF16)
        o_ref[...] += jnp.dot(a, wd_ref[...].astype(BF16), preferred_element_type=F32)


def _experts(xg, blk_e, n_used, wg, wu, wd, j):
    n_rows = xg.shape[0]
    n_blocks = n_rows // MOE_TILE
    n_ff = E_FF // MOE_FF_TILE
    ff = lambda b, f, nu: jnp.where(b < nu[0], f, n_ff - 1)
    grid_spec = pltpu.PrefetchScalarGridSpec(
        num_scalar_prefetch=2,
        grid=(n_blocks, n_ff),
        in_specs=[
            pl.BlockSpec((MOE_TILE, D_MODEL), lambda b, f, be, nu: (b, 0)),
            pl.BlockSpec((None, None, D_MODEL, MOE_FF_TILE),
                         lambda b, f, be, nu: (j, be[b], 0, ff(b, f, nu))),
            pl.BlockSpec((None, None, D_MODEL, MOE_FF_TILE),
                         lambda b, f, be, nu: (j, be[b], 0, ff(b, f, nu))),
            pl.BlockSpec((None, None, MOE_FF_TILE, D_MODEL),
                         lambda b, f, be, nu: (j, be[b], ff(b, f, nu), 0)),
        ],
        out_specs=pl.BlockSpec((MOE_TILE, D_MODEL), lambda b, f, be, nu: (b, 0)),
    )
    return pl.pallas_call(
        _experts_kernel,
        grid_spec=grid_spec,
        out_shape=jax.ShapeDtypeStruct((n_rows, D_MODEL), F32),
        compiler_params=_cparams(("arbitrary", "arbitrary")),
        name="experts",
    )(blk_e, n_used, xg, wg, wu, wd)


def _combine_kernel(x_ref, gate_ref, y0_ref, y1_ref, w_ref, o_ref):
    w = w_ref[...]
    y = w[:, 0:1] * y0_ref[...] + w[:, 1:2] * y1_ref[...]
    o_ref[...] = x_ref[...] + gate_ref[...] * y


def _combine(x, gate, y0, y1, w):
    t = x.shape[0]
    row = pl.BlockSpec((ROW_TILE, D_MODEL), lambda i: (i, 0))
    return pl.pallas_call(
        _combine_kernel,
        grid=(t // ROW_TILE,),
        in_specs=[row, _mod_spec(), row, row, pl.BlockSpec((ROW_TILE, TOP_K), lambda i: (i, 0))],
        out_specs=row,
        out_shape=jax.ShapeDtypeStruct((t, D_MODEL), F32),
        compiler_params=_cparams(("arbitrary",)),
        name="moe_combine",
    )(x, gate, y0, y1, w)


def _moe(x, sh, sc, gate, w_router, wg, wu, wd, j):
    t = x.shape[0]
    wr = jnp.zeros((D_MODEL, LANES), F32).at[:, :N_EXPERTS].set(w_router)
    h, logits = _moe_in(x, sh, sc, wr)
    top_v, top_i = lax.top_k(logits[:, :N_EXPERTS], TOP_K)
    top_w = jax.nn.softmax(top_v, axis=-1)
    flat_e = top_i.reshape(-1)
    onehot = (flat_e[:, None] == jnp.arange(N_EXPERTS, dtype=flat_e.dtype)[None, :]).astype(jnp.int32)
    counts = jnp.sum(onehot, axis=0)
    rank = jnp.sum((jnp.cumsum(onehot, axis=0) - onehot) * onehot, axis=1)
    padded = ((counts + MOE_TILE - 1) // MOE_TILE) * MOE_TILE
    pad_end = jnp.cumsum(padded)
    pad_start = pad_end - padded
    dest = (pad_start[flat_e] + rank).astype(jnp.int32)
    n_rows = t * TOP_K + N_EXPERTS * MOE_TILE
    n_blocks = n_rows // MOE_TILE
    flat_tok = jnp.repeat(jnp.arange(t, dtype=jnp.int32), TOP_K)
    row_tok = jnp.zeros((n_rows,), jnp.int32).at[dest].set(flat_tok)
    blk_start = jnp.arange(n_blocks, dtype=jnp.int32) * MOE_TILE
    blk_e = jnp.minimum(jnp.sum(pad_end[None, :] <= blk_start[:, None], axis=1), N_EXPERTS - 1)
    n_used = (pad_end[-1:] // MOE_TILE).astype(jnp.int32)
    xg = jnp.take(h, row_tok, axis=0)
    out = _experts(xg, blk_e.astype(jnp.int32), n_used, wg, wu, wd, j)
    pos = dest.reshape(t, TOP_K)
    y0 = jnp.take(out, pos[:, 0], axis=0)
    y1 = jnp.take(out, pos[:, 1], axis=0)
    return _combine(x, gate, y0, y1, top_w)


def _final_kernel(x_ref, g_ref, o_ref):
    x = x_ref[...]
    ms = jnp.mean(x * x, axis=-1, keepdims=True)
    o_ref[...] = x * lax.rsqrt(ms + EPS) * g_ref[...]


def _final(x, g):
    n = x.shape[0] - CTX_LEN
    return pl.pallas_call(
        _final_kernel,
        grid=(n // ROW_TILE,),
        in_specs=[pl.BlockSpec((ROW_TILE, D_MODEL), lambda i: (i + 1, 0)),
                  pl.BlockSpec((1, D_MODEL), lambda i: (0, 0))],
        out_specs=pl.BlockSpec((ROW_TILE, D_MODEL), lambda i: (i, 0)),
        out_shape=jax.ShapeDtypeStruct((n, D_MODEL), F32),
        compiler_params=_cparams(("arbitrary",)),
        name="final_norm",
    )(x, g.reshape(1, D_MODEL))


def _rope_tables(n):
    pos = jnp.arange(n, dtype=jnp.int32)
    row = (pos // GRID_W).astype(F32)
    col = (pos % GRID_W).astype(F32)
    inv = 1.0 / (ROPE_BASE ** (jnp.arange(0, ROPE_AXIS, 2, dtype=F32) / ROPE_AXIS))
    ar = row[:, None] * inv
    ac = col[:, None] * inv
    ang = jnp.concatenate([ar, ar, ac, ac], axis=-1)
    cos = jnp.cos(ang)
    sin = jnp.sin(ang)
    first = (jnp.arange(DA_DH) % ROPE_AXIS) < (ROPE_AXIS // 2)
    sa = jnp.where(first, -sin, 0.0)
    sb = jnp.where(first, 0.0, sin)
    pad = lambda a, fill: jnp.concatenate(
        [jnp.full((CTX_LEN, LANES), fill, F32), jnp.tile(a, (1, LANES // DA_DH))], axis=0)
    return pad(cos, 1.0), pad(sa, 0.0), pad(sb, 0.0)


def kernel(x, c, ctx, c_ctx, mod_w, mod_b, hg_lb, ev_w_in, hg_norm_g, sg_ln_g, sg_ln_b, sg_w, sg_b, ev_w_out, da_w_qkv, da_lam, da_subln_g, da_w_out, ffn_w_gate, ffn_w_up, ffn_w_down, moe_w_router, moe_w_gate, moe_w_up, moe_w_down, final_g):
    n = x.shape[1]
    xs = jnp.concatenate([ctx[0], x[0]], axis=0)
    cc_t = jnp.stack([c_ctx, c[0]], axis=1)
    mods = _ada_all(cc_t, mod_w, mod_b).reshape(DEPTH, 2, 6, 1, D_MODEL)
    cos, sa, sb = _rope_tables(n)
    lb_sm = jax.nn.softmax(hg_lb.astype(F32), axis=1)
    lb_all = jnp.cumsum(lb_sm, axis=1) - lb_sm[:, :1]

    for layer in range(DEPTH):
        j = layer // 2
        sh1, sc1, g1, sh2, sc2, g2 = (mods[layer, :, i] for i in range(6))
        if layer % 2 == 0:
            y = _even_in(xs, sh1, sc1, ev_w_in[j].astype(BF16))
            o_f = _hgrn(y, lb_all[0, j], rev=False)
            a = _hgrn(y, lb_all[1, j], rev=True, o_fwd=o_f, gain=hg_norm_g[j])
            xs = _even_out(a, y, sg_ln_g[j], sg_ln_b[j], sg_w[j].astype(BF16), sg_b[j].T,
                           ev_w_out[j].astype(BF16), xs, g1)
            xs = _ffn(xs, sh2, sc2, g2, ffn_w_gate[j].astype(BF16), ffn_w_up[j].astype(BF16),
                      ffn_w_down[j].astype(BF16))
        else:
            lam_init = 0.8 - 0.6 * math.exp(-0.3 * layer)
            q, k, vt = _qkv(xs, sh1, sc1, da_w_qkv[j].astype(BF16), cos, sa, sb)
            a = _attention(q, k, vt, da_lam[j], da_subln_g[j], lam_init)
            xs = _proj_res(a, da_w_out[j].astype(BF16), xs, g1)
            xs = _moe(xs, sh2, sc2, g2, moe_w_router[j], moe_w_gate, moe_w_up, moe_w_down, j)
    return _final(xs, final_g)[None]
```

```python
import functools
import math

import jax
import jax.numpy as jnp
from jax import lax
from jax.experimental import pallas as pl
from jax.experimental.pallas import tpu as pltpu

F32 = jnp.float32
BF16 = jnp.bfloat16

D_MODEL = 1024
DEPTH = 4
GRID_W = 64
CTX_LEN = 256
EPS = 1e-6

HG_WIDTH = 512
HG_HEADS = 4
HG_D = 128
HG_SUB = 16
SG_WIDTH = 512
SG_CHUNK = 128
SG_GROUPS = 4
EVEN_IN = 5 * HG_WIDTH + 2 * SG_WIDTH

DA_HEADS = 8
DA_DH = 64
DA_DV = 128
ROPE_BASE = 10000.0
ROPE_AXIS = DA_DH // 2

D_FF = 2816
N_EXPERTS = 8
TOP_K = 2
E_FF = 3584

LOG2E = 1.4426950408889634
LANES = 128
ROW_TILE = CTX_LEN
ATT_KV_TILE = 1280
ATT_UNROLL = 4
ATT_MIN_SUM = 2.0 ** -80
MOE_TILE = 512
MOE_FF_TILE = 512
FFN_FF_TILE = 256
VMEM_LIMIT = 56 * 1024 * 1024


def _cparams(sem):
    return pltpu.CompilerParams(dimension_semantics=sem, vmem_limit_bytes=VMEM_LIMIT)


def _sigmoid(x):
    return 1.0 / (1.0 + jnp.exp(-x))


def _silu(x):
    return x * _sigmoid(x)


def _gelu(x):
    return 0.5 * x * (1.0 + lax.erf(x * (2.0 ** -0.5)))


def _modulated(x, sh, sc):
    ms = jnp.mean(x * x, axis=-1, keepdims=True)
    return x * lax.rsqrt(ms + EPS) * (1.0 + sc) + sh


def _mod_spec(width=D_MODEL):
    return pl.BlockSpec((None, 1, width), lambda i, *_: (jnp.minimum(i, 1), 0, 0))


def _ada_kernel(c_ref, w_ref, b_ref, o_ref):
    s = _silu(c_ref[...])
    w = w_ref[...]
    r0 = jnp.sum(s[:, 0:1] * w, axis=0, keepdims=True)
    r1 = jnp.sum(s[:, 1:2] * w, axis=0, keepdims=True)
    o_ref[0:1, :] = r0 + b_ref[...]
    o_ref[1:2, :] = r1 + b_ref[...]


def _ada_all(cc_t, mod_w, mod_b):
    n = 6 * D_MODEL
    tn = 1536
    return pl.pallas_call(
        _ada_kernel,
        grid=(DEPTH, n // tn),
        in_specs=[
            pl.BlockSpec((D_MODEL, 2), lambda l, j: (0, 0)),
            pl.BlockSpec((None, D_MODEL, tn), lambda l, j: (l, 0, j)),
            pl.BlockSpec((None, 1, tn), lambda l, j: (l, 0, j)),
        ],
        out_specs=pl.BlockSpec((None, 2, tn), lambda l, j: (l, 0, j)),
        out_shape=jax.ShapeDtypeStruct((DEPTH, 2, n), F32),
        compiler_params=_cparams(("arbitrary", "arbitrary")),
        name="ada",
    )(cc_t, mod_w, mod_b.reshape(DEPTH, 1, n))


def _even_in_kernel(x_ref, sh_ref, sc_ref, w_ref, o_ref):
    h = _modulated(x_ref[...], sh_ref[...], sc_ref[...]).astype(BF16)
    for n in range(0, EVEN_IN, 512):
        o_ref[:, n:n + 512] = jnp.dot(h, w_ref[:, n:n + 512], preferred_element_type=F32)


def _even_in(x, sh, sc, w):
    t = x.shape[0]
    return pl.pallas_call(
        _even_in_kernel,
        grid=(t // ROW_TILE,),
        in_specs=[
            pl.BlockSpec((ROW_TILE, D_MODEL), lambda i: (i, 0)),
            _mod_spec(), _mod_spec(),
            pl.BlockSpec((D_MODEL, EVEN_IN), lambda i: (0, 0)),
        ],
        out_specs=pl.BlockSpec((ROW_TILE, EVEN_IN), lambda i: (i, 0)),
        out_shape=jax.ShapeDtypeStruct((t, EVEN_IN), F32),
        compiler_params=_cparams(("arbitrary",)),
        name="even_in",
    )(x, sh, sc, w)


def _group_scan(x, r_in_group, forward):
    rows = x.shape[0]
    for sh in (1, 2, 4, 8):
        if forward:
            rolled = pltpu.roll(x, sh, 0)
            ok = r_in_group >= sh
        else:
            rolled = pltpu.roll(x, rows - sh, 0)
            ok = r_in_group < HG_SUB - sh
        x = x + jnp.where(ok, rolled, 0.0)
    return x


def _hgrn_kernel(*refs, rev, readout):
    if readout:
        q_ref, f_ref, i_ref, lb_ref, of_ref, g_ref, gain_ref, o_ref, st_ref = refs
    else:
        q_ref, f_ref, i_ref, lb_ref, o_ref, st_ref = refs
    rows = q_ref.shape[0]
    n_groups = rows // HG_SUB

    @pl.when(pl.program_id(1) == 0)
    def _():
        st_ref[...] = jnp.zeros_like(st_ref)

    f = f_ref[...]
    lb = lb_ref[...]
    log_f = jnp.log(lb + (1.0 - lb) * _sigmoid(f))
    kk = (1.0 - lb) * _sigmoid(-f)
    qs = _silu(q_ref[...])
    v = i_ref[...]

    row = lax.broadcasted_iota(jnp.int32, (rows, HG_D), 0)
    r16 = row & (HG_SUB - 1)
    grp = row // HG_SUB
    pfx = _group_scan(log_f, r16, True)
    sfx = _group_scan(log_f, r16, False)
    cum, other = (sfx, pfx) if rev else (pfx, sfx)
    qd = (qs * jnp.exp(cum)).astype(BF16)
    kd = (kk * jnp.exp(other - log_f)).astype(BF16)

    ones = jnp.ones((HG_D, HG_D), BF16)
    acc = jnp.zeros((rows, HG_D), F32)
    for d in range(HG_SUB):
        if d == 0:
            kk_s, cum_s, v_s = kk, cum, v
            p = qs * kk_s
        else:
            shift = rows - d if rev else d
            kk_s = pltpu.roll(kk, shift, 0)
            cum_s = pltpu.roll(cum, shift, 0)
            v_s = pltpu.roll(v, shift, 0)
            ok = (r16 < HG_SUB - d) if rev else (r16 >= d)
            p = jnp.where(ok, qs * kk_s * jnp.exp(cum - cum_s), 0.0)
        rsum = jnp.dot(p.astype(BF16), ones, preferred_element_type=F32)
        acc = acc + rsum * v_s

    v_t = v.T.astype(BF16)
    bd = jnp.concatenate(
        [jnp.where(grp == g, kd, jnp.zeros_like(kd)) for g in range(n_groups)], axis=1)
    u = jnp.dot(v_t, bd, preferred_element_type=F32)

    st = st_ref[...]
    outs = [None] * n_groups
    for g in (range(n_groups - 1, -1, -1) if rev else range(n_groups)):
        lo = g * HG_SUB
        o_inter = lax.dot_general(qd[lo:lo + HG_SUB], st.astype(BF16),
                                  (((1,), (1,)), ((), ())), preferred_element_type=F32)
        outs[g] = acc[lo:lo + HG_SUB] + o_inter
        edge = lo if rev else lo + HG_SUB - 1
        st = st * jnp.exp(cum[edge:edge + 1, :]) + u[:, g * HG_D:(g + 1) * HG_D]
    st_ref[...] = st
    o = jnp.concatenate(outs, axis=0)

    if readout:
        o = o + of_ref[...]
        ms = jnp.mean(o * o, axis=-1, keepdims=True)
        o = o * lax.rsqrt(ms + EPS) * gain_ref[...] * _silu(g_ref[...])
    o_ref[...] = o.astype(o_ref.dtype)


def _hgrn(y, lb, rev, o_fwd=None, gain=None):
    t = y.shape[0]
    nblk = t // ROW_TILE
    readout = o_fwd is not None
    if rev:
        blk = lambda c: jnp.where(c == 0, 0, nblk - c)
    else:
        blk = lambda c: c
    col = lambda slab: pl.BlockSpec((ROW_TILE, HG_D), lambda h, c: (blk(c), slab * HG_HEADS + h))
    in_specs = [col(0), col(2 if rev else 1), col(3),
                pl.BlockSpec((None, 1, HG_D), lambda h, c: (h, 0, 0))]
    args = [y, y, y, lb.reshape(HG_HEADS, 1, HG_D)]
    if readout:
        in_specs += [pl.BlockSpec((ROW_TILE, HG_D), lambda h, c: (blk(c), h)), col(4),
                     pl.BlockSpec((1, HG_D), lambda h, c: (0, 0))]
        args += [o_fwd, y, gain.reshape(1, HG_D)]
    return pl.pallas_call(
        functools.partial(_hgrn_kernel, rev=rev, readout=readout),
        grid=(HG_HEADS, nblk),
        in_specs=in_specs,
        out_specs=pl.BlockSpec((ROW_TILE, HG_D), lambda h, c: (blk(c), h)),
        out_shape=jax.ShapeDtypeStruct((t, HG_WIDTH), BF16 if readout else F32),
        scratch_shapes=[pltpu.VMEM((HG_D, HG_D), F32)],
        compiler_params=_cparams(("arbitrary", "arbitrary")),
        name="hgrn_bwd" if rev else "hgrn_fwd",
    )(*args)


def _even_out_kernel(a_ref, u_ref, v_ref, lng_ref, lnb_ref, sgw_ref, sgb_ref, w_ref, x_ref,
                     gate_ref, o_ref):
    u = _gelu(u_ref[...])
    v = _gelu(v_ref[...])
    mu = jnp.mean(v, axis=-1, keepdims=True)
    var = jnp.mean(jnp.square(v - mu), axis=-1, keepdims=True)
    vn = ((v - mu) * lax.rsqrt(var + 1e-5) * lng_ref[...] + lnb_ref[...]).astype(BF16)
    rows = u.shape[0]
    mixed_rows = []
    for c in range(rows // SG_CHUNK):
        lo = c * SG_CHUNK
        parts = []
        for g in range(SG_GROUPS):
            m = jnp.dot(sgw_ref[g], vn[lo:lo + SG_CHUNK, g * LANES:(g + 1) * LANES],
                        preferred_element_type=F32)
            parts.append(m + sgb_ref[:, g:g + 1])
        mixed_rows.append(jnp.concatenate(parts, axis=1))
    s = (u * jnp.concatenate(mixed_rows, axis=0)).astype(BF16)
    y = jnp.dot(a_ref[...], w_ref[0:HG_WIDTH, :], preferred_element_type=F32)
    y = y + jnp.dot(s, w_ref[HG_WIDTH:, :], preferred_element_type=F32)
    o_ref[...] = x_ref[...] + gate_ref[...] * y


def _even_out(a, y_proj, ln_g, ln_b, sg_w, sg_b_t, w_out, x, gate):
    t = x.shape[0]
    full = lambda shape: pl.BlockSpec(shape, lambda i: (0,) * len(shape))
    return pl.pallas_call(
        _even_out_kernel,
        grid=(t // ROW_TILE,),
        in_specs=[
            pl.BlockSpec((ROW_TILE, HG_WIDTH), lambda i: (i, 0)),
            pl.BlockSpec((ROW_TILE, SG_WIDTH), lambda i: (i, 5)),
            pl.BlockSpec((ROW_TILE, SG_WIDTH), lambda i: (i, 6)),
            full((1, SG_WIDTH)), full((1, SG_WIDTH)),
            full((SG_GROUPS, SG_CHUNK, SG_CHUNK)), full((SG_CHUNK, SG_GROUPS)),
            full((D_MODEL, D_MODEL)),
            pl.BlockSpec((ROW_TILE, D_MODEL), lambda i: (i, 0)),
            _mod_spec(),
        ],
        out_specs=pl.BlockSpec((ROW_TILE, D_MODEL), lambda i: (i, 0)),
        out_shape=jax.ShapeDtypeStruct((t, D_MODEL), F32),
        compiler_params=_cparams(("arbitrary",)),
        name="even_out",
    )(a, y_proj, y_proj, ln_g.reshape(1, -1), ln_b.reshape(1, -1), sg_w, sg_b_t, w_out, x, gate)


def _ffn_kernel(x_ref, sh_ref, sc_ref, gate_ref, wg_ref, wu_ref, wd_ref, o_ref):
    x = x_ref[...]
    h = _modulated(x, sh_ref[...], sc_ref[...]).astype(BF16)
    y = jnp.zeros(x.shape, F32)
    for n in range(0, D_FF, FFN_FF_TILE):
        g = jnp.dot(h, wg_ref[:, n:n + FFN_FF_TILE], preferred_element_type=F32)
        u = jnp.dot(h, wu_ref[:, n:n + FFN_FF_TILE], preferred_element_type=F32)
        a = (_silu(g) * u).astype(BF16)
        y = y + jnp.dot(a, wd_ref[n:n + FFN_FF_TILE, :], preferred_element_type=F32)
    o_ref[...] = x + gate_ref[...] * y


def _ffn(x, sh, sc, gate, wg, wu, wd):
    t = x.shape[0]
    full = lambda shape: pl.BlockSpec(shape, lambda i: (0,) * len(shape))
    return pl.pallas_call(
        _ffn_kernel,
        grid=(t // ROW_TILE,),
        in_specs=[
            pl.BlockSpec((ROW_TILE, D_MODEL), lambda i: (i, 0)),
            _mod_spec(), _mod_spec(), _mod_spec(),
            full((D_MODEL, D_FF)), full((D_MODEL, D_FF)), full((D_FF, D_MODEL)),
        ],
        out_specs=pl.BlockSpec((ROW_TILE, D_MODEL), lambda i: (i, 0)),
        out_shape=jax.ShapeDtypeStruct((t, D_MODEL), F32),
        compiler_params=_cparams(("arbitrary",)),
        name="ffn",
    )(x, sh, sc, gate, wg, wu, wd)


def _qkv_kernel(x_ref, sh_ref, sc_ref, w_ref, cos_ref, sa_ref, sb_ref, q_ref, k_ref, vt_ref):
    h = _modulated(x_ref[...], sh_ref[...], sc_ref[...]).astype(BF16)
    cos = cos_ref[...]
    sa = sa_ref[...]
    sb = sb_ref[...]

    def rope(z):
        return z * cos + pltpu.roll(z, LANES - 16, 1) * sa + pltpu.roll(z, 16, 1) * sb

    for n in range(0, D_MODEL, LANES):
        q = jnp.dot(h, w_ref[:, n:n + LANES], preferred_element_type=F32)
        q_ref[:, n:n + LANES] = (rope(q) * (DA_DH ** -0.5 * LOG2E)).astype(BF16)
        k = jnp.dot(h, w_ref[:, D_MODEL + n:D_MODEL + n + LANES], preferred_element_type=F32)
        k_ref[:, n:n + LANES] = rope(k).astype(BF16)
    for n in range(0, D_MODEL, 512):
        v = jnp.dot(h, w_ref[:, 2 * D_MODEL + n:2 * D_MODEL + n + 512], preferred_element_type=F32)
        vt_ref[n:n + 512, :] = v.T.astype(BF16)


def _qkv(x, sh, sc, w, cos, sa, sb):
    t = x.shape[0]
    row = lambda width: pl.BlockSpec((ROW_TILE, width), lambda i: (i, 0))
    return pl.pallas_call(
        _qkv_kernel,
        grid=(t // ROW_TILE,),
        in_specs=[row(D_MODEL), _mod_spec(), _mod_spec(),
                  pl.BlockSpec((D_MODEL, 3 * D_MODEL), lambda i: (0, 0)),
                  row(LANES), row(LANES), row(LANES)],
        out_specs=[row(D_MODEL), row(D_MODEL), pl.BlockSpec((D_MODEL, ROW_TILE), lambda i: (0, i))],
        out_shape=[jax.ShapeDtypeStruct((t, D_MODEL), BF16), jax.ShapeDtypeStruct((t, D_MODEL), BF16),
                   jax.ShapeDtypeStruct((D_MODEL, t), BF16)],
        compiler_params=_cparams(("arbitrary",)),
        name="qkv",
    )(x, sh, sc, w, cos, sa, sb)


def _attn_kernel(q_ref, k_ref, vt_ref, lam_ref, g_ref, o_ref, m_ref, l_ref, acc_ref, kmax_ref, *,
                 lam_init, n_keys):
    tq = q_ref.shape[0]
    q = q_ref[...]
    lane = lax.broadcasted_iota(jnp.int32, q.shape, 1)
    zero = jnp.zeros_like(q)
    qq = jnp.concatenate([jnp.where(lane < DA_DH, q, zero), jnp.where(lane >= DA_DH, q, zero)], axis=0)

    tk = ATT_KV_TILE
    n_chunks = n_keys // tk
    is_ctx = pl.program_id(1) == 0
    ones8 = jnp.ones((8, DA_DV), BF16)
    nt = (((1,), (1,)), ((), ()))

    @pl.when(is_ctx)
    def _():
        lane8 = lax.broadcasted_iota(jnp.int32, (8, DA_DV), 1)
        row8 = lax.broadcasted_iota(jnp.int32, (8, DA_DV), 0)
        sel = jnp.where((lane8 >= DA_DH) == (row8 == 1), 1.0, 0.0).astype(BF16)

        def body(c, best):
            kc = k_ref[pl.ds(pl.multiple_of(c * tk, tk), tk), :].astype(F32)
            n2 = lax.dot_general(sel, (kc * kc).astype(BF16), nt, preferred_element_type=F32)
            return jnp.maximum(best, n2)
        best = lax.fori_loop(0, n_chunks, body, jnp.zeros((8, tk), F32))
        kmax_ref[...] = jnp.broadcast_to(jnp.max(best, axis=1, keepdims=True), kmax_ref.shape)

    qf = qq.astype(F32)
    qn2 = lax.dot_general(ones8, (qf * qf).astype(BF16), nt, preferred_element_type=F32)[0:1]
    col = lax.broadcasted_iota(jnp.int32, (1, 2 * tq), 1)
    kn2 = jnp.where(col < tq, kmax_ref[0:1, 0:1], kmax_ref[1:2, 0:1])
    m_bound = jnp.sqrt(qn2 * kn2)

    def scores(off, size):
        k = k_ref[pl.ds(off, size), :]
        return lax.dot_general(k, qq, nt, preferred_element_type=F32)

    def fast_chunk(off, size):
        p = jnp.exp2(scores(off, size) - m_bound)
        l_ref[...] += jnp.sum(p, axis=0, keepdims=True)
        acc_ref[...] += jnp.dot(vt_ref[:, pl.ds(off, size)], p.astype(BF16), preferred_element_type=F32)

    def online_chunk(off, size):
        s = scores(off, size)
        m_old = m_ref[...]
        m_new = jnp.maximum(m_old, jnp.max(s, axis=0, keepdims=True))
        alpha = jnp.exp2(m_old - m_new)
        p = jnp.exp2(s - m_new)
        l_ref[...] = alpha * l_ref[...] + jnp.sum(p, axis=0, keepdims=True)
        acc_ref[...] = alpha * acc_ref[...] + jnp.dot(vt_ref[:, pl.ds(off, size)], p.astype(BF16),
                                                      preferred_element_type=F32)
        m_ref[...] = m_new

    def all_keys(chunk, unroll):
        @pl.when(is_ctx)
        def _():
            chunk(0, CTX_LEN)

        @pl.when(jnp.logical_not(is_ctx))
        def _():
            def body(jj, carry):
                for u in range(unroll):
                    chunk(pl.multiple_of((unroll * jj + u) * tk, tk), tk)
                return carry
            lax.fori_loop(0, n_chunks // unroll, body, 0)
            for c in range(n_chunks - n_chunks % unroll, n_chunks):
                chunk(c * tk, tk)

    l_ref[...] = jnp.zeros_like(l_ref)
    acc_ref[...] = jnp.zeros_like(acc_ref)
    all_keys(fast_chunk, ATT_UNROLL)

    @pl.when(jnp.logical_not(jnp.min(l_ref[...]) >= ATT_MIN_SUM))
    def _():
        m_ref[...] = jnp.full_like(m_ref, -jnp.inf)
        l_ref[...] = jnp.zeros_like(l_ref)
        acc_ref[...] = jnp.zeros_like(acc_ref)
        all_keys(online_chunk, 1)

    lam = lam_ref[...]
    lam_full = (jnp.exp(jnp.sum(lam[0:1] * lam[1:2], axis=-1, keepdims=True))
                - jnp.exp(jnp.sum(lam[2:3] * lam[3:4], axis=-1, keepdims=True)) + lam_init)
    o_all = acc_ref[...] / l_ref[...]
    o = (o_all[:, 0:tq] - lam_full * o_all[:, tq:2 * tq]).T
    ms = jnp.mean(o * o, axis=-1, keepdims=True)
    o = o * lax.rsqrt(ms + 1e-5) * g_ref[...] * (1.0 - lam_init)
    o_ref[...] = o.astype(o_ref.dtype)


def _attention(q, k, vt, lam, subln_g, lam_init):
    t = q.shape[0]
    assert t % ATT_KV_TILE == 0
    return pl.pallas_call(
        functools.partial(_attn_kernel, lam_init=lam_init, n_keys=t),
        grid=(DA_HEADS, t // ROW_TILE),
        in_specs=[
            pl.BlockSpec((ROW_TILE, DA_DV), lambda h, i: (i, h)),
            pl.BlockSpec((t, DA_DV), lambda h, i: (0, h)),
            pl.BlockSpec((DA_DV, t), lambda h, i: (h, 0)),
            pl.BlockSpec((4, DA_DH), lambda h, i: (0, 0)),
            pl.BlockSpec((1, DA_DV), lambda h, i: (0, 0)),
        ],
        out_specs=pl.BlockSpec((ROW_TILE, DA_DV), lambda h, i: (i, h)),
        out_shape=jax.ShapeDtypeStruct((t, D_MODEL), BF16),
        scratch_shapes=[pltpu.VMEM((1, 2 * ROW_TILE), F32), pltpu.VMEM((1, 2 * ROW_TILE), F32),
                        pltpu.VMEM((DA_DV, 2 * ROW_TILE), F32), pltpu.VMEM((8, LANES), F32)],
        compiler_params=_cparams(("arbitrary", "arbitrary")),
        name="attention",
    )(q, k, vt, lam, subln_g.reshape(1, DA_DV))


def _proj_res_kernel(a_ref, w_ref, x_ref, gate_ref, o_ref):
    y = jnp.dot(a_ref[...], w_ref[...], preferred_element_type=F32)
    o_ref[...] = x_ref[...] + gate_ref[...] * y


def _proj_res(a, w, x, gate):
    t = x.shape[0]
    row = pl.BlockSpec((ROW_TILE, D_MODEL), lambda i: (i, 0))
    return pl.pallas_call(
        _proj_res_kernel,
        grid=(t // ROW_TILE,),
        in_specs=[row, pl.BlockSpec((D_MODEL, D_MODEL), lambda i: (0, 0)), row, _mod_spec()],
        out_specs=row,
        out_shape=jax.ShapeDtypeStruct((t, D_MODEL), F32),
        compiler_params=_cparams(("arbitrary",)),
        name="proj_res",
    )(a, w, x, gate)


def _moe_in_kernel(x_ref, sh_ref, sc_ref, wr_ref, h_ref, logit_ref):
    h = _modulated(x_ref[...], sh_ref[...], sc_ref[...])
    h_ref[...] = h.astype(BF16)
    logit_ref[...] = jnp.dot(h, wr_ref[...], preferred_element_type=F32,
                             precision=lax.Precision.HIGHEST)


def _moe_in(x, sh, sc, w_router_padded):
    t = x.shape[0]
    row = lambda width: pl.BlockSpec((ROW_TILE, width), lambda i: (i, 0))
    return pl.pallas_call(
        _moe_in_kernel,
        grid=(t // ROW_TILE,),
        in_specs=[row(D_MODEL), _mod_spec(), _mod_spec(),
                  pl.BlockSpec((D_MODEL, LANES), lambda i: (0, 0))],
        out_specs=[row(D_MODEL), row(LANES)],
        out_shape=[jax.ShapeDtypeStruct((t, D_MODEL), BF16), jax.ShapeDtypeStruct((t, LANES), F32)],
        compiler_params=_cparams(("arbitrary",)),
        name="moe_in",
    )(x, sh, sc, w_router_padded)


def _experts_kernel(blk_e_ref, n_used_ref, x_ref, wg_ref, wu_ref, wd_ref, o_ref):
    b = pl.program_id(0)
    f = pl.program_id(1)

    @pl.when(f == 0)
    def _():
        o_ref[...] = jnp.zeros_like(o_ref)

    @pl.when(b < n_used_ref[0])
    def _():
        x = x_ref[...]
        g = jnp.dot(x, wg_ref[...].astype(BF16), preferred_element_type=F32)
        u = jnp.dot(x, wu_ref[...].astype(BF16), preferred_element_type=F32)
        a = (_silu(g) * u).astype(BF16)
        o_ref[...] += jnp.dot(a, wd_ref[...].astype(BF16), preferred_element_type=F32)


def _experts(xg, blk_e, n_used, wg, wu, wd, j):
    n_rows = xg.shape[0]
    n_blocks = n_rows // MOE_TILE
    n_ff = E_FF // MOE_FF_TILE
    ff = lambda b, f, nu: jnp.where(b < nu[0], f, n_ff - 1)
    grid_spec = pltpu.PrefetchScalarGridSpec(
        num_scalar_prefetch=2,
        grid=(n_blocks, n_ff),
        in_specs=[
            pl.BlockSpec((MOE_TILE, D_MODEL), lambda b, f, be, nu: (b, 0)),
            pl.BlockSpec((None, None, D_MODEL, MOE_FF_TILE),
                         lambda b, f, be, nu: (j, be[b], 0, ff(b, f, nu))),
            pl.BlockSpec((None, None, D_MODEL, MOE_FF_TILE),
                         lambda b, f, be, nu: (j, be[b], 0, ff(b, f, nu))),
            pl.BlockSpec((None, None, MOE_FF_TILE, D_MODEL),
                         lambda b, f, be, nu: (j, be[b], ff(b, f, nu), 0)),
        ],
        out_specs=pl.BlockSpec((MOE_TILE, D_MODEL), lambda b, f, be, nu: (b, 0)),
    )
    return pl.pallas_call(
        _experts_kernel,
        grid_spec=grid_spec,
        out_shape=jax.ShapeDtypeStruct((n_rows, D_MODEL), F32),
        compiler_params=_cparams(("arbitrary", "arbitrary")),
        name="experts",
    )(blk_e, n_used, xg, wg, wu, wd)


def _combine_kernel(x_ref, gate_ref, y0_ref, y1_ref, w_ref, o_ref):
    w = w_ref[...]
    y = w[:, 0:1] * y0_ref[...] + w[:, 1:2] * y1_ref[...]
    o_ref[...] = x_ref[...] + gate_ref[...] * y


def _combine(x, gate, y0, y1, w):
    t = x.shape[0]
    row = pl.BlockSpec((ROW_TILE, D_MODEL), lambda i: (i, 0))
    return pl.pallas_call(
        _combine_kernel,
        grid=(t // ROW_TILE,),
        in_specs=[row, _mod_spec(), row, row, pl.BlockSpec((ROW_TILE, TOP_K), lambda i: (i, 0))],
        out_specs=row,
        out_shape=jax.ShapeDtypeStruct((t, D_MODEL), F32),
        compiler_params=_cparams(("arbitrary",)),
        name="moe_combine",
    )(x, gate, y0, y1, w)


def _moe(x, sh, sc, gate, w_router, wg, wu, wd, j):
    t = x.shape[0]
    wr = jnp.zeros((D_MODEL, LANES), F32).at[:, :N_EXPERTS].set(w_router)
    h, logits = _moe_in(x, sh, sc, wr)
    top_v, top_i = lax.top_k(logits[:, :N_EXPERTS], TOP_K)
    top_w = jax.nn.softmax(top_v, axis=-1)
    flat_e = top_i.reshape(-1)
    onehot = (flat_e[:, None] == jnp.arange(N_EXPERTS, dtype=flat_e.dtype)[None, :]).astype(jnp.int32)
    counts = jnp.sum(onehot, axis=0)
    rank = jnp.sum((jnp.cumsum(onehot, axis=0) - onehot) * onehot, axis=1)
    padded = ((counts + MOE_TILE - 1) // MOE_TILE) * MOE_TILE
    pad_end = jnp.cumsum(padded)
    pad_start = pad_end - padded
    dest = (pad_start[flat_e] + rank).astype(jnp.int32)
    n_rows = t * TOP_K + N_EXPERTS * MOE_TILE
    n_blocks = n_rows // MOE_TILE
    flat_tok = jnp.repeat(jnp.arange(t, dtype=jnp.int32), TOP_K)
    row_tok = jnp.zeros((n_rows,), jnp.int32).at[dest].set(flat_tok)
    blk_start = jnp.arange(n_blocks, dtype=jnp.int32) * MOE_TILE
    blk_e = jnp.minimum(jnp.sum(pad_end[None, :] <= blk_start[:, None], axis=1), N_EXPERTS - 1)
    n_used = (pad_end[-1:] // MOE_TILE).astype(jnp.int32)
    xg = jnp.take(h, row_tok, axis=0)
    out = _experts(xg, blk_e.astype(jnp.int32), n_used, wg, wu, wd, j)
    pos = dest.reshape(t, TOP_K)
    y0 = jnp.take(out, pos[:, 0], axis=0)
    y1 = jnp.take(out, pos[:, 1], axis=0)
    return _combine(x, gate, y0, y1, top_w)


def _final_kernel(x_ref, g_ref, o_ref):
    x = x_ref[...]
    ms = jnp.mean(x * x, axis=-1, keepdims=True)
    o_ref[...] = x * lax.rsqrt(ms + EPS) * g_ref[...]


def _final(x, g):
    n = x.shape[0] - CTX_LEN
    return pl.pallas_call(
        _final_kernel,
        grid=(n // ROW_TILE,),
        in_specs=[pl.BlockSpec((ROW_TILE, D_MODEL), lambda i: (i + 1, 0)),
                  pl.BlockSpec((1, D_MODEL), lambda i: (0, 0))],
        out_specs=pl.BlockSpec((ROW_TILE, D_MODEL), lambda i: (i, 0)),
        out_shape=jax.ShapeDtypeStruct((n, D_MODEL), F32),
        compiler_params=_cparams(("arbitrary",)),
        name="final_norm",
    )(x, g.reshape(1, D_MODEL))


def _rope_tables(n):
    pos = jnp.arange(n, dtype=jnp.int32)
    row = (pos // GRID_W).astype(F32)
    col = (pos % GRID_W).astype(F32)
    inv = 1.0 / (ROPE_BASE ** (jnp.arange(0, ROPE_AXIS, 2, dtype=F32) / ROPE_AXIS))
    ar = row[:, None] * inv
    ac = col[:, None] * inv
    ang = jnp.concatenate([ar, ar, ac, ac], axis=-1)
    cos = jnp.cos(ang)
    sin = jnp.sin(ang)
    first = (jnp.arange(DA_DH) % ROPE_AXIS) < (ROPE_AXIS // 2)
    sa = jnp.where(first, -sin, 0.0)
    sb = jnp.where(first, 0.0, sin)
    pad = lambda a, fill: jnp.concatenate(
        [jnp.full((CTX_LEN, LANES), fill, F32), jnp.tile(a, (1, LANES // DA_DH))], axis=0)
    return pad(cos, 1.0), pad(sa, 0.0), pad(sb, 0.0)


def kernel(x, c, ctx, c_ctx, mod_w, mod_b, hg_lb, ev_w_in, hg_norm_g, sg_ln_g, sg_ln_b, sg_w, sg_b, ev_w_out, da_w_qkv, da_lam, da_subln_g, da_w_out, ffn_w_gate, ffn_w_up, ffn_w_down, moe_w_router, moe_w_gate, moe_w_up, moe_w_down, final_g):
    n = x.shape[1]
    xs = jnp.concatenate([ctx[0], x[0]], axis=0)
    cc_t = jnp.stack([c_ctx, c[0]], axis=1)
    mods = _ada_all(cc_t, mod_w, mod_b).reshape(DEPTH, 2, 6, 1, D_MODEL)
    cos, sa, sb = _rope_tables(n)
    lb_sm = jax.nn.softmax(hg_lb.astype(F32), axis=1)
    lb_all = jnp.cumsum(lb_sm, axis=1) - lb_sm[:, :1]

    for layer in range(DEPTH):
        j = layer // 2
        sh1, sc1, g1, sh2, sc2, g2 = (mods[layer, :, i] for i in range(6))
        if layer % 2 == 0:
            y = _even_in(xs, sh1, sc1, ev_w_in[j].astype(BF16))
            o_f = _hgrn(y, lb_all[0, j], rev=False)
            a = _hgrn(y, lb_all[1, j], rev=True, o_fwd=o_f, gain=hg_norm_g[j])
            xs = _even_out(a, y, sg_ln_g[j], sg_ln_b[j], sg_w[j].astype(BF16), sg_b[j].T,
                           ev_w_out[j].astype(BF16), xs, g1)
            xs = _ffn(xs, sh2, sc2, g2, ffn_w_gate[j].astype(BF16), ffn_w_up[j].astype(BF16),
                      ffn_w_down[j].astype(BF16))
        else:
            lam_init = 0.8 - 0.6 * math.exp(-0.3 * layer)
            q, k, vt = _qkv(xs, sh1, sc1, da_w_qkv[j].astype(BF16), cos, sa, sb)
            a = _attention(q, k, vt, da_lam[j], da_subln_g[j], lam_init)
            xs = _proj_res(a, da_w_out[j].astype(BF16), xs, g1)
            xs = _moe(xs, sh2, sc2, g2, moe_w_router[j], moe_w_gate, moe_w_up, moe_w_down, j)
    return _final(xs, final_g)[None]
```

```python
import functools
import math

import jax
import jax.numpy as jnp
from jax import lax
from jax.experimental import pallas as pl
from jax.experimental.pallas import tpu as pltpu

F32 = jnp.float32
BF16 = jnp.bfloat16

D_MODEL = 1024
DEPTH = 4
GRID_W = 64
CTX_LEN = 256
EPS = 1e-6

HG_WIDTH = 512
HG_HEADS = 4
HG_D = 128
HG_SUB = 16
HG_SAFE_DECAY = 80.0
SG_WIDTH = 512
SG_CHUNK = 128
SG_GROUPS = 4
EVEN_IN = 5 * HG_WIDTH + 2 * SG_WIDTH

DA_HEADS = 8
DA_DH = 64
DA_DV = 128
ROPE_BASE = 10000.0
ROPE_AXIS = DA_DH // 2

D_FF = 2816
N_EXPERTS = 8
TOP_K = 2
E_FF = 3584

LOG2E = 1.4426950408889634
LANES = 128
ROW_TILE = CTX_LEN
ATT_KV_TILE = 1280
ATT_UNROLL = 4
ATT_MIN_SUM = 2.0 ** -80
MOE_TILE = 1024
MOE_FF_TILE = 512
FFN_FF_TILE = 256
VMEM_LIMIT = 56 * 1024 * 1024


def _cparams(sem):
    return pltpu.CompilerParams(dimension_semantics=sem, vmem_limit_bytes=VMEM_LIMIT)


def _sigmoid(x):
    return 1.0 / (1.0 + jnp.exp(-x))


def _silu(x):
    return x * _sigmoid(x)


def _gelu(x):
    return 0.5 * x * (1.0 + lax.erf(x * (2.0 ** -0.5)))


def _modulated(x, sh, sc):
    ms = jnp.mean(x * x, axis=-1, keepdims=True)
    return x * lax.rsqrt(ms + EPS) * (1.0 + sc) + sh


def _mod_spec(width=D_MODEL):
    return pl.BlockSpec((None, 1, width), lambda i, *_: (jnp.minimum(i, 1), 0, 0))


def _ada_kernel(c_ref, w_ref, b_ref, o_ref):
    s = _silu(c_ref[...])
    w = w_ref[...]
    r0 = jnp.sum(s[:, 0:1] * w, axis=0, keepdims=True)
    r1 = jnp.sum(s[:, 1:2] * w, axis=0, keepdims=True)
    o_ref[0:1, :] = r0 + b_ref[...]
    o_ref[1:2, :] = r1 + b_ref[...]


def _ada_all(cc_t, mod_w, mod_b):
    n = 6 * D_MODEL
    tn = 1536
    return pl.pallas_call(
        _ada_kernel,
        grid=(DEPTH, n // tn),
        in_specs=[
            pl.BlockSpec((D_MODEL, 2), lambda l, j: (0, 0)),
            pl.BlockSpec((None, D_MODEL, tn), lambda l, j: (l, 0, j)),
            pl.BlockSpec((None, 1, tn), lambda l, j: (l, 0, j)),
        ],
        out_specs=pl.BlockSpec((None, 2, tn), lambda l, j: (l, 0, j)),
        out_shape=jax.ShapeDtypeStruct((DEPTH, 2, n), F32),
        compiler_params=_cparams(("arbitrary", "arbitrary")),
        name="ada",
    )(cc_t, mod_w, mod_b.reshape(DEPTH, 1, n))


def _even_in_kernel(x_ref, sh_ref, sc_ref, w_ref, o_ref):
    h = _modulated(x_ref[...], sh_ref[...], sc_ref[...]).astype(BF16)
    for n in range(0, EVEN_IN, 512):
        o_ref[:, n:n + 512] = jnp.dot(h, w_ref[:, n:n + 512], preferred_element_type=F32)


def _even_in(x, sh, sc, w):
    t = x.shape[0]
    return pl.pallas_call(
        _even_in_kernel,
        grid=(t // ROW_TILE,),
        in_specs=[
            pl.BlockSpec((ROW_TILE, D_MODEL), lambda i: (i, 0)),
            _mod_spec(), _mod_spec(),
            pl.BlockSpec((D_MODEL, EVEN_IN), lambda i: (0, 0)),
        ],
        out_specs=pl.BlockSpec((ROW_TILE, EVEN_IN), lambda i: (i, 0)),
        out_shape=jax.ShapeDtypeStruct((t, EVEN_IN), F32),
        compiler_params=_cparams(("arbitrary",)),
        name="even_in",
    )(x, sh, sc, w)


def _group_scan(x, r_in_group, forward):
    rows = x.shape[0]
    for sh in (1, 2, 4, 8):
        if forward:
            rolled = pltpu.roll(x, sh, 0)
            ok = r_in_group >= sh
        else:
            rolled = pltpu.roll(x, rows - sh, 0)
            ok = r_in_group < HG_SUB - sh
        x = x + jnp.where(ok, rolled, 0.0)
    return x


def _hgrn_kernel(*refs, rev, readout):
    if readout:
        q_ref, f_ref, i_ref, lb_ref, of_ref, g_ref, gain_ref, o_ref, st_ref, acc_ref = refs
    else:
        q_ref, f_ref, i_ref, lb_ref, o_ref, st_ref, acc_ref = refs
    rows = q_ref.shape[0]
    n_groups = rows // HG_SUB

    @pl.when(pl.program_id(1) == 0)
    def _():
        st_ref[...] = jnp.zeros_like(st_ref)

    f = f_ref[...]
    lb = lb_ref[...]
    t_abs = jnp.exp(-jnp.abs(f))
    big = 1.0 / (1.0 + t_abs)
    small = t_abs * big
    log_f = jnp.log(lb + (1.0 - lb) * jnp.where(f >= 0, big, small))
    kk = (1.0 - lb) * jnp.where(f >= 0, small, big)
    qs = _silu(q_ref[...])
    v = i_ref[...]

    row = lax.broadcasted_iota(jnp.int32, (rows, HG_D), 0)
    r16 = row & (HG_SUB - 1)
    grp = row // HG_SUB
    pfx = _group_scan(log_f, r16, True)
    sfx = _group_scan(log_f, r16, False)
    cum, other = (sfx, pfx) if rev else (pfx, sfx)
    qd = (qs * jnp.exp(cum)).astype(BF16)
    kd = (kk * jnp.exp(other - log_f)).astype(BF16)

    kn = (kk * jnp.exp(-cum)).astype(BF16)
    sc = lax.dot_general(qd, kn, (((1,), (1,)), ((), ())), preferred_element_type=F32)
    ti = lax.broadcasted_iota(jnp.int32, (rows, rows), 0)
    si = lax.broadcasted_iota(jnp.int32, (rows, rows), 1)
    reach = ti & (HG_SUB - 1)
    dist, reach = (si - ti, HG_SUB - 1 - reach) if rev else (ti - si, reach)
    ok = dist.astype(jnp.uint32) <= reach.astype(jnp.uint32)
    acc_factored = jnp.dot(jnp.where(ok, sc, 0.0).astype(BF16), v.astype(BF16),
                           preferred_element_type=F32)

    v_t = v.T.astype(BF16)
    bd = jnp.concatenate(
        [jnp.where(grp == g, kd, jnp.zeros_like(kd)) for g in range(n_groups)], axis=1)
    u = jnp.dot(v_t, bd, preferred_element_type=F32)

    st = st_ref[...]
    outs = [None] * n_groups
    for g in (range(n_groups - 1, -1, -1) if rev else range(n_groups)):
        lo = g * HG_SUB
        outs[g] = lax.dot_general(qd[lo:lo + HG_SUB], st.astype(BF16),
                                  (((1,), (1,)), ((), ())), preferred_element_type=F32)
        edge = lo if rev else lo + HG_SUB - 1
        st = st * jnp.exp(cum[edge:edge + 1, :]) + u[:, g * HG_D:(g + 1) * HG_D]
    st_ref[...] = st
    o_inter = jnp.concatenate(outs, axis=0)
    acc_ref[...] = acc_factored + o_inter

    @pl.when(jnp.logical_not(jnp.min(cum) >= -HG_SAFE_DECAY))
    def _():
        ones = jnp.ones((HG_D, HG_D), BF16)
        acc = o_inter
        for d in range(HG_SUB):
            if d == 0:
                kk_s, cum_s, v_s = kk, cum, v
                p = qs * kk_s
            else:
                shift = rows - d if rev else d
                kk_s = pltpu.roll(kk, shift, 0)
                cum_s = pltpu.roll(cum, shift, 0)
                v_s = pltpu.roll(v, shift, 0)
                ok_d = (r16 < HG_SUB - d) if rev else (r16 >= d)
                p = jnp.where(ok_d, qs * kk_s * jnp.exp(cum - cum_s), 0.0)
            rsum = jnp.dot(p.astype(BF16), ones, preferred_element_type=F32)
            acc = acc + rsum * v_s
        acc_ref[...] = acc

    o = acc_ref[...]

    if readout:
        o = o + of_ref[...]
        ms = jnp.mean(o * o, axis=-1, keepdims=True)
        o = o * lax.rsqrt(ms + EPS) * gain_ref[...] * _silu(g_ref[...])
    o_ref[...] = o.astype(o_ref.dtype)


def _hgrn(y, lb, rev, o_fwd=None, gain=None):
    t = y.shape[0]
    nblk = t // ROW_TILE
    readout = o_fwd is not None
    if rev:
        blk = lambda c: jnp.where(c == 0, 0, nblk - c)
    else:
        blk = lambda c: c
    col = lambda slab: pl.BlockSpec((ROW_TILE, HG_D), lambda h, c: (blk(c), slab * HG_HEADS + h))
    in_specs = [col(0), col(2 if rev else 1), col(3),
                pl.BlockSpec((None, 1, HG_D), lambda h, c: (h, 0, 0))]
    args = [y, y, y, lb.reshape(HG_HEADS, 1, HG_D)]
    if readout:
        in_specs += [pl.BlockSpec((ROW_TILE, HG_D), lambda h, c: (blk(c), h)), col(4),
                     pl.BlockSpec((1, HG_D), lambda h, c: (0, 0))]
        args += [o_fwd, y, gain.reshape(1, HG_D)]
    return pl.pallas_call(
        functools.partial(_hgrn_kernel, rev=rev, readout=readout),
        grid=(HG_HEADS, nblk),
        in_specs=in_specs,
        out_specs=pl.BlockSpec((ROW_TILE, HG_D), lambda h, c: (blk(c), h)),
        out_shape=jax.ShapeDtypeStruct((t, HG_WIDTH), BF16 if readout else F32),
        scratch_shapes=[pltpu.VMEM((HG_D, HG_D), F32), pltpu.VMEM((ROW_TILE, HG_D), F32)],
        compiler_params=_cparams(("arbitrary", "arbitrary")),
        name="hgrn_bwd" if rev else "hgrn_fwd",
    )(*args)


def _even_out_kernel(a_ref, u_ref, v_ref, lng_ref, lnb_ref, sgw_ref, sgb_ref, w_ref, x_ref,
                     gate_ref, o_ref):
    u = _gelu(u_ref[...])
    v = _gelu(v_ref[...])
    mu = jnp.mean(v, axis=-1, keepdims=True)
    var = jnp.mean(jnp.square(v - mu), axis=-1, keepdims=True)
    vn = ((v - mu) * lax.rsqrt(var + 1e-5) * lng_ref[...] + lnb_ref[...]).astype(BF16)
    rows = u.shape[0]
    mixed_rows = []
    for c in range(rows // SG_CHUNK):
        lo = c * SG_CHUNK
        parts = []
        for g in range(SG_GROUPS):
            m = jnp.dot(sgw_ref[g], vn[lo:lo + SG_CHUNK, g * LANES:(g + 1) * LANES],
                        preferred_element_type=F32)
            parts.append(m + sgb_ref[:, g:g + 1])
        mixed_rows.append(jnp.concatenate(parts, axis=1))
    s = (u * jnp.concatenate(mixed_rows, axis=0)).astype(BF16)
    y = jnp.dot(a_ref[...], w_ref[0:HG_WIDTH, :], preferred_element_type=F32)
    y = y + jnp.dot(s, w_ref[HG_WIDTH:, :], preferred_element_type=F32)
    o_ref[...] = x_ref[...] + gate_ref[...] * y


def _even_out(a, y_proj, ln_g, ln_b, sg_w, sg_b_t, w_out, x, gate):
    t = x.shape[0]
    full = lambda shape: pl.BlockSpec(shape, lambda i: (0,) * len(shape))
    return pl.pallas_call(
        _even_out_kernel,
        grid=(t // ROW_TILE,),
        in_specs=[
            pl.BlockSpec((ROW_TILE, HG_WIDTH), lambda i: (i, 0)),
            pl.BlockSpec((ROW_TILE, SG_WIDTH), lambda i: (i, 5)),
            pl.BlockSpec((ROW_TILE, SG_WIDTH), lambda i: (i, 6)),
            full((1, SG_WIDTH)), full((1, SG_WIDTH)),
            full((SG_GROUPS, SG_CHUNK, SG_CHUNK)), full((SG_CHUNK, SG_GROUPS)),
            full((D_MODEL, D_MODEL)),
            pl.BlockSpec((ROW_TILE, D_MODEL), lambda i: (i, 0)),
            _mod_spec(),
        ],
        out_specs=pl.BlockSpec((ROW_TILE, D_MODEL), lambda i: (i, 0)),
        out_shape=jax.ShapeDtypeStruct((t, D_MODEL), F32),
        compiler_params=_cparams(("arbitrary",)),
        name="even_out",
    )(a, y_proj, y_proj, ln_g.reshape(1, -1), ln_b.reshape(1, -1), sg_w, sg_b_t, w_out, x, gate)


def _ffn_kernel(x_ref, sh_ref, sc_ref, gate_ref, wg_ref, wu_ref, wd_ref, o_ref):
    x = x_ref[...]
    h = _modulated(x, sh_ref[...], sc_ref[...]).astype(BF16)
    y = jnp.zeros(x.shape, F32)
    for n in range(0, D_FF, FFN_FF_TILE):
        g = jnp.dot(h, wg_ref[:, n:n + FFN_FF_TILE], preferred_element_type=F32)
        u = jnp.dot(h, wu_ref[:, n:n + FFN_FF_TILE], preferred_element_type=F32)
        a = (_silu(g) * u).astype(BF16)
        y = y + jnp.dot(a, wd_ref[n:n + FFN_FF_TILE, :], preferred_element_type=F32)
    o_ref[...] = x + gate_ref[...] * y


def _ffn(x, sh, sc, gate, wg, wu, wd):
    t = x.shape[0]
    full = lambda shape: pl.BlockSpec(shape, lambda i: (0,) * len(shape))
    return pl.pallas_call(
        _ffn_kernel,
        grid=(t // ROW_TILE,),
        in_specs=[
            pl.BlockSpec((ROW_TILE, D_MODEL), lambda i: (i, 0)),
            _mod_spec(), _mod_spec(), _mod_spec(),
            full((D_MODEL, D_FF)), full((D_MODEL, D_FF)), full((D_FF, D_MODEL)),
        ],
        out_specs=pl.BlockSpec((ROW_TILE, D_MODEL), lambda i: (i, 0)),
        out_shape=jax.ShapeDtypeStruct((t, D_MODEL), F32),
        compiler_params=_cparams(("arbitrary",)),
        name="ffn",
    )(x, sh, sc, gate, wg, wu, wd)


def _qkv_kernel(x_ref, sh_ref, sc_ref, w_ref, cos_ref, sa_ref, sb_ref, q_ref, k_ref, vt_ref):
    h = _modulated(x_ref[...], sh_ref[...], sc_ref[...]).astype(BF16)
    cos = cos_ref[...]
    sa = sa_ref[...]
    sb = sb_ref[...]

    def rope(z):
        return z * cos + pltpu.roll(z, LANES - 16, 1) * sa + pltpu.roll(z, 16, 1) * sb

    for n in range(0, D_MODEL, LANES):
        q = jnp.dot(h, w_ref[:, n:n + LANES], preferred_element_type=F32)
        q_ref[:, n:n + LANES] = (rope(q) * (DA_DH ** -0.5 * LOG2E)).astype(BF16)
        k = jnp.dot(h, w_ref[:, D_MODEL + n:D_MODEL + n + LANES], preferred_element_type=F32)
        k_ref[:, n:n + LANES] = rope(k).astype(BF16)
    for n in range(0, D_MODEL, 512):
        v = jnp.dot(h, w_ref[:, 2 * D_MODEL + n:2 * D_MODEL + n + 512], preferred_element_type=F32)
        vt_ref[n:n + 512, :] = v.T.astype(BF16)


def _qkv(x, sh, sc, w, cos, sa, sb):
    t = x.shape[0]
    row = lambda width: pl.BlockSpec((ROW_TILE, width), lambda i: (i, 0))
    return pl.pallas_call(
        _qkv_kernel,
        grid=(t // ROW_TILE,),
        in_specs=[row(D_MODEL), _mod_spec(), _mod_spec(),
                  pl.BlockSpec((D_MODEL, 3 * D_MODEL), lambda i: (0, 0)),
                  row(LANES), row(LANES), row(LANES)],
        out_specs=[row(D_MODEL), row(D_MODEL), pl.BlockSpec((D_MODEL, ROW_TILE), lambda i: (0, i))],
        out_shape=[jax.ShapeDtypeStruct((t, D_MODEL), BF16), jax.ShapeDtypeStruct((t, D_MODEL), BF16),
                   jax.ShapeDtypeStruct((D_MODEL, t), BF16)],
        compiler_params=_cparams(("arbitrary",)),
        name="qkv",
    )(x, sh, sc, w, cos, sa, sb)


def _attn_kernel(q_ref, k_ref, vt_ref, lam_ref, g_ref, o_ref, m_ref, l_ref, acc_ref, kmax_ref, *,
                 lam_init, n_keys):
    tq = q_ref.shape[0]
    q = q_ref[...]
    lane = lax.broadcasted_iota(jnp.int32, q.shape, 1)
    zero = jnp.zeros_like(q)
    qq = jnp.concatenate([jnp.where(lane < DA_DH, q, zero), jnp.where(lane >= DA_DH, q, zero)], axis=0)

    tk = ATT_KV_TILE
    n_chunks = n_keys // tk
    is_ctx = pl.program_id(1) == 0
    ones8 = jnp.ones((8, DA_DV), BF16)
    nt = (((1,), (1,)), ((), ()))

    @pl.when(is_ctx)
    def _():
        lane8 = lax.broadcasted_iota(jnp.int32, (8, DA_DV), 1)
        row8 = lax.broadcasted_iota(jnp.int32, (8, DA_DV), 0)
        sel = jnp.where((lane8 >= DA_DH) == (row8 == 1), 1.0, 0.0).astype(BF16)

        def body(c, best):
            kc = k_ref[pl.ds(pl.multiple_of(c * tk, tk), tk), :].astype(F32)
            n2 = lax.dot_general(sel, (kc * kc).astype(BF16), nt, preferred_element_type=F32)
            return jnp.maximum(best, n2)
        best = lax.fori_loop(0, n_chunks, body, jnp.zeros((8, tk), F32))
        kmax_ref[...] = jnp.broadcast_to(jnp.max(best, axis=1, keepdims=True), kmax_ref.shape)

    qf = qq.astype(F32)
    qn2 = lax.dot_general(ones8, (qf * qf).astype(BF16), nt, preferred_element_type=F32)[0:1]
    col = lax.broadcasted_iota(jnp.int32, (1, 2 * tq), 1)
    kn2 = jnp.where(col < tq, kmax_ref[0:1, 0:1], kmax_ref[1:2, 0:1])
    m_bound = jnp.sqrt(qn2 * kn2)

    def scores(off, size):
        k = k_ref[pl.ds(off, size), :]
        return lax.dot_general(k, qq, nt, preferred_element_type=F32)

    def fast_chunk(off, size):
        p = jnp.exp2(scores(off, size) - m_bound)
        l_ref[...] += jnp.sum(p, axis=0, keepdims=True)
        acc_ref[...] += jnp.dot(vt_ref[:, pl.ds(off, size)], p.astype(BF16), preferred_element_type=F32)

    def online_chunk(off, size):
        s = scores(off, size)
        m_old = m_ref[...]
        m_new = jnp.maximum(m_old, jnp.max(s, axis=0, keepdims=True))
        alpha = jnp.exp2(m_old - m_new)
        p = jnp.exp2(s - m_new)
        l_ref[...] = alpha * l_ref[...] + jnp.sum(p, axis=0, keepdims=True)
        acc_ref[...] = alpha * acc_ref[...] + jnp.dot(vt_ref[:, pl.ds(off, size)], p.astype(BF16),
                                                      preferred_element_type=F32)
        m_ref[...] = m_new

    def all_keys(chunk, unroll):
        @pl.when(is_ctx)
        def _():
            chunk(0, CTX_LEN)

        @pl.when(jnp.logical_not(is_ctx))
        def _():
            def body(jj, carry):
                for u in range(unroll):
                    chunk(pl.multiple_of((unroll * jj + u) * tk, tk), tk)
                return carry
            lax.fori_loop(0, n_chunks // unroll, body, 0)
            for c in range(n_chunks - n_chunks % unroll, n_chunks):
                chunk(c * tk, tk)

    l_ref[...] = jnp.zeros_like(l_ref)
    acc_ref[...] = jnp.zeros_like(acc_ref)
    all_keys(fast_chunk, ATT_UNROLL)

    @pl.when(jnp.logical_not(jnp.min(l_ref[...]) >= ATT_MIN_SUM))
    def _():
        m_ref[...] = jnp.full_like(m_ref, -jnp.inf)
        l_ref[...] = jnp.zeros_like(l_ref)
        acc_ref[...] = jnp.zeros_like(acc_ref)
        all_keys(online_chunk, 1)

    lam = lam_ref[...]
    lam_full = (jnp.exp(jnp.sum(lam[0:1] * lam[1:2], axis=-1, keepdims=True))
                - jnp.exp(jnp.sum(lam[2:3] * lam[3:4], axis=-1, keepdims=True)) + lam_init)
    o_all = acc_ref[...] / l_ref[...]
    o = (o_all[:, 0:tq] - lam_full * o_all[:, tq:2 * tq]).T
    ms = jnp.mean(o * o, axis=-1, keepdims=True)
    o = o * lax.rsqrt(ms + 1e-5) * g_ref[...] * (1.0 - lam_init)
    o_ref[...] = o.astype(o_ref.dtype)


def _attention(q, k, vt, lam, subln_g, lam_init):
    t = q.shape[0]
    assert t % ATT_KV_TILE == 0
    return pl.pallas_call(
        functools.partial(_attn_kernel, lam_init=lam_init, n_keys=t),
        grid=(DA_HEADS, t // ROW_TILE),
        in_specs=[
            pl.BlockSpec((ROW_TILE, DA_DV), lambda h, i: (i, h)),
            pl.BlockSpec((t, DA_DV), lambda h, i: (0, h)),
            pl.BlockSpec((DA_DV, t), lambda h, i: (h, 0)),
            pl.BlockSpec((4, DA_DH), lambda h, i: (0, 0)),
            pl.BlockSpec((1, DA_DV), lambda h, i: (0, 0)),
        ],
        out_specs=pl.BlockSpec((ROW_TILE, DA_DV), lambda h, i: (i, h)),
        out_shape=jax.ShapeDtypeStruct((t, D_MODEL), BF16),
        scratch_shapes=[pltpu.VMEM((1, 2 * ROW_TILE), F32), pltpu.VMEM((1, 2 * ROW_TILE), F32),
                        pltpu.VMEM((DA_DV, 2 * ROW_TILE), F32), pltpu.VMEM((8, LANES), F32)],
        compiler_params=_cparams(("arbitrary", "arbitrary")),
        name="attention",
    )(q, k, vt, lam, subln_g.reshape(1, DA_DV))


def _proj_res_kernel(a_ref, w_ref, x_ref, gate_ref, o_ref):
    y = jnp.dot(a_ref[...], w_ref[...], preferred_element_type=F32)
    o_ref[...] = x_ref[...] + gate_ref[...] * y


def _proj_res(a, w, x, gate):
    t = x.shape[0]
    row = pl.BlockSpec((ROW_TILE, D_MODEL), lambda i: (i, 0))
    return pl.pallas_call(
        _proj_res_kernel,
        grid=(t // ROW_TILE,),
        in_specs=[row, pl.BlockSpec((D_MODEL, D_MODEL), lambda i: (0, 0)), row, _mod_spec()],
        out_specs=row,
        out_shape=jax.ShapeDtypeStruct((t, D_MODEL), F32),
        compiler_params=_cparams(("arbitrary",)),
        name="proj_res",
    )(a, w, x, gate)


def _moe_in_kernel(x_ref, sh_ref, sc_ref, wr_ref, h_ref, logit_ref):
    h = _modulated(x_ref[...], sh_ref[...], sc_ref[...])
    h_ref[...] = h.astype(BF16)
    logit_ref[...] = jnp.dot(h, wr_ref[...], preferred_element_type=F32,
                             precision=lax.Precision.HIGHEST)


def _moe_in(x, sh, sc, w_router_padded):
    t = x.shape[0]
    row = lambda width: pl.BlockSpec((ROW_TILE, width), lambda i: (i, 0))
    return pl.pallas_call(
        _moe_in_kernel,
        grid=(t // ROW_TILE,),
        in_specs=[row(D_MODEL), _mod_spec(), _mod_spec(),
                  pl.BlockSpec((D_MODEL, LANES), lambda i: (0, 0))],
        out_specs=[row(D_MODEL), row(LANES)],
        out_shape=[jax.ShapeDtypeStruct((t, D_MODEL), BF16), jax.ShapeDtypeStruct((t, LANES), F32)],
        compiler_params=_cparams(("arbitrary",)),
        name="moe_in",
    )(x, sh, sc, w_router_padded)


def _experts_kernel(blk_e_ref, n_used_ref, x_ref, wg_ref, wu_ref, wd_ref, o_ref):
    b = pl.program_id(0)
    f = pl.program_id(1)

    @pl.when(f == 0)
    def _():
        o_ref[...] = jnp.zeros_like(o_ref)

    @pl.when(b < n_used_ref[0])
    def _():
        x = x_ref[...]
        g = jnp.dot(x, wg_ref[...].astype(BF16), preferred_element_type=F32)
        u = jnp.dot(x, wu_ref[...].astype(BF16), preferred_element_type=F32)
        a = (_silu(g) * u).astype(BF16)
        o_ref[...] += jnp.dot(a, wd_ref[...].astype(BF16), preferred_element_type=F32)


def _experts(xg, blk_e, n_used, wg, wu, wd, j):
    n_rows = xg.shape[0]
    n_blocks = n_rows // MOE_TILE
    n_ff = E_FF // MOE_FF_TILE
    ff = lambda b, f, nu: jnp.where(b < nu[0], f, n_ff - 1)
    grid_spec = pltpu.PrefetchScalarGridSpec(
        num_scalar_prefetch=2,
        grid=(n_blocks, n_ff),
        in_specs=[
            pl.BlockSpec((MOE_TILE, D_MODEL), lambda b, f, be, nu: (b, 0)),
            pl.BlockSpec((None, None, D_MODEL, MOE_FF_TILE),
                         lambda b, f, be, nu: (j, be[b], 0, ff(b, f, nu))),
            pl.BlockSpec((None, None, D_MODEL, MOE_FF_TILE),
                         lambda b, f, be, nu: (j, be[b], 0, ff(b, f, nu))),
            pl.BlockSpec((None, None, MOE_FF_TILE, D_MODEL),
                         lambda b, f, be, nu: (j, be[b], ff(b, f, nu), 0)),
        ],
        out_specs=pl.BlockSpec((MOE_TILE, D_MODEL), lambda b, f, be, nu: (b, 0)),
    )
    return pl.pallas_call(
        _experts_kernel,
        grid_spec=grid_spec,
        out_shape=jax.ShapeDtypeStruct((n_rows, D_MODEL), F32),
        compiler_params=_cparams(("arbitrary", "arbitrary")),
        name="experts",
    )(blk_e, n_used, xg, wg, wu, wd)


def _combine_kernel(x_ref, gate_ref, y0_ref, y1_ref, w_ref, o_ref):
    w = w_ref[...]
    y = w[:, 0:1] * y0_ref[...] + w[:, 1:2] * y1_ref[...]
    o_ref[...] = x_ref[...] + gate_ref[...] * y


def _combine(x, gate, y0, y1, w):
    t = x.shape[0]
    row = pl.BlockSpec((ROW_TILE, D_MODEL), lambda i: (i, 0))
    return pl.pallas_call(
        _combine_kernel,
        grid=(t // ROW_TILE,),
        in_specs=[row, _mod_spec(), row, row, pl.BlockSpec((ROW_TILE, TOP_K), lambda i: (i, 0))],
        out_specs=row,
        out_shape=jax.ShapeDtypeStruct((t, D_MODEL), F32),
        compiler_params=_cparams(("arbitrary",)),
        name="moe_combine",
    )(x, gate, y0, y1, w)


def _moe(x, sh, sc, gate, w_router, wg, wu, wd, j):
    t = x.shape[0]
    wr = jnp.zeros((D_MODEL, LANES), F32).at[:, :N_EXPERTS].set(w_router)
    h, logits = _moe_in(x, sh, sc, wr)
    top_v, top_i = lax.top_k(logits[:, :N_EXPERTS], TOP_K)
    top_w = jax.nn.softmax(top_v, axis=-1)
    flat_e = top_i.reshape(-1)
    onehot = (flat_e[:, None] == jnp.arange(N_EXPERTS, dtype=flat_e.dtype)[None, :]).astype(jnp.int32)
    counts = jnp.sum(onehot, axis=0)
    rank = jnp.sum((jnp.cumsum(onehot, axis=0) - onehot) * onehot, axis=1)
    padded = ((counts + MOE_TILE - 1) // MOE_TILE) * MOE_TILE
    pad_end = jnp.cumsum(padded)
    pad_start = pad_end - padded
    dest = (pad_start[flat_e] + rank).astype(jnp.int32)
    n_blocks = (t * TOP_K + N_EXPERTS * (MOE_TILE - 1)) // MOE_TILE
    n_rows = n_blocks * MOE_TILE
    flat_tok = jnp.repeat(jnp.arange(t, dtype=jnp.int32), TOP_K)
    row_tok = jnp.zeros((n_rows,), jnp.int32).at[dest].set(flat_tok)
    blk_start = jnp.arange(n_blocks, dtype=jnp.int32) * MOE_TILE
    blk_e = jnp.minimum(jnp.sum(pad_end[None, :] <= blk_start[:, None], axis=1), N_EXPERTS - 1)
    n_used = (pad_end[-1:] // MOE_TILE).astype(jnp.int32)
    xg = jnp.take(h, row_tok, axis=0)
    out = _experts(xg, blk_e.astype(jnp.int32), n_used, wg, wu, wd, j)
    pos = dest.reshape(t, TOP_K)
    y0 = jnp.take(out, pos[:, 0], axis=0)
    y1 = jnp.take(out, pos[:, 1], axis=0)
    return _combine(x, gate, y0, y1, top_w)


def _final_kernel(x_ref, g_ref, o_ref):
    x = x_ref[...]
    ms = jnp.mean(x * x, axis=-1, keepdims=True)
    o_ref[...] = x * lax.rsqrt(ms + EPS) * g_ref[...]


def _final(x, g):
    n = x.shape[0] - CTX_LEN
    return pl.pallas_call(
        _final_kernel,
        grid=(n // ROW_TILE,),
        in_specs=[pl.BlockSpec((ROW_TILE, D_MODEL), lambda i: (i + 1, 0)),
                  pl.BlockSpec((1, D_MODEL), lambda i: (0, 0))],
        out_specs=pl.BlockSpec((ROW_TILE, D_MODEL), lambda i: (i, 0)),
        out_shape=jax.ShapeDtypeStruct((n, D_MODEL), F32),
        compiler_params=_cparams(("arbitrary",)),
        name="final_norm",
    )(x, g.reshape(1, D_MODEL))


def _rope_tables(n):
    n_rows = n // GRID_W
    inv = 1.0 / (ROPE_BASE ** (jnp.arange(0, ROPE_AXIS, 2, dtype=F32) / ROPE_AXIS))
    ar = jnp.arange(n_rows, dtype=F32)[:, None] * inv
    ac = jnp.arange(GRID_W, dtype=F32)[:, None] * inv
    by_row = lambda a: jnp.repeat(a, GRID_W, axis=0)
    by_col = lambda a: jnp.tile(a, (n_rows, 1))
    spread = lambda fn: jnp.concatenate([by_row(fn(ar))] * 2 + [by_col(fn(ac))] * 2, axis=-1)
    cos = spread(jnp.cos)
    sin = spread(jnp.sin)
    first = (jnp.arange(DA_DH) % ROPE_AXIS) < (ROPE_AXIS // 2)
    sa = jnp.where(first, -sin, 0.0)
    sb = jnp.where(first, 0.0, sin)
    pad = lambda a, fill: jnp.concatenate(
        [jnp.full((CTX_LEN, LANES), fill, F32), jnp.tile(a, (1, LANES // DA_DH))], axis=0)
    return pad(cos, 1.0), pad(sa, 0.0), pad(sb, 0.0)


def kernel(x, c, ctx, c_ctx, mod_w, mod_b, hg_lb, ev_w_in, hg_norm_g, sg_ln_g, sg_ln_b, sg_w, sg_b, ev_w_out, da_w_qkv, da_lam, da_subln_g, da_w_out, ffn_w_gate, ffn_w_up, ffn_w_down, moe_w_router, moe_w_gate, moe_w_up, moe_w_down, final_g):
    n = x.shape[1]
    xs = jnp.concatenate([ctx[0], x[0]], axis=0)
    cc_t = jnp.stack([c_ctx, c[0]], axis=1)
    mods = _ada_all(cc_t, mod_w, mod_b).reshape(DEPTH, 2, 6, 1, D_MODEL)
    cos, sa, sb = _rope_tables(n)
    lb_sm = jax.nn.softmax(hg_lb.astype(F32), axis=1)
    lb_all = jnp.cumsum(lb_sm, axis=1) - lb_sm[:, :1]

    for layer in range(DEPTH):
        j = layer // 2
        sh1, sc1, g1, sh2, sc2, g2 = (mods[layer, :, i] for i in range(6))
        if layer % 2 == 0:
            y = _even_in(xs, sh1, sc1, ev_w_in[j].astype(BF16))
            o_f = _hgrn(y, lb_all[0, j], rev=False)
            a = _hgrn(y, lb_all[1, j], rev=True, o_fwd=o_f, gain=hg_norm_g[j])
            xs = _even_out(a, y, sg_ln_g[j], sg_ln_b[j], sg_w[j].astype(BF16), sg_b[j].T,
                           ev_w_out[j].astype(BF16), xs, g1)
            xs = _ffn(xs, sh2, sc2, g2, ffn_w_gate[j].astype(BF16), ffn_w_up[j].astype(BF16),
                      ffn_w_down[j].astype(BF16))
        else:
            lam_init = 0.8 - 0.6 * math.exp(-0.3 * layer)
            q, k, vt = _qkv(xs, sh1, sc1, da_w_qkv[j].astype(BF16), cos, sa, sb)
            a = _attention(q, k, vt, da_lam[j], da_subln_g[j], lam_init)
            xs = _proj_res(a, da_w_out[j].astype(BF16), xs, g1)
            xs = _moe(xs, sh2, sc2, g2, moe_w_router[j], moe_w_gate, moe_w_up, moe_w_down, j)
    return _final(xs, final_g)[None]
```

```python
import functools
import math

import jax
import jax.numpy as jnp
from jax import lax
from jax.experimental import pallas as pl
from jax.experimental.pallas import tpu as pltpu

F32 = jnp.float32
BF16 = jnp.bfloat16

D_MODEL = 1024
DEPTH = 4
GRID_W = 64
CTX_LEN = 256
EPS = 1e-6

HG_WIDTH = 512
HG_HEADS = 4
HG_D = 128
HG_SUB = 16
HG_SAFE_DECAY = 80.0
SG_WIDTH = 512
SG_CHUNK = 128
SG_GROUPS = 4
EVEN_IN = 5 * HG_WIDTH + 2 * SG_WIDTH

DA_HEADS = 8
DA_DH = 64
DA_DV = 128
ROPE_BASE = 10000.0
ROPE_AXIS = DA_DH // 2

D_FF = 2816
N_EXPERTS = 8
TOP_K = 2
E_FF = 3584

LOG2E = 1.4426950408889634
LANES = 128
ROW_TILE = CTX_LEN
ATT_KV_TILE = 1280
ATT_UNROLL = 4
ATT_MIN_SUM = 2.0 ** -80
MOE_TILE = 1024
MOE_FF_TILE = 512
FFN_FF_TILE = 256
VMEM_LIMIT = 56 * 1024 * 1024


def _cparams(sem):
    return pltpu.CompilerParams(dimension_semantics=sem, vmem_limit_bytes=VMEM_LIMIT)


def _sigmoid(x):
    return 1.0 / (1.0 + jnp.exp(-x))


def _silu(x):
    return x * _sigmoid(x)


def _gelu(x):
    return 0.5 * x * (1.0 + lax.erf(x * (2.0 ** -0.5)))


def _modulated(x, sh, sc):
    ms = jnp.mean(x * x, axis=-1, keepdims=True)
    return x * lax.rsqrt(ms + EPS) * (1.0 + sc) + sh


def _mod_spec(width=D_MODEL):
    return pl.BlockSpec((None, 1, width), lambda i, *_: (jnp.minimum(i, 1), 0, 0))


def _ada_kernel(c_ref, w_ref, b_ref, o_ref):
    s = _silu(c_ref[...])
    w = w_ref[...]
    r0 = jnp.sum(s[:, 0:1] * w, axis=0, keepdims=True)
    r1 = jnp.sum(s[:, 1:2] * w, axis=0, keepdims=True)
    o_ref[0:1, :] = r0 + b_ref[...]
    o_ref[1:2, :] = r1 + b_ref[...]


def _ada_all(cc_t, mod_w, mod_b):
    n = 6 * D_MODEL
    tn = 1536
    return pl.pallas_call(
        _ada_kernel,
        grid=(DEPTH, n // tn),
        in_specs=[
            pl.BlockSpec((D_MODEL, 2), lambda l, j: (0, 0)),
            pl.BlockSpec((None, D_MODEL, tn), lambda l, j: (l, 0, j)),
            pl.BlockSpec((None, 1, tn), lambda l, j: (l, 0, j)),
        ],
        out_specs=pl.BlockSpec((None, 2, tn), lambda l, j: (l, 0, j)),
        out_shape=jax.ShapeDtypeStruct((DEPTH, 2, n), F32),
        compiler_params=_cparams(("arbitrary", "arbitrary")),
        name="ada",
    )(cc_t, mod_w, mod_b.reshape(DEPTH, 1, n))


def _even_in_kernel(x_ref, sh_ref, sc_ref, w_ref, o_ref):
    h = _modulated(x_ref[...], sh_ref[...], sc_ref[...]).astype(BF16)
    for n in range(0, EVEN_IN, 512):
        o_ref[:, n:n + 512] = jnp.dot(h, w_ref[:, n:n + 512], preferred_element_type=F32)


def _even_in(x, sh, sc, w):
    t = x.shape[0]
    return pl.pallas_call(
        _even_in_kernel,
        grid=(t // ROW_TILE,),
        in_specs=[
            pl.BlockSpec((ROW_TILE, D_MODEL), lambda i: (i, 0)),
            _mod_spec(), _mod_spec(),
            pl.BlockSpec((D_MODEL, EVEN_IN), lambda i: (0, 0)),
        ],
        out_specs=pl.BlockSpec((ROW_TILE, EVEN_IN), lambda i: (i, 0)),
        out_shape=jax.ShapeDtypeStruct((t, EVEN_IN), F32),
        compiler_params=_cparams(("arbitrary",)),
        name="even_in",
    )(x, sh, sc, w)


def _group_scan(x, r_in_group, forward):
    rows = x.shape[0]
    for sh in (1, 2, 4, 8):
        if forward:
            rolled = pltpu.roll(x, sh, 0)
            ok = r_in_group >= sh
        else:
            rolled = pltpu.roll(x, rows - sh, 0)
            ok = r_in_group < HG_SUB - sh
        x = x + jnp.where(ok, rolled, 0.0)
    return x


def _hgrn_kernel(*refs, rev, readout):
    if readout:
        q_ref, f_ref, i_ref, lb_ref, of_ref, g_ref, gain_ref, o_ref, st_ref, acc_ref = refs
    else:
        q_ref, f_ref, i_ref, lb_ref, o_ref, st_ref, acc_ref = refs
    rows = q_ref.shape[0]
    n_groups = rows // HG_SUB

    @pl.when(pl.program_id(1) == 0)
    def _():
        st_ref[...] = jnp.zeros_like(st_ref)

    f = f_ref[...]
    lb = lb_ref[...]
    t_abs = jnp.exp(-jnp.abs(f))
    big = 1.0 / (1.0 + t_abs)
    small = t_abs * big
    log_f = jnp.log(lb + (1.0 - lb) * jnp.where(f >= 0, big, small))
    kk = (1.0 - lb) * jnp.where(f >= 0, small, big)
    qs = _silu(q_ref[...])
    v = i_ref[...]

    row = lax.broadcasted_iota(jnp.int32, (rows, HG_D), 0)
    r16 = row & (HG_SUB - 1)
    grp = row // HG_SUB
    pfx = _group_scan(log_f, r16, True)
    sfx = _group_scan(log_f, r16, False)
    cum, other = (sfx, pfx) if rev else (pfx, sfx)
    qd = (qs * jnp.exp(cum)).astype(BF16)
    kd = (kk * jnp.exp(other - log_f)).astype(BF16)

    kn = (kk * jnp.exp(-cum)).astype(BF16)
    sc = lax.dot_general(qd, kn, (((1,), (1,)), ((), ())), preferred_element_type=F32)
    ti = lax.broadcasted_iota(jnp.int32, (rows, rows), 0)
    si = lax.broadcasted_iota(jnp.int32, (rows, rows), 1)
    reach = ti & (HG_SUB - 1)
    dist, reach = (si - ti, HG_SUB - 1 - reach) if rev else (ti - si, reach)
    ok = dist.astype(jnp.uint32) <= reach.astype(jnp.uint32)
    acc_factored = jnp.dot(jnp.where(ok, sc, 0.0).astype(BF16), v.astype(BF16),
                           preferred_element_type=F32)

    v_t = v.T.astype(BF16)
    bd = jnp.concatenate(
        [jnp.where(grp == g, kd, jnp.zeros_like(kd)) for g in range(n_groups)], axis=1)
    u = jnp.dot(v_t, bd, preferred_element_type=F32)

    st = st_ref[...]
    outs = [None] * n_groups
    for g in (range(n_groups - 1, -1, -1) if rev else range(n_groups)):
        lo = g * HG_SUB
        outs[g] = lax.dot_general(qd[lo:lo + HG_SUB], st.astype(BF16),
                                  (((1,), (1,)), ((), ())), preferred_element_type=F32)
        edge = lo if rev else lo + HG_SUB - 1
        st = st * jnp.exp(cum[edge:edge + 1, :]) + u[:, g * HG_D:(g + 1) * HG_D]
    st_ref[...] = st
    o_inter = jnp.concatenate(outs, axis=0)
    acc_ref[...] = acc_factored + o_inter

    @pl.when(jnp.logical_not(jnp.min(cum) >= -HG_SAFE_DECAY))
    def _():
        ones = jnp.ones((HG_D, HG_D), BF16)
        acc = o_inter
        for d in range(HG_SUB):
            if d == 0:
                kk_s, cum_s, v_s = kk, cum, v
                p = qs * kk_s
            else:
                shift = rows - d if rev else d
                kk_s = pltpu.roll(kk, shift, 0)
                cum_s = pltpu.roll(cum, shift, 0)
                v_s = pltpu.roll(v, shift, 0)
                ok_d = (r16 < HG_SUB - d) if rev else (r16 >= d)
                p = jnp.where(ok_d, qs * kk_s * jnp.exp(cum - cum_s), 0.0)
            rsum = jnp.dot(p.astype(BF16), ones, preferred_element_type=F32)
            acc = acc + rsum * v_s
        acc_ref[...] = acc

    o = acc_ref[...]

    if readout:
        o = o + of_ref[...]
        ms = jnp.mean(o * o, axis=-1, keepdims=True)
        o = o * lax.rsqrt(ms + EPS) * gain_ref[...] * _silu(g_ref[...])
    o_ref[...] = o.astype(o_ref.dtype)


def _hgrn(y, lb, rev, o_fwd=None, gain=None):
    t = y.shape[0]
    nblk = t // ROW_TILE
    readout = o_fwd is not None
    if rev:
        blk = lambda c: jnp.where(c == 0, 0, nblk - c)
    else:
        blk = lambda c: c
    col = lambda slab: pl.BlockSpec((ROW_TILE, HG_D), lambda h, c: (blk(c), slab * HG_HEADS + h))
    in_specs = [col(0), col(2 if rev else 1), col(3),
                pl.BlockSpec((None, 1, HG_D), lambda h, c: (h, 0, 0))]
    args = [y, y, y, lb.reshape(HG_HEADS, 1, HG_D)]
    if readout:
        in_specs += [pl.BlockSpec((ROW_TILE, HG_D), lambda h, c: (blk(c), h)), col(4),
                     pl.BlockSpec((1, HG_D), lambda h, c: (0, 0))]
        args += [o_fwd, y, gain.reshape(1, HG_D)]
    return pl.pallas_call(
        functools.partial(_hgrn_kernel, rev=rev, readout=readout),
        grid=(HG_HEADS, nblk),
        in_specs=in_specs,
        out_specs=pl.BlockSpec((ROW_TILE, HG_D), lambda h, c: (blk(c), h)),
        out_shape=jax.ShapeDtypeStruct((t, HG_WIDTH), BF16 if readout else F32),
        scratch_shapes=[pltpu.VMEM((HG_D, HG_D), F32), pltpu.VMEM((ROW_TILE, HG_D), F32)],
        compiler_params=_cparams(("arbitrary", "arbitrary")),
        name="hgrn_bwd" if rev else "hgrn_fwd",
    )(*args)


def _even_out_kernel(a_ref, u_ref, v_ref, lng_ref, lnb_ref, sgw_ref, sgb_ref, w_ref, x_ref,
                     gate_ref, o_ref):
    u = _gelu(u_ref[...])
    v = _gelu(v_ref[...])
    mu = jnp.mean(v, axis=-1, keepdims=True)
    var = jnp.mean(jnp.square(v - mu), axis=-1, keepdims=True)
    vn = ((v - mu) * lax.rsqrt(var + 1e-5) * lng_ref[...] + lnb_ref[...]).astype(BF16)
    rows = u.shape[0]
    mixed_rows = []
    for c in range(rows // SG_CHUNK):
        lo = c * SG_CHUNK
        parts = []
        for g in range(SG_GROUPS):
            m = jnp.dot(sgw_ref[g], vn[lo:lo + SG_CHUNK, g * LANES:(g + 1) * LANES],
                        preferred_element_type=F32)
            parts.append(m + sgb_ref[:, g:g + 1])
        mixed_rows.append(jnp.concatenate(parts, axis=1))
    s = (u * jnp.concatenate(mixed_rows, axis=0)).astype(BF16)
    y = jnp.dot(a_ref[...], w_ref[0:HG_WIDTH, :], preferred_element_type=F32)
    y = y + jnp.dot(s, w_ref[HG_WIDTH:, :], preferred_element_type=F32)
    o_ref[...] = x_ref[...] + gate_ref[...] * y


def _even_out(a, y_proj, ln_g, ln_b, sg_w, sg_b_t, w_out, x, gate):
    t = x.shape[0]
    full = lambda shape: pl.BlockSpec(shape, lambda i: (0,) * len(shape))
    return pl.pallas_call(
        _even_out_kernel,
        grid=(t // ROW_TILE,),
        in_specs=[
            pl.BlockSpec((ROW_TILE, HG_WIDTH), lambda i: (i, 0)),
            pl.BlockSpec((ROW_TILE, SG_WIDTH), lambda i: (i, 5)),
            pl.BlockSpec((ROW_TILE, SG_WIDTH), lambda i: (i, 6)),
            full((1, SG_WIDTH)), full((1, SG_WIDTH)),
            full((SG_GROUPS, SG_CHUNK, SG_CHUNK)), full((SG_CHUNK, SG_GROUPS)),
            full((D_MODEL, D_MODEL)),
            pl.BlockSpec((ROW_TILE, D_MODEL), lambda i: (i, 0)),
            _mod_spec(),
        ],
        out_specs=pl.BlockSpec((ROW_TILE, D_MODEL), lambda i: (i, 0)),
        out_shape=jax.ShapeDtypeStruct((t, D_MODEL), F32),
        compiler_params=_cparams(("arbitrary",)),
        name="even_out",
    )(a, y_proj, y_proj, ln_g.reshape(1, -1), ln_b.reshape(1, -1), sg_w, sg_b_t, w_out, x, gate)


def _ffn_kernel(x_ref, sh_ref, sc_ref, gate_ref, wg_ref, wu_ref, wd_ref, o_ref):
    x = x_ref[...]
    h = _modulated(x, sh_ref[...], sc_ref[...]).astype(BF16)
    y = jnp.zeros(x.shape, F32)
    for n in range(0, D_FF, FFN_FF_TILE):
        g = jnp.dot(h, wg_ref[:, n:n + FFN_FF_TILE], preferred_element_type=F32)
        u = jnp.dot(h, wu_ref[:, n:n + FFN_FF_TILE], preferred_element_type=F32)
        a = (_silu(g) * u).astype(BF16)
        y = y + jnp.dot(a, wd_ref[n:n + FFN_FF_TILE, :], preferred_element_type=F32)
    o_ref[...] = x + gate_ref[...] * y


def _ffn(x, sh, sc, gate, wg, wu, wd):
    t = x.shape[0]
    full = lambda shape: pl.BlockSpec(shape, lambda i: (0,) * len(shape))
    return pl.pallas_call(
        _ffn_kernel,
        grid=(t // ROW_TILE,),
        in_specs=[
            pl.BlockSpec((ROW_TILE, D_MODEL), lambda i: (i, 0)),
            _mod_spec(), _mod_spec(), _mod_spec(),
            full((D_MODEL, D_FF)), full((D_MODEL, D_FF)), full((D_FF, D_MODEL)),
        ],
        out_specs=pl.BlockSpec((ROW_TILE, D_MODEL), lambda i: (i, 0)),
        out_shape=jax.ShapeDtypeStruct((t, D_MODEL), F32),
        compiler_params=_cparams(("arbitrary",)),
        name="ffn",
    )(x, sh, sc, gate, wg, wu, wd)


def _qkv_kernel(x_ref, sh_ref, sc_ref, w_ref, cos_ref, sa_ref, sb_ref, q_ref, k_ref, vt_ref):
    h = _modulated(x_ref[...], sh_ref[...], sc_ref[...]).astype(BF16)
    cos = cos_ref[...]
    sa = sa_ref[...]
    sb = sb_ref[...]

    def rope(z):
        return z * cos + pltpu.roll(z, LANES - 16, 1) * sa + pltpu.roll(z, 16, 1) * sb

    for n in range(0, D_MODEL, LANES):
        q = jnp.dot(h, w_ref[:, n:n + LANES], preferred_element_type=F32)
        q_ref[:, n:n + LANES] = (rope(q) * (DA_DH ** -0.5 * LOG2E)).astype(BF16)
        k = jnp.dot(h, w_ref[:, D_MODEL + n:D_MODEL + n + LANES], preferred_element_type=F32)
        k_ref[:, n:n + LANES] = rope(k).astype(BF16)
    for n in range(0, D_MODEL, 512):
        v = jnp.dot(h, w_ref[:, 2 * D_MODEL + n:2 * D_MODEL + n + 512], preferred_element_type=F32)
        vt_ref[n:n + 512, :] = v.T.astype(BF16)


def _qkv(x, sh, sc, w, cos, sa, sb):
    t = x.shape[0]
    row = lambda width: pl.BlockSpec((ROW_TILE, width), lambda i: (i, 0))
    return pl.pallas_call(
        _qkv_kernel,
        grid=(t // ROW_TILE,),
        in_specs=[row(D_MODEL), _mod_spec(), _mod_spec(),
                  pl.BlockSpec((D_MODEL, 3 * D_MODEL), lambda i: (0, 0)),
                  row(LANES), row(LANES), row(LANES)],
        out_specs=[row(D_MODEL), row(D_MODEL), pl.BlockSpec((D_MODEL, ROW_TILE), lambda i: (0, i))],
        out_shape=[jax.ShapeDtypeStruct((t, D_MODEL), BF16), jax.ShapeDtypeStruct((t, D_MODEL), BF16),
                   jax.ShapeDtypeStruct((D_MODEL, t), BF16)],
        compiler_params=_cparams(("arbitrary",)),
        name="qkv",
    )(x, sh, sc, w, cos, sa, sb)


def _attn_kernel(q_ref, k_ref, vt_ref, lam_ref, g_ref, o_ref, m_ref, l_ref, acc_ref, kmax_ref, *,
                 lam_init, n_keys):
    tq = q_ref.shape[0]
    q = q_ref[...]
    lane = lax.broadcasted_iota(jnp.int32, q.shape, 1)
    zero = jnp.zeros_like(q)
    qq = jnp.concatenate([jnp.where(lane < DA_DH, q, zero), jnp.where(lane >= DA_DH, q, zero)], axis=0)

    tk = ATT_KV_TILE
    n_chunks = n_keys // tk
    is_ctx = pl.program_id(1) == 0
    ones8 = jnp.ones((8, DA_DV), BF16)
    nt = (((1,), (1,)), ((), ()))

    @pl.when(is_ctx)
    def _():
        lane8 = lax.broadcasted_iota(jnp.int32, (8, DA_DV), 1)
        row8 = lax.broadcasted_iota(jnp.int32, (8, DA_DV), 0)
        sel = jnp.where((lane8 >= DA_DH) == (row8 == 1), 1.0, 0.0).astype(BF16)

        def body(c, best):
            kc = k_ref[pl.ds(pl.multiple_of(c * tk, tk), tk), :].astype(F32)
            n2 = lax.dot_general(sel, (kc * kc).astype(BF16), nt, preferred_element_type=F32)
            return jnp.maximum(best, n2)
        best = lax.fori_loop(0, n_chunks, body, jnp.zeros((8, tk), F32))
        kmax_ref[...] = jnp.broadcast_to(jnp.max(best, axis=1, keepdims=True), kmax_ref.shape)

    qf = qq.astype(F32)
    qn2 = lax.dot_general(ones8, (qf * qf).astype(BF16), nt, preferred_element_type=F32)[0:1]
    col = lax.broadcasted_iota(jnp.int32, (1, 2 * tq), 1)
    kn2 = jnp.where(col < tq, kmax_ref[0:1, 0:1], kmax_ref[1:2, 0:1])
    m_bound = jnp.sqrt(qn2 * kn2)

    def scores(off, size):
        k = k_ref[pl.ds(off, size), :]
        return lax.dot_general(k, qq, nt, preferred_element_type=F32)

    def fast_chunk(off, size):
        p = jnp.exp2(scores(off, size) - m_bound)
        l_ref[...] += jnp.sum(p, axis=0, keepdims=True)
        acc_ref[...] += jnp.dot(vt_ref[:, pl.ds(off, size)], p.astype(BF16), preferred_element_type=F32)

    def online_chunk(off, size):
        s = scores(off, size)
        m_old = m_ref[...]
        m_new = jnp.maximum(m_old, jnp.max(s, axis=0, keepdims=True))
        alpha = jnp.exp2(m_old - m_new)
        p = jnp.exp2(s - m_new)
        l_ref[...] = alpha * l_ref[...] + jnp.sum(p, axis=0, keepdims=True)
        acc_ref[...] = alpha * acc_ref[...] + jnp.dot(vt_ref[:, pl.ds(off, size)], p.astype(BF16),
                                                      preferred_element_type=F32)
        m_ref[...] = m_new

    def all_keys(chunk, unroll):
        @pl.when(is_ctx)
        def _():
            chunk(0, CTX_LEN)

        @pl.when(jnp.logical_not(is_ctx))
        def _():
            def body(jj, carry):
                for u in range(unroll):
                    chunk(pl.multiple_of((unroll * jj + u) * tk, tk), tk)
                return carry
            lax.fori_loop(0, n_chunks // unroll, body, 0)
            for c in range(n_chunks - n_chunks % unroll, n_chunks):
                chunk(c * tk, tk)

    l_ref[...] = jnp.zeros_like(l_ref)
    acc_ref[...] = jnp.zeros_like(acc_ref)
    all_keys(fast_chunk, ATT_UNROLL)

    @pl.when(jnp.logical_not(jnp.min(l_ref[...]) >= ATT_MIN_SUM))
    def _():
        m_ref[...] = jnp.full_like(m_ref, -jnp.inf)
        l_ref[...] = jnp.zeros_like(l_ref)
        acc_ref[...] = jnp.zeros_like(acc_ref)
        all_keys(online_chunk, 1)

    lam = lam_ref[...]
    lam_full = (jnp.exp(jnp.sum(lam[0:1] * lam[1:2], axis=-1, keepdims=True))
                - jnp.exp(jnp.sum(lam[2:3] * lam[3:4], axis=-1, keepdims=True)) + lam_init)
    o_all = acc_ref[...] / l_ref[...]
    o = (o_all[:, 0:tq] - lam_full * o_all[:, tq:2 * tq]).T
    ms = jnp.mean(o * o, axis=-1, keepdims=True)
    o = o * lax.rsqrt(ms + 1e-5) * g_ref[...] * (1.0 - lam_init)
    o_ref[...] = o.astype(o_ref.dtype)


def _attention(q, k, vt, lam, subln_g, lam_init):
    t = q.shape[0]
    assert t % ATT_KV_TILE == 0
    return pl.pallas_call(
        functools.partial(_attn_kernel, lam_init=lam_init, n_keys=t),
        grid=(DA_HEADS, t // ROW_TILE),
        in_specs=[
            pl.BlockSpec((ROW_TILE, DA_DV), lambda h, i: (i, h)),
            pl.BlockSpec((t, DA_DV), lambda h, i: (0, h)),
            pl.BlockSpec((DA_DV, t), lambda h, i: (h, 0)),
            pl.BlockSpec((4, DA_DH), lambda h, i: (0, 0)),
            pl.BlockSpec((1, DA_DV), lambda h, i: (0, 0)),
        ],
        out_specs=pl.BlockSpec((ROW_TILE, DA_DV), lambda h, i: (i, h)),
        out_shape=jax.ShapeDtypeStruct((t, D_MODEL), BF16),
        scratch_shapes=[pltpu.VMEM((1, 2 * ROW_TILE), F32), pltpu.VMEM((1, 2 * ROW_TILE), F32),
                        pltpu.VMEM((DA_DV, 2 * ROW_TILE), F32), pltpu.VMEM((8, LANES), F32)],
        compiler_params=_cparams(("arbitrary", "arbitrary")),
        name="attention",
    )(q, k, vt, lam, subln_g.reshape(1, DA_DV))


def _proj_res_kernel(a_ref, w_ref, x_ref, gate_ref, o_ref):
    y = jnp.dot(a_ref[...], w_ref[...], preferred_element_type=F32)
    o_ref[...] = x_ref[...] + gate_ref[...] * y


def _proj_res(a, w, x, gate):
    t = x.shape[0]
    row = pl.BlockSpec((ROW_TILE, D_MODEL), lambda i: (i, 0))
    return pl.pallas_call(
        _proj_res_kernel,
        grid=(t // ROW_TILE,),
        in_specs=[row, pl.BlockSpec((D_MODEL, D_MODEL), lambda i: (0, 0)), row, _mod_spec()],
        out_specs=row,
        out_shape=jax.ShapeDtypeStruct((t, D_MODEL), F32),
        compiler_params=_cparams(("arbitrary",)),
        name="proj_res",
    )(a, w, x, gate)


HALF = D_MODEL // 2


def _pack_bf16_pairs(h):
    bits = pltpu.bitcast(h.astype(BF16).astype(F32), jnp.uint32)
    return (bits[:, HALF:] & jnp.uint32(0xFFFF0000)) | (bits[:, :HALF] >> 16)


def _unpack_bf16_pairs(w):
    lo = pltpu.bitcast(w << 16, F32)
    hi = pltpu.bitcast(w & jnp.uint32(0xFFFF0000), F32)
    return jnp.concatenate([lo, hi], axis=1).astype(BF16)


def _moe_in_kernel(x_ref, sh_ref, sc_ref, wr_ref, h_ref, logit_ref):
    h = _modulated(x_ref[...], sh_ref[...], sc_ref[...])
    h_ref[...] = _pack_bf16_pairs(h)
    logit_ref[...] = jnp.dot(h, wr_ref[...], preferred_element_type=F32,
                             precision=lax.Precision.HIGHEST)


def _moe_in(x, sh, sc, w_router_padded):
    t = x.shape[0]
    row = lambda width: pl.BlockSpec((ROW_TILE, width), lambda i: (i, 0))
    return pl.pallas_call(
        _moe_in_kernel,
        grid=(t // ROW_TILE,),
        in_specs=[row(D_MODEL), _mod_spec(), _mod_spec(),
                  pl.BlockSpec((D_MODEL, LANES), lambda i: (0, 0))],
        out_specs=[row(HALF), row(LANES)],
        out_shape=[jax.ShapeDtypeStruct((t, HALF), jnp.uint32), jax.ShapeDtypeStruct((t, LANES), F32)],
        compiler_params=_cparams(("arbitrary",)),
        name="moe_in",
    )(x, sh, sc, w_router_padded)


def _dispatch_kernel(tok_ref, h_ref, o_ref):
    def body(g, carry):
        base = pl.multiple_of(g * 8, 8)
        rows = [h_ref[pl.ds(tok_ref[0, 0, base + u], 1), :] for u in range(8)]
        o_ref[pl.ds(base, 8), :] = jnp.concatenate(rows, axis=0)
        return carry
    lax.fori_loop(0, o_ref.shape[0] // 8, body, 0)


def _dispatch(h_packed, row_tok):
    n_rows = row_tok.shape[0]
    n_blocks = n_rows // MOE_TILE
    return pl.pallas_call(
        _dispatch_kernel,
        grid=(n_blocks,),
        in_specs=[
            pl.BlockSpec((1, 1, MOE_TILE), lambda b: (b, 0, 0), memory_space=pltpu.SMEM),
            pl.BlockSpec(memory_space=pltpu.VMEM),
        ],
        out_specs=pl.BlockSpec((MOE_TILE, HALF), lambda b: (b, 0)),
        out_shape=jax.ShapeDtypeStruct((n_rows, HALF), jnp.uint32),
        compiler_params=_cparams(("arbitrary",)),
        name="moe_dispatch",
    )(row_tok.reshape(n_blocks, 1, MOE_TILE), h_packed)


def _experts_kernel(blk_e_ref, n_used_ref, x_ref, wg_ref, wu_ref, wd_ref, o_ref, xb_ref):
    b = pl.program_id(0)
    f = pl.program_id(1)

    @pl.when(f == 0)
    def _():
        o_ref[...] = jnp.zeros_like(o_ref)
        xb_ref[...] = _unpack_bf16_pairs(x_ref[...])

    @pl.when(b < n_used_ref[0])
    def _():
        x = xb_ref[...]
        g = jnp.dot(x, wg_ref[...].astype(BF16), preferred_element_type=F32)
        u = jnp.dot(x, wu_ref[...].astype(BF16), preferred_element_type=F32)
        a = (_silu(g) * u).astype(BF16)
        o_ref[...] += jnp.dot(a, wd_ref[...].astype(BF16), preferred_element_type=F32)


def _experts(xg, blk_e, n_used, wg, wu, wd, j):
    n_rows = xg.shape[0]
    n_blocks = n_rows // MOE_TILE
    n_ff = E_FF // MOE_FF_TILE
    ff = lambda b, f, nu: jnp.where(b < nu[0], f, n_ff - 1)
    grid_spec = pltpu.PrefetchScalarGridSpec(
        num_scalar_prefetch=2,
        grid=(n_blocks, n_ff),
        in_specs=[
            pl.BlockSpec((MOE_TILE, HALF), lambda b, f, be, nu: (b, 0)),
            pl.BlockSpec((None, None, D_MODEL, MOE_FF_TILE),
                         lambda b, f, be, nu: (j, be[b], 0, ff(b, f, nu))),
            pl.BlockSpec((None, None, D_MODEL, MOE_FF_TILE),
                         lambda b, f, be, nu: (j, be[b], 0, ff(b, f, nu))),
            pl.BlockSpec((None, None, MOE_FF_TILE, D_MODEL),
                         lambda b, f, be, nu: (j, be[b], ff(b, f, nu), 0)),
        ],
        out_specs=pl.BlockSpec((MOE_TILE, D_MODEL), lambda b, f, be, nu: (b, 0)),
        scratch_shapes=[pltpu.VMEM((MOE_TILE, D_MODEL), BF16)],
    )
    return pl.pallas_call(
        _experts_kernel,
        grid_spec=grid_spec,
        out_shape=jax.ShapeDtypeStruct((n_rows, D_MODEL), F32),
        compiler_params=_cparams(("arbitrary", "arbitrary")),
        name="experts",
    )(blk_e, n_used, xg, wg, wu, wd)


def _combine_kernel(x_ref, gate_ref, y0_ref, y1_ref, w_ref, o_ref):
    w = w_ref[...]
    y = w[:, 0:1] * y0_ref[...] + w[:, 1:2] * y1_ref[...]
    o_ref[...] = x_ref[...] + gate_ref[...] * y


def _combine(x, gate, y0, y1, w):
    t = x.shape[0]
    row = pl.BlockSpec((ROW_TILE, D_MODEL), lambda i: (i, 0))
    return pl.pallas_call(
        _combine_kernel,
        grid=(t // ROW_TILE,),
        in_specs=[row, _mod_spec(), row, row, pl.BlockSpec((ROW_TILE, TOP_K), lambda i: (i, 0))],
        out_specs=row,
        out_shape=jax.ShapeDtypeStruct((t, D_MODEL), F32),
        compiler_params=_cparams(("arbitrary",)),
        name="moe_combine",
    )(x, gate, y0, y1, w)


def _moe(x, sh, sc, gate, w_router, wg, wu, wd, j):
    t = x.shape[0]
    wr = jnp.zeros((D_MODEL, LANES), F32).at[:, :N_EXPERTS].set(w_router)
    h, logits = _moe_in(x, sh, sc, wr)
    top_v, top_i = lax.top_k(logits[:, :N_EXPERTS], TOP_K)
    top_w = jax.nn.softmax(top_v, axis=-1)
    flat_e = top_i.reshape(-1)
    onehot = (flat_e[:, None] == jnp.arange(N_EXPERTS, dtype=flat_e.dtype)[None, :]).astype(jnp.int32)
    counts = jnp.sum(onehot, axis=0)
    rank = jnp.sum((jnp.cumsum(onehot, axis=0) - onehot) * onehot, axis=1)
    padded = ((counts + MOE_TILE - 1) // MOE_TILE) * MOE_TILE
    pad_end = jnp.cumsum(padded)
    pad_start = pad_end - padded
    dest = (pad_start[flat_e] + rank).astype(jnp.int32)
    n_blocks = (t * TOP_K + N_EXPERTS * (MOE_TILE - 1)) // MOE_TILE
    n_rows = n_blocks * MOE_TILE
    flat_tok = jnp.repeat(jnp.arange(t, dtype=jnp.int32), TOP_K)
    row_tok = jnp.zeros((n_rows,), jnp.int32).at[dest].set(flat_tok)
    blk_start = jnp.arange(n_blocks, dtype=jnp.int32) * MOE_TILE
    blk_e = jnp.minimum(jnp.sum(pad_end[None, :] <= blk_start[:, None], axis=1), N_EXPERTS - 1)
    n_used = (pad_end[-1:] // MOE_TILE).astype(jnp.int32)
    xg = _dispatch(h, row_tok)
    out = _experts(xg, blk_e.astype(jnp.int32), n_used, wg, wu, wd, j)
    pos = dest.reshape(t, TOP_K)
    y0 = jnp.take(out, pos[:, 0], axis=0)
    y1 = jnp.take(out, pos[:, 1], axis=0)
    return _combine(x, gate, y0, y1, top_w)


def _final_kernel(x_ref, g_ref, o_ref):
    x = x_ref[...]
    ms = jnp.mean(x * x, axis=-1, keepdims=True)
    o_ref[...] = x * lax.rsqrt(ms + EPS) * g_ref[...]


def _final(x, g):
    n = x.shape[0] - CTX_LEN
    return pl.pallas_call(
        _final_kernel,
        grid=(n // ROW_TILE,),
        in_specs=[pl.BlockSpec((ROW_TILE, D_MODEL), lambda i: (i + 1, 0)),
                  pl.BlockSpec((1, D_MODEL), lambda i: (0, 0))],
        out_specs=pl.BlockSpec((ROW_TILE, D_MODEL), lambda i: (i, 0)),
        out_shape=jax.ShapeDtypeStruct((n, D_MODEL), F32),
        compiler_params=_cparams(("arbitrary",)),
        name="final_norm",
    )(x, g.reshape(1, D_MODEL))


def _rope_tables(n):
    n_rows = n // GRID_W
    inv = 1.0 / (ROPE_BASE ** (jnp.arange(0, ROPE_AXIS, 2, dtype=F32) / ROPE_AXIS))
    ar = jnp.arange(n_rows, dtype=F32)[:, None] * inv
    ac = jnp.arange(GRID_W, dtype=F32)[:, None] * inv
    by_row = lambda a: jnp.repeat(a, GRID_W, axis=0)
    by_col = lambda a: jnp.tile(a, (n_rows, 1))
    spread = lambda fn: jnp.concatenate([by_row(fn(ar))] * 2 + [by_col(fn(ac))] * 2, axis=-1)
    cos = spread(jnp.cos)
    sin = spread(jnp.sin)
    first = (jnp.arange(DA_DH) % ROPE_AXIS) < (ROPE_AXIS // 2)
    sa = jnp.where(first, -sin, 0.0)
    sb = jnp.where(first, 0.0, sin)
    pad = lambda a, fill: jnp.concatenate(
        [jnp.full((CTX_LEN, LANES), fill, F32), jnp.tile(a, (1, LANES // DA_DH))], axis=0)
    return pad(cos, 1.0), pad(sa, 0.0), pad(sb, 0.0)


def kernel(x, c, ctx, c_ctx, mod_w, mod_b, hg_lb, ev_w_in, hg_norm_g, sg_ln_g, sg_ln_b, sg_w, sg_b, ev_w_out, da_w_qkv, da_lam, da_subln_g, da_w_out, ffn_w_gate, ffn_w_up, ffn_w_down, moe_w_router, moe_w_gate, moe_w_up, moe_w_down, final_g):
    n = x.shape[1]
    xs = jnp.concatenate([ctx[0], x[0]], axis=0)
    cc_t = jnp.stack([c_ctx, c[0]], axis=1)
    mods = _ada_all(cc_t, mod_w, mod_b).reshape(DEPTH, 2, 6, 1, D_MODEL)
    cos, sa, sb = _rope_tables(n)
    lb_sm = jax.nn.softmax(hg_lb.astype(F32), axis=1)
    lb_all = jnp.cumsum(lb_sm, axis=1) - lb_sm[:, :1]

    for layer in range(DEPTH):
        j = layer // 2
        sh1, sc1, g1, sh2, sc2, g2 = (mods[layer, :, i] for i in range(6))
        if layer % 2 == 0:
            y = _even_in(xs, sh1, sc1, ev_w_in[j].astype(BF16))
            o_f = _hgrn(y, lb_all[0, j], rev=False)
            a = _hgrn(y, lb_all[1, j], rev=True, o_fwd=o_f, gain=hg_norm_g[j])
            xs = _even_out(a, y, sg_ln_g[j], sg_ln_b[j], sg_w[j].astype(BF16), sg_b[j].T,
                           ev_w_out[j].astype(BF16), xs, g1)
            xs = _ffn(xs, sh2, sc2, g2, ffn_w_gate[j].astype(BF16), ffn_w_up[j].astype(BF16),
                      ffn_w_down[j].astype(BF16))
        else:
            lam_init = 0.8 - 0.6 * math.exp(-0.3 * layer)
            q, k, vt = _qkv(xs, sh1, sc1, da_w_qkv[j].astype(BF16), cos, sa, sb)
            a = _attention(q, k, vt, da_lam[j], da_subln_g[j], lam_init)
            xs = _proj_res(a, da_w_out[j].astype(BF16), xs, g1)
            xs = _moe(xs, sh2, sc2, g2, moe_w_router[j], moe_w_gate, moe_w_up, moe_w_down, j)
    return _final(xs, final_g)[None]
```

```python
import functools
import math

import jax
import jax.numpy as jnp
from jax import lax
from jax.experimental import pallas as pl
from jax.experimental.pallas import tpu as pltpu

F32 = jnp.float32
BF16 = jnp.bfloat16

D_MODEL = 1024
DEPTH = 4
GRID_W = 64
CTX_LEN = 256
EPS = 1e-6

HG_WIDTH = 512
HG_HEADS = 4
HG_D = 128
HG_SUB = 16
HG_SAFE_DECAY = 80.0
SG_WIDTH = 512
SG_CHUNK = 128
SG_GROUPS = 4
EVEN_IN = 5 * HG_WIDTH + 2 * SG_WIDTH

DA_HEADS = 8
DA_DH = 64
DA_DV = 128
ROPE_BASE = 10000.0
ROPE_AXIS = DA_DH // 2

D_FF = 2816
N_EXPERTS = 8
TOP_K = 2
E_FF = 3584

LOG2E = 1.4426950408889634
LANES = 128
ROW_TILE = CTX_LEN
ATT_KV_TILE = 1280
ATT_UNROLL = 13
ATT_MIN_SUM = 2.0 ** -80
MOE_TILE = 1024
MOE_FF_TILE = 512
FFN_FF_TILE = 256
VMEM_LIMIT = 56 * 1024 * 1024


def _cparams(sem):
    return pltpu.CompilerParams(dimension_semantics=sem, vmem_limit_bytes=VMEM_LIMIT)


def _sigmoid(x):
    return 1.0 / (1.0 + jnp.exp(-x))


def _silu(x):
    return x * _sigmoid(x)


def _gelu(x):
    return 0.5 * x * (1.0 + lax.erf(x * (2.0 ** -0.5)))


def _modulated(x, sh, sc):
    ms = jnp.mean(x * x, axis=-1, keepdims=True)
    return x * lax.rsqrt(ms + EPS) * (1.0 + sc) + sh


def _mod_spec(width=D_MODEL):
    return pl.BlockSpec((None, 1, width), lambda i, *_: (jnp.minimum(i, 1), 0, 0))


def _ada_kernel(c_ref, w_ref, b_ref, o_ref):
    s = _silu(c_ref[...])
    w = w_ref[...]
    r0 = jnp.sum(s[:, 0:1] * w, axis=0, keepdims=True)
    r1 = jnp.sum(s[:, 1:2] * w, axis=0, keepdims=True)
    o_ref[0:1, :] = r0 + b_ref[...]
    o_ref[1:2, :] = r1 + b_ref[...]


def _ada_all(cc_t, mod_w, mod_b):
    n = 6 * D_MODEL
    tn = 1536
    return pl.pallas_call(
        _ada_kernel,
        grid=(DEPTH, n // tn),
        in_specs=[
            pl.BlockSpec((D_MODEL, 2), lambda l, j: (0, 0)),
            pl.BlockSpec((None, D_MODEL, tn), lambda l, j: (l, 0, j)),
            pl.BlockSpec((None, 1, tn), lambda l, j: (l, 0, j)),
        ],
        out_specs=pl.BlockSpec((None, 2, tn), lambda l, j: (l, 0, j)),
        out_shape=jax.ShapeDtypeStruct((DEPTH, 2, n), F32),
        compiler_params=_cparams(("arbitrary", "arbitrary")),
        name="ada",
    )(cc_t, mod_w, mod_b.reshape(DEPTH, 1, n))


def _even_in_kernel(x_ref, sh_ref, sc_ref, w_ref, o_ref):
    h = _modulated(x_ref[...], sh_ref[...], sc_ref[...]).astype(BF16)
    for n in range(0, EVEN_IN, 512):
        o_ref[:, n:n + 512] = jnp.dot(h, w_ref[:, n:n + 512], preferred_element_type=F32)


def _even_in(x, sh, sc, w):
    t = x.shape[0]
    return pl.pallas_call(
        _even_in_kernel,
        grid=(t // ROW_TILE,),
        in_specs=[
            pl.BlockSpec((ROW_TILE, D_MODEL), lambda i: (i, 0)),
            _mod_spec(), _mod_spec(),
            pl.BlockSpec((D_MODEL, EVEN_IN), lambda i: (0, 0)),
        ],
        out_specs=pl.BlockSpec((ROW_TILE, EVEN_IN), lambda i: (i, 0)),
        out_shape=jax.ShapeDtypeStruct((t, EVEN_IN), F32),
        compiler_params=_cparams(("arbitrary",)),
        name="even_in",
    )(x, sh, sc, w)


def _group_scan(x, r_in_group, forward):
    rows = x.shape[0]
    for sh in (1, 2, 4, 8):
        if forward:
            rolled = pltpu.roll(x, sh, 0)
            ok = r_in_group >= sh
        else:
            rolled = pltpu.roll(x, rows - sh, 0)
            ok = r_in_group < HG_SUB - sh
        x = x + jnp.where(ok, rolled, 0.0)
    return x


def _hgrn_kernel(*refs, rev, readout):
    if readout:
        q_ref, f_ref, i_ref, lb_ref, of_ref, g_ref, gain_ref, o_ref, st_ref, acc_ref = refs
    else:
        q_ref, f_ref, i_ref, lb_ref, o_ref, st_ref, acc_ref = refs
    rows = q_ref.shape[0]
    n_groups = rows // HG_SUB

    @pl.when(pl.program_id(1) == 0)
    def _():
        st_ref[...] = jnp.zeros_like(st_ref)

    f = f_ref[...]
    lb = lb_ref[...]
    t_abs = jnp.exp(-jnp.abs(f))
    big = 1.0 / (1.0 + t_abs)
    small = t_abs * big
    log_f = jnp.log(lb + (1.0 - lb) * jnp.where(f >= 0, big, small))
    kk = (1.0 - lb) * jnp.where(f >= 0, small, big)
    qs = _silu(q_ref[...])
    v = i_ref[...]

    row = lax.broadcasted_iota(jnp.int32, (rows, HG_D), 0)
    r16 = row & (HG_SUB - 1)
    grp = row // HG_SUB
    pfx = _group_scan(log_f, r16, True)
    sfx = _group_scan(log_f, r16, False)
    cum, other = (sfx, pfx) if rev else (pfx, sfx)
    qd = (qs * jnp.exp(cum)).astype(BF16)
    kd = (kk * jnp.exp(other - log_f)).astype(BF16)

    kn = (kk * jnp.exp(-cum)).astype(BF16)
    sc = lax.dot_general(qd, kn, (((1,), (1,)), ((), ())), preferred_element_type=F32)
    ti = lax.broadcasted_iota(jnp.int32, (rows, rows), 0)
    si = lax.broadcasted_iota(jnp.int32, (rows, rows), 1)
    reach = ti & (HG_SUB - 1)
    dist, reach = (si - ti, HG_SUB - 1 - reach) if rev else (ti - si, reach)
    ok = dist.astype(jnp.uint32) <= reach.astype(jnp.uint32)
    acc_factored = jnp.dot(jnp.where(ok, sc, 0.0).astype(BF16), v.astype(BF16),
                           preferred_element_type=F32)

    v_t = v.T.astype(BF16)
    bd = jnp.concatenate(
        [jnp.where(grp == g, kd, jnp.zeros_like(kd)) for g in range(n_groups)], axis=1)
    u = jnp.dot(v_t, bd, preferred_element_type=F32)

    st = st_ref[...]
    outs = [None] * n_groups
    for g in (range(n_groups - 1, -1, -1) if rev else range(n_groups)):
        lo = g * HG_SUB
        outs[g] = lax.dot_general(qd[lo:lo + HG_SUB], st.astype(BF16),
                                  (((1,), (1,)), ((), ())), preferred_element_type=F32)
        edge = lo if rev else lo + HG_SUB - 1
        st = st * jnp.exp(cum[edge:edge + 1, :]) + u[:, g * HG_D:(g + 1) * HG_D]
    st_ref[...] = st
    o_inter = jnp.concatenate(outs, axis=0)
    acc_ref[...] = acc_factored + o_inter

    @pl.when(jnp.logical_not(jnp.min(cum) >= -HG_SAFE_DECAY))
    def _():
        ones = jnp.ones((HG_D, HG_D), BF16)
        acc = o_inter
        for d in range(HG_SUB):
            if d == 0:
                kk_s, cum_s, v_s = kk, cum, v
                p = qs * kk_s
            else:
                shift = rows - d if rev else d
                kk_s = pltpu.roll(kk, shift, 0)
                cum_s = pltpu.roll(cum, shift, 0)
                v_s = pltpu.roll(v, shift, 0)
                ok_d = (r16 < HG_SUB - d) if rev else (r16 >= d)
                p = jnp.where(ok_d, qs * kk_s * jnp.exp(cum - cum_s), 0.0)
            rsum = jnp.dot(p.astype(BF16), ones, preferred_element_type=F32)
            acc = acc + rsum * v_s
        acc_ref[...] = acc

    o = acc_ref[...]

    if readout:
        o = o + of_ref[...]
        ms = jnp.mean(o * o, axis=-1, keepdims=True)
        o = o * lax.rsqrt(ms + EPS) * gain_ref[...] * _silu(g_ref[...])
    o_ref[...] = o.astype(o_ref.dtype)


def _hgrn(y, lb, rev, o_fwd=None, gain=None):
    t = y.shape[0]
    nblk = t // ROW_TILE
    readout = o_fwd is not None
    if rev:
        blk = lambda c: jnp.where(c == 0, 0, nblk - c)
    else:
        blk = lambda c: c
    col = lambda slab: pl.BlockSpec((ROW_TILE, HG_D), lambda h, c: (blk(c), slab * HG_HEADS + h))
    in_specs = [col(0), col(2 if rev else 1), col(3),
                pl.BlockSpec((None, 1, HG_D), lambda h, c: (h, 0, 0))]
    args = [y, y, y, lb.reshape(HG_HEADS, 1, HG_D)]
    if readout:
        in_specs += [pl.BlockSpec((ROW_TILE, HG_D), lambda h, c: (blk(c), h)), col(4),
                     pl.BlockSpec((1, HG_D), lambda h, c: (0, 0))]
        args += [o_fwd, y, gain.reshape(1, HG_D)]
    return pl.pallas_call(
        functools.partial(_hgrn_kernel, rev=rev, readout=readout),
        grid=(HG_HEADS, nblk),
        in_specs=in_specs,
        out_specs=pl.BlockSpec((ROW_TILE, HG_D), lambda h, c: (blk(c), h)),
        out_shape=jax.ShapeDtypeStruct((t, HG_WIDTH), BF16 if readout else F32),
        scratch_shapes=[pltpu.VMEM((HG_D, HG_D), F32), pltpu.VMEM((ROW_TILE, HG_D), F32)],
        compiler_params=_cparams(("arbitrary", "arbitrary")),
        name="hgrn_bwd" if rev else "hgrn_fwd",
    )(*args)


def _even_out_kernel(a_ref, u_ref, v_ref, lng_ref, lnb_ref, sgw_ref, sgb_ref, w_ref, x_ref,
                     gate_ref, o_ref):
    u = _gelu(u_ref[...])
    v = _gelu(v_ref[...])
    mu = jnp.mean(v, axis=-1, keepdims=True)
    var = jnp.mean(jnp.square(v - mu), axis=-1, keepdims=True)
    vn = ((v - mu) * lax.rsqrt(var + 1e-5) * lng_ref[...] + lnb_ref[...]).astype(BF16)
    rows = u.shape[0]
    mixed_rows = []
    for c in range(rows // SG_CHUNK):
        lo = c * SG_CHUNK
        parts = []
        for g in range(SG_GROUPS):
            m = jnp.dot(sgw_ref[g], vn[lo:lo + SG_CHUNK, g * LANES:(g + 1) * LANES],
                        preferred_element_type=F32)
            parts.append(m + sgb_ref[:, g:g + 1])
        mixed_rows.append(jnp.concatenate(parts, axis=1))
    s = (u * jnp.concatenate(mixed_rows, axis=0)).astype(BF16)
    y = jnp.dot(a_ref[...], w_ref[0:HG_WIDTH, :], preferred_element_type=F32)
    y = y + jnp.dot(s, w_ref[HG_WIDTH:, :], preferred_element_type=F32)
    o_ref[...] = x_ref[...] + gate_ref[...] * y


def _even_out(a, y_proj, ln_g, ln_b, sg_w, sg_b_t, w_out, x, gate):
    t = x.shape[0]
    full = lambda shape: pl.BlockSpec(shape, lambda i: (0,) * len(shape))
    return pl.pallas_call(
        _even_out_kernel,
        grid=(t // ROW_TILE,),
        in_specs=[
            pl.BlockSpec((ROW_TILE, HG_WIDTH), lambda i: (i, 0)),
            pl.BlockSpec((ROW_TILE, SG_WIDTH), lambda i: (i, 5)),
            pl.BlockSpec((ROW_TILE, SG_WIDTH), lambda i: (i, 6)),
            full((1, SG_WIDTH)), full((1, SG_WIDTH)),
            full((SG_GROUPS, SG_CHUNK, SG_CHUNK)), full((SG_CHUNK, SG_GROUPS)),
            full((D_MODEL, D_MODEL)),
            pl.BlockSpec((ROW_TILE, D_MODEL), lambda i: (i, 0)),
            _mod_spec(),
        ],
        out_specs=pl.BlockSpec((ROW_TILE, D_MODEL), lambda i: (i, 0)),
        out_shape=jax.ShapeDtypeStruct((t, D_MODEL), F32),
        compiler_params=_cparams(("arbitrary",)),
        name="even_out",
    )(a, y_proj, y_proj, ln_g.reshape(1, -1), ln_b.reshape(1, -1), sg_w, sg_b_t, w_out, x, gate)


def _ffn_kernel(x_ref, sh_ref, sc_ref, gate_ref, wg_ref, wu_ref, wd_ref, o_ref):
    x = x_ref[...]
    h = _modulated(x, sh_ref[...], sc_ref[...]).astype(BF16)
    y = jnp.zeros(x.shape, F32)
    for n in range(0, D_FF, FFN_FF_TILE):
        g = jnp.dot(h, wg_ref[:, n:n + FFN_FF_TILE], preferred_element_type=F32)
        u = jnp.dot(h, wu_ref[:, n:n + FFN_FF_TILE], preferred_element_type=F32)
        a = (_silu(g) * u).astype(BF16)
        y = y + jnp.dot(a, wd_ref[n:n + FFN_FF_TILE, :], preferred_element_type=F32)
    o_ref[...] = x + gate_ref[...] * y


def _ffn(x, sh, sc, gate, wg, wu, wd):
    t = x.shape[0]
    full = lambda shape: pl.BlockSpec(shape, lambda i: (0,) * len(shape))
    return pl.pallas_call(
        _ffn_kernel,
        grid=(t // ROW_TILE,),
        in_specs=[
            pl.BlockSpec((ROW_TILE, D_MODEL), lambda i: (i, 0)),
            _mod_spec(), _mod_spec(), _mod_spec(),
            full((D_MODEL, D_FF)), full((D_MODEL, D_FF)), full((D_FF, D_MODEL)),
        ],
        out_specs=pl.BlockSpec((ROW_TILE, D_MODEL), lambda i: (i, 0)),
        out_shape=jax.ShapeDtypeStruct((t, D_MODEL), F32),
        compiler_params=_cparams(("arbitrary",)),
        name="ffn",
    )(x, sh, sc, gate, wg, wu, wd)


def _qkv_kernel(x_ref, sh_ref, sc_ref, w_ref, rt_ref, ct_ref, q_ref, k_ref, vt_ref):
    h = _modulated(x_ref[...], sh_ref[...], sc_ref[...]).astype(BF16)
    grid_rows = ROW_TILE // GRID_W

    def table(i):
        by_row = jnp.concatenate(
            [rt_ref[i, r] for r in range(grid_rows) for _ in range(GRID_W // 8)], axis=0)
        return by_row + jnp.concatenate([ct_ref[i]] * grid_rows, axis=0)

    cos, sa, sb = table(0), table(1), table(2)

    def rope(z):
        return z * cos + pltpu.roll(z, LANES - 16, 1) * sa + pltpu.roll(z, 16, 1) * sb

    for n in range(0, D_MODEL, 512):
        q = jnp.dot(h, w_ref[:, n:n + 512], preferred_element_type=F32)
        k = jnp.dot(h, w_ref[:, D_MODEL + n:D_MODEL + n + 512], preferred_element_type=F32)
        for m in range(0, 512, LANES):
            q_ref[:, n + m:n + m + LANES] = (
                rope(q[:, m:m + LANES]) * (DA_DH ** -0.5 * LOG2E)).astype(BF16)
            k_ref[:, n + m:n + m + LANES] = rope(k[:, m:m + LANES]).astype(BF16)
        v = jnp.dot(h, w_ref[:, 2 * D_MODEL + n:2 * D_MODEL + n + 512], preferred_element_type=F32)
        vt_ref[n:n + 512, :] = v.T.astype(BF16)


def _qkv(x, sh, sc, w, row_tab, col_tab):
    t = x.shape[0]
    grid_rows = ROW_TILE // GRID_W
    row = lambda width: pl.BlockSpec((ROW_TILE, width), lambda i: (i, 0))
    return pl.pallas_call(
        _qkv_kernel,
        grid=(t // ROW_TILE,),
        in_specs=[row(D_MODEL), _mod_spec(), _mod_spec(),
                  pl.BlockSpec((D_MODEL, 3 * D_MODEL), lambda i: (0, 0)),
                  pl.BlockSpec((3, grid_rows, 8, LANES), lambda i: (0, i, 0, 0)),
                  pl.BlockSpec((None, 3, GRID_W, LANES), lambda i: (jnp.minimum(i, 1), 0, 0, 0))],
        out_specs=[row(D_MODEL), row(D_MODEL), pl.BlockSpec((D_MODEL, ROW_TILE), lambda i: (0, i))],
        out_shape=[jax.ShapeDtypeStruct((t, D_MODEL), BF16), jax.ShapeDtypeStruct((t, D_MODEL), BF16),
                   jax.ShapeDtypeStruct((D_MODEL, t), BF16)],
        compiler_params=_cparams(("arbitrary",)),
        name="qkv",
    )(x, sh, sc, w, row_tab, col_tab)


def _attn_kernel(q_ref, k_ref, vt_ref, lam_ref, g_ref, o_ref, m_ref, l_ref, acc_ref, kmax_ref, *,
                 lam_init, n_keys):
    tq = q_ref.shape[0]
    q = q_ref[...]
    lane = lax.broadcasted_iota(jnp.int32, q.shape, 1)
    zero = jnp.zeros_like(q)
    qq = jnp.concatenate([jnp.where(lane < DA_DH, q, zero), jnp.where(lane >= DA_DH, q, zero)], axis=0)

    tk = ATT_KV_TILE
    n_chunks = n_keys // tk
    is_ctx = pl.program_id(1) == 0
    ones8 = jnp.ones((8, DA_DV), BF16)
    nt = (((1,), (1,)), ((), ()))

    @pl.when(is_ctx)
    def _():
        lane8 = lax.broadcasted_iota(jnp.int32, (8, DA_DV), 1)
        row8 = lax.broadcasted_iota(jnp.int32, (8, DA_DV), 0)
        sel = jnp.where((lane8 >= DA_DH) == (row8 == 1), 1.0, 0.0).astype(BF16)

        def body(c, best):
            kc = k_ref[pl.ds(pl.multiple_of(c * tk, tk), tk), :].astype(F32)
            n2 = lax.dot_general(sel, (kc * kc).astype(BF16), nt, preferred_element_type=F32)
            return jnp.maximum(best, n2)
        best = lax.fori_loop(0, n_chunks, body, jnp.zeros((8, tk), F32))
        kmax_ref[...] = jnp.broadcast_to(jnp.max(best, axis=1, keepdims=True), kmax_ref.shape)

    qf = qq.astype(F32)
    qn2 = lax.dot_general(ones8, (qf * qf).astype(BF16), nt, preferred_element_type=F32)[0:1]
    col = lax.broadcasted_iota(jnp.int32, (1, 2 * tq), 1)
    kn2 = jnp.where(col < tq, kmax_ref[0:1, 0:1], kmax_ref[1:2, 0:1])
    m_bound = jnp.sqrt(qn2 * kn2)

    def scores(off, size):
        k = k_ref[pl.ds(off, size), :]
        return lax.dot_general(k, qq, nt, preferred_element_type=F32)

    def fast_chunk(off, size):
        p = jnp.exp2(scores(off, size) - m_bound)
        l_ref[...] += jnp.sum(p, axis=0, keepdims=True)
        acc_ref[...] += jnp.dot(vt_ref[:, pl.ds(off, size)], p.astype(BF16), preferred_element_type=F32)

    def online_chunk(off, size):
        s = scores(off, size)
        m_old = m_ref[...]
        m_new = jnp.maximum(m_old, jnp.max(s, axis=0, keepdims=True))
        alpha = jnp.exp2(m_old - m_new)
        p = jnp.exp2(s - m_new)
        l_ref[...] = alpha * l_ref[...] + jnp.sum(p, axis=0, keepdims=True)
        acc_ref[...] = alpha * acc_ref[...] + jnp.dot(vt_ref[:, pl.ds(off, size)], p.astype(BF16),
                                                      preferred_element_type=F32)
        m_ref[...] = m_new

    def all_keys(chunk, unroll):
        @pl.when(is_ctx)
        def _():
            chunk(0, CTX_LEN)

        @pl.when(jnp.logical_not(is_ctx))
        def _():
            def body(jj, carry):
                for u in range(unroll):
                    chunk(pl.multiple_of((unroll * jj + u) * tk, tk), tk)
                return carry
            lax.fori_loop(0, n_chunks // unroll, body, 0)
            for c in range(n_chunks - n_chunks % unroll, n_chunks):
                chunk(c * tk, tk)

    l_ref[...] = jnp.zeros_like(l_ref)
    acc_ref[...] = jnp.zeros_like(acc_ref)
    all_keys(fast_chunk, ATT_UNROLL)

    @pl.when(jnp.logical_not(jnp.min(l_ref[...]) >= ATT_MIN_SUM))
    def _():
        m_ref[...] = jnp.full_like(m_ref, -jnp.inf)
        l_ref[...] = jnp.zeros_like(l_ref)
        acc_ref[...] = jnp.zeros_like(acc_ref)
        all_keys(online_chunk, 1)

    lam = lam_ref[...]
    lam_full = (jnp.exp(jnp.sum(lam[0:1] * lam[1:2], axis=-1, keepdims=True))
                - jnp.exp(jnp.sum(lam[2:3] * lam[3:4], axis=-1, keepdims=True)) + lam_init)
    o_all = acc_ref[...] / l_ref[...]
    o = (o_all[:, 0:tq] - lam_full * o_all[:, tq:2 * tq]).T
    ms = jnp.mean(o * o, axis=-1, keepdims=True)
    o = o * lax.rsqrt(ms + 1e-5) * g_ref[...] * (1.0 - lam_init)
    o_ref[...] = o.astype(o_ref.dtype)


def _attention(q, k, vt, lam, subln_g, lam_init):
    t = q.shape[0]
    assert t % ATT_KV_TILE == 0
    return pl.pallas_call(
        functools.partial(_attn_kernel, lam_init=lam_init, n_keys=t),
        grid=(DA_HEADS, t // ROW_TILE),
        in_specs=[
            pl.BlockSpec((ROW_TILE, DA_DV), lambda h, i: (i, h)),
            pl.BlockSpec((t, DA_DV), lambda h, i: (0, h)),
            pl.BlockSpec((DA_DV, t), lambda h, i: (h, 0)),
            pl.BlockSpec((4, DA_DH), lambda h, i: (0, 0)),
            pl.BlockSpec((1, DA_DV), lambda h, i: (0, 0)),
        ],
        out_specs=pl.BlockSpec((ROW_TILE, DA_DV), lambda h, i: (i, h)),
        out_shape=jax.ShapeDtypeStruct((t, D_MODEL), BF16),
        scratch_shapes=[pltpu.VMEM((1, 2 * ROW_TILE), F32), pltpu.VMEM((1, 2 * ROW_TILE), F32),
                        pltpu.VMEM((DA_DV, 2 * ROW_TILE), F32), pltpu.VMEM((8, LANES), F32)],
        compiler_params=_cparams(("arbitrary", "arbitrary")),
        name="attention",
    )(q, k, vt, lam, subln_g.reshape(1, DA_DV))


def _proj_res_kernel(a_ref, w_ref, x_ref, gate_ref, o_ref):
    y = jnp.dot(a_ref[...], w_ref[...], preferred_element_type=F32)
    o_ref[...] = x_ref[...] + gate_ref[...] * y


def _proj_res(a, w, x, gate):
    t = x.shape[0]
    row = pl.BlockSpec((ROW_TILE, D_MODEL), lambda i: (i, 0))
    return pl.pallas_call(
        _proj_res_kernel,
        grid=(t // ROW_TILE,),
        in_specs=[row, pl.BlockSpec((D_MODEL, D_MODEL), lambda i: (0, 0)), row, _mod_spec()],
        out_specs=row,
        out_shape=jax.ShapeDtypeStruct((t, D_MODEL), F32),
        compiler_params=_cparams(("arbitrary",)),
        name="proj_res",
    )(a, w, x, gate)


HALF = D_MODEL // 2


def _pack_bf16_pairs(h):
    bits = pltpu.bitcast(h.astype(BF16).astype(F32), jnp.uint32)
    return (bits[:, HALF:] & jnp.uint32(0xFFFF0000)) | (bits[:, :HALF] >> 16)


def _unpack_bf16_pairs(w):
    lo = pltpu.bitcast(w << 16, F32)
    hi = pltpu.bitcast(w & jnp.uint32(0xFFFF0000), F32)
    return jnp.concatenate([lo, hi], axis=1).astype(BF16)


def _moe_in_kernel(x_ref, sh_ref, sc_ref, wr_ref, h_ref, logit_ref):
    h = _modulated(x_ref[...], sh_ref[...], sc_ref[...])
    h_ref[...] = _pack_bf16_pairs(h)
    logit_ref[...] = jnp.dot(h, wr_ref[...], preferred_element_type=F32,
                             precision=lax.Precision.HIGHEST)


def _moe_in(x, sh, sc, w_router_padded):
    t = x.shape[0]
    row = lambda width: pl.BlockSpec((ROW_TILE, width), lambda i: (i, 0))
    return pl.pallas_call(
        _moe_in_kernel,
        grid=(t // ROW_TILE,),
        in_specs=[row(D_MODEL), _mod_spec(), _mod_spec(),
                  pl.BlockSpec((D_MODEL, LANES), lambda i: (0, 0))],
        out_specs=[row(HALF), row(LANES)],
        out_shape=[jax.ShapeDtypeStruct((t, HALF), jnp.uint32), jax.ShapeDtypeStruct((t, LANES), F32)],
        compiler_params=_cparams(("arbitrary",)),
        name="moe_in",
    )(x, sh, sc, w_router_padded)


def _dispatch_kernel(tok_ref, h_ref, o_ref):
    def body(g, carry):
        base = pl.multiple_of(g * 8, 8)
        rows = [h_ref[pl.ds(tok_ref[0, 0, base + u], 1), :] for u in range(8)]
        o_ref[pl.ds(base, 8), :] = jnp.concatenate(rows, axis=0)
        return carry
    lax.fori_loop(0, o_ref.shape[0] // 8, body, 0)


def _dispatch(h_packed, row_tok):
    n_rows = row_tok.shape[0]
    n_blocks = n_rows // MOE_TILE
    return pl.pallas_call(
        _dispatch_kernel,
        grid=(n_blocks,),
        in_specs=[
            pl.BlockSpec((1, 1, MOE_TILE), lambda b: (b, 0, 0), memory_space=pltpu.SMEM),
            pl.BlockSpec(memory_space=pltpu.VMEM),
        ],
        out_specs=pl.BlockSpec((MOE_TILE, HALF), lambda b: (b, 0)),
        out_shape=jax.ShapeDtypeStruct((n_rows, HALF), jnp.uint32),
        compiler_params=_cparams(("arbitrary",)),
        name="moe_dispatch",
    )(row_tok.reshape(n_blocks, 1, MOE_TILE), h_packed)


def _experts_kernel(blk_e_ref, n_used_ref, x_ref, wg_ref, wu_ref, wd_ref, o_ref, xb_ref):
    b = pl.program_id(0)
    f = pl.program_id(1)

    @pl.when(f == 0)
    def _():
        o_ref[...] = jnp.zeros_like(o_ref)
        xb_ref[...] = _unpack_bf16_pairs(x_ref[...])

    @pl.when(b < n_used_ref[0])
    def _():
        x = xb_ref[...]
        g = jnp.dot(x, wg_ref[...].astype(BF16), preferred_element_type=F32)
        u = jnp.dot(x, wu_ref[...].astype(BF16), preferred_element_type=F32)
        a = (_silu(g) * u).astype(BF16)
        o_ref[...] += jnp.dot(a, wd_ref[...].astype(BF16), preferred_element_type=F32)


def _experts(xg, blk_e, n_used, wg, wu, wd, j):
    n_rows = xg.shape[0]
    n_blocks = n_rows // MOE_TILE
    n_ff = E_FF // MOE_FF_TILE
    ff = lambda b, f, nu: jnp.where(b < nu[0], f, n_ff - 1)
    grid_spec = pltpu.PrefetchScalarGridSpec(
        num_scalar_prefetch=2,
        grid=(n_blocks, n_ff),
        in_specs=[
            pl.BlockSpec((MOE_TILE, HALF), lambda b, f, be, nu: (b, 0)),
            pl.BlockSpec((None, None, D_MODEL, MOE_FF_TILE),
                         lambda b, f, be, nu: (j, be[b], 0, ff(b, f, nu))),
            pl.BlockSpec((None, None, D_MODEL, MOE_FF_TILE),
                         lambda b, f, be, nu: (j, be[b], 0, ff(b, f, nu))),
            pl.BlockSpec((None, None, MOE_FF_TILE, D_MODEL),
                         lambda b, f, be, nu: (j, be[b], ff(b, f, nu), 0)),
        ],
        out_specs=pl.BlockSpec((MOE_TILE, D_MODEL), lambda b, f, be, nu: (b, 0)),
        scratch_shapes=[pltpu.VMEM((MOE_TILE, D_MODEL), BF16)],
    )
    return pl.pallas_call(
        _experts_kernel,
        grid_spec=grid_spec,
        out_shape=jax.ShapeDtypeStruct((n_rows, D_MODEL), F32),
        compiler_params=_cparams(("arbitrary", "arbitrary")),
        name="experts",
    )(blk_e, n_used, xg, wg, wu, wd)


def _combine_kernel(x_ref, gate_ref, y0_ref, y1_ref, w_ref, o_ref):
    w = w_ref[...]
    y = w[:, 0:1] * y0_ref[...] + w[:, 1:2] * y1_ref[...]
    o_ref[...] = x_ref[...] + gate_ref[...] * y


def _combine(x, gate, y0, y1, w):
    t = x.shape[0]
    row = pl.BlockSpec((ROW_TILE, D_MODEL), lambda i: (i, 0))
    return pl.pallas_call(
        _combine_kernel,
        grid=(t // ROW_TILE,),
        in_specs=[row, _mod_spec(), row, row, pl.BlockSpec((ROW_TILE, TOP_K), lambda i: (i, 0))],
        out_specs=row,
        out_shape=jax.ShapeDtypeStruct((t, D_MODEL), F32),
        compiler_params=_cparams(("arbitrary",)),
        name="moe_combine",
    )(x, gate, y0, y1, w)


def _moe(x, sh, sc, gate, w_router, wg, wu, wd, j):
    t = x.shape[0]
    wr = jnp.zeros((D_MODEL, LANES), F32).at[:, :N_EXPERTS].set(w_router)
    h, logits = _moe_in(x, sh, sc, wr)
    top_v, top_i = lax.top_k(logits[:, :N_EXPERTS], TOP_K)
    top_w = jax.nn.softmax(top_v, axis=-1)
    flat_e = top_i.reshape(-1)
    onehot = (flat_e[:, None] == jnp.arange(N_EXPERTS, dtype=flat_e.dtype)[None, :]).astype(jnp.int32)
    counts = jnp.sum(onehot, axis=0)
    rank = jnp.sum((jnp.cumsum(onehot, axis=0) - onehot) * onehot, axis=1)
    padded = ((counts + MOE_TILE - 1) // MOE_TILE) * MOE_TILE
    pad_end = jnp.cumsum(padded)
    pad_start = pad_end - padded
    dest = (pad_start[flat_e] + rank).astype(jnp.int32)
    n_blocks = (t * TOP_K + N_EXPERTS * (MOE_TILE - 1)) // MOE_TILE
    n_rows = n_blocks * MOE_TILE
    flat_tok = jnp.repeat(jnp.arange(t, dtype=jnp.int32), TOP_K)
    row_tok = jnp.zeros((n_rows,), jnp.int32).at[dest].set(flat_tok)
    blk_start = jnp.arange(n_blocks, dtype=jnp.int32) * MOE_TILE
    blk_e = jnp.minimum(jnp.sum(pad_end[None, :] <= blk_start[:, None], axis=1), N_EXPERTS - 1)
    n_used = (pad_end[-1:] // MOE_TILE).astype(jnp.int32)
    xg = _dispatch(h, row_tok)
    out = _experts(xg, blk_e.astype(jnp.int32), n_used, wg, wu, wd, j)
    pos = dest.reshape(t, TOP_K)
    y0 = jnp.take(out, pos[:, 0], axis=0)
    y1 = jnp.take(out, pos[:, 1], axis=0)
    return _combine(x, gate, y0, y1, top_w)


def _final_kernel(x_ref, g_ref, o_ref):
    x = x_ref[...]
    ms = jnp.mean(x * x, axis=-1, keepdims=True)
    o_ref[...] = x * lax.rsqrt(ms + EPS) * g_ref[...]


def _final(x, g):
    n = x.shape[0] - CTX_LEN
    return pl.pallas_call(
        _final_kernel,
        grid=(n // ROW_TILE,),
        in_specs=[pl.BlockSpec((ROW_TILE, D_MODEL), lambda i: (i + 1, 0)),
                  pl.BlockSpec((1, D_MODEL), lambda i: (0, 0))],
        out_specs=pl.BlockSpec((ROW_TILE, D_MODEL), lambda i: (i, 0)),
        out_shape=jax.ShapeDtypeStruct((n, D_MODEL), F32),
        compiler_params=_cparams(("arbitrary",)),
        name="final_norm",
    )(x, g.reshape(1, D_MODEL))


def _rope_tables(n):
    n_rows = n // GRID_W
    inv = 1.0 / (ROPE_BASE ** (jnp.arange(0, ROPE_AXIS, 2, dtype=F32) / ROPE_AXIS))
    ar = jnp.arange(n_rows, dtype=F32)[:, None] * inv
    ac = jnp.arange(GRID_W, dtype=F32)[:, None] * inv
    lane = jnp.arange(LANES)
    by_row = (lane % DA_DH) < ROPE_AXIS
    first = (lane % ROPE_AXIS) < (ROPE_AXIS // 2)
    freq = lane % (ROPE_AXIS // 2)

    def lanes(ang):
        c, s = jnp.cos(ang)[:, freq], jnp.sin(ang)[:, freq]
        return jnp.stack([c, jnp.where(first, -s, 0.0), jnp.where(first, 0.0, s)])

    ident = jnp.stack([jnp.ones((1, LANES), F32), jnp.zeros((1, LANES), F32), jnp.zeros((1, LANES), F32)])
    ctx_rows = jnp.broadcast_to(jnp.where(by_row, ident, 0.0), (3, CTX_LEN // GRID_W, LANES))
    row_tab = jnp.concatenate([ctx_rows, jnp.where(by_row, lanes(ar), 0.0)], axis=1)
    row_tab = jnp.broadcast_to(row_tab[:, :, None, :], (3, row_tab.shape[1], 8, LANES))
    ctx_cols = jnp.broadcast_to(jnp.where(by_row, 0.0, ident), (3, GRID_W, LANES))
    col_tab = jnp.stack([ctx_cols, jnp.where(by_row, 0.0, lanes(ac))])
    return row_tab, col_tab


def kernel(x, c, ctx, c_ctx, mod_w, mod_b, hg_lb, ev_w_in, hg_norm_g, sg_ln_g, sg_ln_b, sg_w, sg_b, ev_w_out, da_w_qkv, da_lam, da_subln_g, da_w_out, ffn_w_gate, ffn_w_up, ffn_w_down, moe_w_router, moe_w_gate, moe_w_up, moe_w_down, final_g):
    n = x.shape[1]
    xs = jnp.concatenate([ctx[0], x[0]], axis=0)
    cc_t = jnp.stack([c_ctx, c[0]], axis=1)
    mods = _ada_all(cc_t, mod_w, mod_b).reshape(DEPTH, 2, 6, 1, D_MODEL)
    row_tab, col_tab = _rope_tables(n)
    lb_sm = jax.nn.softmax(hg_lb.astype(F32), axis=1)
    lb_all = jnp.cumsum(lb_sm, axis=1) - lb_sm[:, :1]

    for layer in range(DEPTH):
        j = layer // 2
        sh1, sc1, g1, sh2, sc2, g2 = (mods[layer, :, i] for i in range(6))
        if layer % 2 == 0:
            y = _even_in(xs, sh1, sc1, ev_w_in[j].astype(BF16))
            o_f = _hgrn(y, lb_all[0, j], rev=False)
            a = _hgrn(y, lb_all[1, j], rev=True, o_fwd=o_f, gain=hg_norm_g[j])
            xs = _even_out(a, y, sg_ln_g[j], sg_ln_b[j], sg_w[j].astype(BF16), sg_b[j].T,
                           ev_w_out[j].astype(BF16), xs, g1)
            xs = _ffn(xs, sh2, sc2, g2, ffn_w_gate[j].astype(BF16), ffn_w_up[j].astype(BF16),
                      ffn_w_down[j].astype(BF16))
        else:
            lam_init = 0.8 - 0.6 * math.exp(-0.3 * layer)
            q, k, vt = _qkv(xs, sh1, sc1, da_w_qkv[j].astype(BF16), row_tab, col_tab)
            a = _attention(q, k, vt, da_lam[j], da_subln_g[j], lam_init)
            xs = _proj_res(a, da_w_out[j].astype(BF16), xs, g1)
            xs = _moe(xs, sh2, sc2, g2, moe_w_router[j], moe_w_gate, moe_w_up, moe_w_down, j)
    return _final(xs, final_g)[None]
```

```python
import functools
import math

import jax
import jax.numpy as jnp
from jax import lax
from jax.experimental import pallas as pl
from jax.experimental.pallas import tpu as pltpu

F32 = jnp.float32
BF16 = jnp.bfloat16

D_MODEL = 1024
DEPTH = 4
GRID_W = 64
CTX_LEN = 256
EPS = 1e-6

HG_WIDTH = 512
HG_HEADS = 4
HG_D = 128
HG_SUB = 16
HG_SAFE_DECAY = 80.0
SG_WIDTH = 512
SG_CHUNK = 128
SG_GROUPS = 4
EVEN_IN = 5 * HG_WIDTH + 2 * SG_WIDTH

DA_HEADS = 8
DA_DH = 64
DA_DV = 128
ROPE_BASE = 10000.0
ROPE_AXIS = DA_DH // 2

D_FF = 2816
N_EXPERTS = 8
TOP_K = 2
E_FF = 3584

LOG2E = 1.4426950408889634
LANES = 128
ROW_TILE = CTX_LEN
ATT_KV_TILE = 1280
ATT_UNROLL = 13
ATT_MIN_SUM = 2.0 ** -80
MOE_TILE = 1024
MOE_FF_TILE = 512
FFN_FF_TILE = 256
VMEM_LIMIT = 56 * 1024 * 1024


def _cparams(sem):
    return pltpu.CompilerParams(dimension_semantics=sem, vmem_limit_bytes=VMEM_LIMIT)


def _sigmoid(x):
    return 1.0 / (1.0 + jnp.exp(-x))


def _silu(x):
    return x * _sigmoid(x)


def _gelu(x):
    return 0.5 * x * (1.0 + lax.erf(x * (2.0 ** -0.5)))


def _modulated(x, sh, sc):
    ms = jnp.mean(x * x, axis=-1, keepdims=True)
    return x * lax.rsqrt(ms + EPS) * (1.0 + sc) + sh


def _mod_spec(width=D_MODEL):
    return pl.BlockSpec((None, 1, width), lambda i, *_: (jnp.minimum(i, 1), 0, 0))


def _ada_kernel(c_ref, w_ref, b_ref, o_ref):
    s = _silu(c_ref[...])
    w = w_ref[...]
    r0 = jnp.sum(s[:, 0:1] * w, axis=0, keepdims=True)
    r1 = jnp.sum(s[:, 1:2] * w, axis=0, keepdims=True)
    o_ref[0:1, :] = r0 + b_ref[...]
    o_ref[1:2, :] = r1 + b_ref[...]


def _ada_all(cc_t, mod_w, mod_b):
    n = 6 * D_MODEL
    tn = 1536
    return pl.pallas_call(
        _ada_kernel,
        grid=(DEPTH, n // tn),
        in_specs=[
            pl.BlockSpec((D_MODEL, 2), lambda l, j: (0, 0)),
            pl.BlockSpec((None, D_MODEL, tn), lambda l, j: (l, 0, j)),
            pl.BlockSpec((None, 1, tn), lambda l, j: (l, 0, j)),
        ],
        out_specs=pl.BlockSpec((None, 2, tn), lambda l, j: (l, 0, j)),
        out_shape=jax.ShapeDtypeStruct((DEPTH, 2, n), F32),
        compiler_params=_cparams(("arbitrary", "arbitrary")),
        name="ada",
    )(cc_t, mod_w, mod_b.reshape(DEPTH, 1, n))


def _even_in_kernel(x_ref, sh_ref, sc_ref, w_ref, o_ref):
    h = _modulated(x_ref[...], sh_ref[...], sc_ref[...]).astype(BF16)
    for n in range(0, EVEN_IN, 512):
        o_ref[:, n:n + 512] = jnp.dot(h, w_ref[:, n:n + 512], preferred_element_type=F32)


def _even_in(x, sh, sc, w):
    t = x.shape[0]
    return pl.pallas_call(
        _even_in_kernel,
        grid=(t // ROW_TILE,),
        in_specs=[
            pl.BlockSpec((ROW_TILE, D_MODEL), lambda i: (i, 0)),
            _mod_spec(), _mod_spec(),
            pl.BlockSpec((D_MODEL, EVEN_IN), lambda i: (0, 0)),
        ],
        out_specs=pl.BlockSpec((ROW_TILE, EVEN_IN), lambda i: (i, 0)),
        out_shape=jax.ShapeDtypeStruct((t, EVEN_IN), F32),
        compiler_params=_cparams(("arbitrary",)),
        name="even_in",
    )(x, sh, sc, w)


def _group_scan(x, r_in_group, forward):
    rows = x.shape[0]
    for sh in (1, 2, 4, 8):
        if forward:
            rolled = pltpu.roll(x, sh, 0)
            ok = r_in_group >= sh
        else:
            rolled = pltpu.roll(x, rows - sh, 0)
            ok = r_in_group < HG_SUB - sh
        x = x + jnp.where(ok, rolled, 0.0)
    return x


def _hgrn_kernel(*refs, rev, readout):
    if readout:
        q_ref, f_ref, i_ref, lb_ref, of_ref, g_ref, gain_ref, o_ref, st_ref, acc_ref = refs
    else:
        q_ref, f_ref, i_ref, lb_ref, o_ref, st_ref, acc_ref = refs
    rows = q_ref.shape[0]
    n_groups = rows // HG_SUB

    @pl.when(pl.program_id(1) == 0)
    def _():
        st_ref[...] = jnp.zeros_like(st_ref)

    f = f_ref[...]
    lb = lb_ref[...]
    t_abs = jnp.exp(-jnp.abs(f))
    big = 1.0 / (1.0 + t_abs)
    small = t_abs * big
    log_f = jnp.log(lb + (1.0 - lb) * jnp.where(f >= 0, big, small))
    kk = (1.0 - lb) * jnp.where(f >= 0, small, big)
    qs = _silu(q_ref[...])
    v = i_ref[...]

    row = lax.broadcasted_iota(jnp.int32, (rows, HG_D), 0)
    r16 = row & (HG_SUB - 1)
    grp = row // HG_SUB
    pfx = _group_scan(log_f, r16, True)
    sfx = _group_scan(log_f, r16, False)
    cum, other = (sfx, pfx) if rev else (pfx, sfx)
    qd = (qs * jnp.exp(cum)).astype(BF16)
    kd = (kk * jnp.exp(other - log_f)).astype(BF16)

    kn = (kk * jnp.exp(-cum)).astype(BF16)
    sc = lax.dot_general(qd, kn, (((1,), (1,)), ((), ())), preferred_element_type=F32)
    ti = lax.broadcasted_iota(jnp.int32, (rows, rows), 0)
    si = lax.broadcasted_iota(jnp.int32, (rows, rows), 1)
    reach = ti & (HG_SUB - 1)
    dist, reach = (si - ti, HG_SUB - 1 - reach) if rev else (ti - si, reach)
    ok = dist.astype(jnp.uint32) <= reach.astype(jnp.uint32)
    acc_factored = jnp.dot(jnp.where(ok, sc, 0.0).astype(BF16), v.astype(BF16),
                           preferred_element_type=F32)

    v_t = v.T.astype(BF16)
    bd = jnp.concatenate(
        [jnp.where(grp == g, kd, jnp.zeros_like(kd)) for g in range(n_groups)], axis=1)
    u = jnp.dot(v_t, bd, preferred_element_type=F32)

    st = st_ref[...]
    outs = [None] * n_groups
    for g in (range(n_groups - 1, -1, -1) if rev else range(n_groups)):
        lo = g * HG_SUB
        outs[g] = lax.dot_general(qd[lo:lo + HG_SUB], st.astype(BF16),
                                  (((1,), (1,)), ((), ())), preferred_element_type=F32)
        edge = lo if rev else lo + HG_SUB - 1
        st = st * jnp.exp(cum[edge:edge + 1, :]) + u[:, g * HG_D:(g + 1) * HG_D]
    st_ref[...] = st
    o_inter = jnp.concatenate(outs, axis=0)
    acc_ref[...] = acc_factored + o_inter

    @pl.when(jnp.logical_not(jnp.min(cum) >= -HG_SAFE_DECAY))
    def _():
        ones = jnp.ones((HG_D, HG_D), BF16)
        acc = o_inter
        for d in range(HG_SUB):
            if d == 0:
                kk_s, cum_s, v_s = kk, cum, v
                p = qs * kk_s
            else:
                shift = rows - d if rev else d
                kk_s = pltpu.roll(kk, shift, 0)
                cum_s = pltpu.roll(cum, shift, 0)
                v_s = pltpu.roll(v, shift, 0)
                ok_d = (r16 < HG_SUB - d) if rev else (r16 >= d)
                p = jnp.where(ok_d, qs * kk_s * jnp.exp(cum - cum_s), 0.0)
            rsum = jnp.dot(p.astype(BF16), ones, preferred_element_type=F32)
            acc = acc + rsum * v_s
        acc_ref[...] = acc

    o = acc_ref[...]

    if readout:
        o = o + of_ref[...]
        ms = jnp.mean(o * o, axis=-1, keepdims=True)
        o = o * lax.rsqrt(ms + EPS) * gain_ref[...] * _silu(g_ref[...])
    o_ref[...] = o.astype(o_ref.dtype)


def _hgrn(y, lb, rev, o_fwd=None, gain=None):
    t = y.shape[0]
    nblk = t // ROW_TILE
    readout = o_fwd is not None
    if rev:
        blk = lambda c: jnp.where(c == 0, 0, nblk - c)
    else:
        blk = lambda c: c
    col = lambda slab: pl.BlockSpec((ROW_TILE, HG_D), lambda h, c: (blk(c), slab * HG_HEADS + h))
    in_specs = [col(0), col(2 if rev else 1), col(3),
                pl.BlockSpec((None, 1, HG_D), lambda h, c: (h, 0, 0))]
    args = [y, y, y, lb.reshape(HG_HEADS, 1, HG_D)]
    if readout:
        in_specs += [pl.BlockSpec((ROW_TILE, HG_D), lambda h, c: (blk(c), h)), col(4),
                     pl.BlockSpec((1, HG_D), lambda h, c: (0, 0))]
        args += [o_fwd, y, gain.reshape(1, HG_D)]
    return pl.pallas_call(
        functools.partial(_hgrn_kernel, rev=rev, readout=readout),
        grid=(HG_HEADS, nblk),
        in_specs=in_specs,
        out_specs=pl.BlockSpec((ROW_TILE, HG_D), lambda h, c: (blk(c), h)),
        out_shape=jax.ShapeDtypeStruct((t, HG_WIDTH), BF16 if readout else F32),
        scratch_shapes=[pltpu.VMEM((HG_D, HG_D), F32), pltpu.VMEM((ROW_TILE, HG_D), F32)],
        compiler_params=_cparams(("arbitrary", "arbitrary")),
        name="hgrn_bwd" if rev else "hgrn_fwd",
    )(*args)


def _even_out_kernel(a_ref, u_ref, v_ref, lng_ref, lnb_ref, sgw_ref, sgb_ref, w_ref, x_ref,
                     gate_ref, o_ref):
    u = _gelu(u_ref[...])
    v = _gelu(v_ref[...])
    mu = jnp.mean(v, axis=-1, keepdims=True)
    var = jnp.mean(jnp.square(v - mu), axis=-1, keepdims=True)
    vn = ((v - mu) * lax.rsqrt(var + 1e-5) * lng_ref[...] + lnb_ref[...]).astype(BF16)
    rows = u.shape[0]
    mixed_rows = []
    for c in range(rows // SG_CHUNK):
        lo = c * SG_CHUNK
        parts = []
        for g in range(SG_GROUPS):
            m = jnp.dot(sgw_ref[g], vn[lo:lo + SG_CHUNK, g * LANES:(g + 1) * LANES],
                        preferred_element_type=F32)
            parts.append(m + sgb_ref[:, g:g + 1])
        mixed_rows.append(jnp.concatenate(parts, axis=1))
    s = (u * jnp.concatenate(mixed_rows, axis=0)).astype(BF16)
    y = jnp.dot(a_ref[...], w_ref[0:HG_WIDTH, :], preferred_element_type=F32)
    y = y + jnp.dot(s, w_ref[HG_WIDTH:, :], preferred_element_type=F32)
    o_ref[...] = x_ref[...] + gate_ref[...] * y


def _even_out(a, y_proj, ln_g, ln_b, sg_w, sg_b_t, w_out, x, gate):
    t = x.shape[0]
    full = lambda shape: pl.BlockSpec(shape, lambda i: (0,) * len(shape))
    return pl.pallas_call(
        _even_out_kernel,
        grid=(t // ROW_TILE,),
        in_specs=[
            pl.BlockSpec((ROW_TILE, HG_WIDTH), lambda i: (i, 0)),
            pl.BlockSpec((ROW_TILE, SG_WIDTH), lambda i: (i, 5)),
            pl.BlockSpec((ROW_TILE, SG_WIDTH), lambda i: (i, 6)),
            full((1, SG_WIDTH)), full((1, SG_WIDTH)),
            full((SG_GROUPS, SG_CHUNK, SG_CHUNK)), full((SG_CHUNK, SG_GROUPS)),
            full((D_MODEL, D_MODEL)),
            pl.BlockSpec((ROW_TILE, D_MODEL), lambda i: (i, 0)),
            _mod_spec(),
        ],
        out_specs=pl.BlockSpec((ROW_TILE, D_MODEL), lambda i: (i, 0)),
        out_shape=jax.ShapeDtypeStruct((t, D_MODEL), F32),
        compiler_params=_cparams(("arbitrary",)),
        name="even_out",
    )(a, y_proj, y_proj, ln_g.reshape(1, -1), ln_b.reshape(1, -1), sg_w, sg_b_t, w_out, x, gate)


def _ffn_kernel(x_ref, sh_ref, sc_ref, gate_ref, wg_ref, wu_ref, wd_ref, o_ref):
    x = x_ref[...]
    h = _modulated(x, sh_ref[...], sc_ref[...]).astype(BF16)
    y = jnp.zeros(x.shape, F32)
    for n in range(0, D_FF, FFN_FF_TILE):
        g = jnp.dot(h, wg_ref[:, n:n + FFN_FF_TILE], preferred_element_type=F32)
        u = jnp.dot(h, wu_ref[:, n:n + FFN_FF_TILE], preferred_element_type=F32)
        a = (_silu(g) * u).astype(BF16)
        y = y + jnp.dot(a, wd_ref[n:n + FFN_FF_TILE, :], preferred_element_type=F32)
    o_ref[...] = x + gate_ref[...] * y


def _ffn(x, sh, sc, gate, wg, wu, wd):
    t = x.shape[0]
    full = lambda shape: pl.BlockSpec(shape, lambda i: (0,) * len(shape))
    return pl.pallas_call(
        _ffn_kernel,
        grid=(t // ROW_TILE,),
        in_specs=[
            pl.BlockSpec((ROW_TILE, D_MODEL), lambda i: (i, 0)),
            _mod_spec(), _mod_spec(), _mod_spec(),
            full((D_MODEL, D_FF)), full((D_MODEL, D_FF)), full((D_FF, D_MODEL)),
        ],
        out_specs=pl.BlockSpec((ROW_TILE, D_MODEL), lambda i: (i, 0)),
        out_shape=jax.ShapeDtypeStruct((t, D_MODEL), F32),
        compiler_params=_cparams(("arbitrary",)),
        name="ffn",
    )(x, sh, sc, gate, wg, wu, wd)


def _qkv_kernel(x_ref, sh_ref, sc_ref, w_ref, rt_ref, ct_ref, q_ref, k_ref, vt_ref):
    h = _modulated(x_ref[...], sh_ref[...], sc_ref[...]).astype(BF16)
    grid_rows = ROW_TILE // GRID_W

    def table(i):
        by_row = jnp.concatenate(
            [rt_ref[i, r] for r in range(grid_rows) for _ in range(GRID_W // 8)], axis=0)
        return by_row + jnp.concatenate([ct_ref[i]] * grid_rows, axis=0)

    cos, sa, sb = table(0), table(1), table(2)

    def rope(z):
        return z * cos + pltpu.roll(z, LANES - 16, 1) * sa + pltpu.roll(z, 16, 1) * sb

    for n in range(0, D_MODEL, 512):
        q = jnp.dot(h, w_ref[:, n:n + 512], preferred_element_type=F32)
        k = jnp.dot(h, w_ref[:, D_MODEL + n:D_MODEL + n + 512], preferred_element_type=F32)
        for m in range(0, 512, LANES):
            q_ref[:, n + m:n + m + LANES] = (
                rope(q[:, m:m + LANES]) * (DA_DH ** -0.5 * LOG2E)).astype(BF16)
            k_ref[:, n + m:n + m + LANES] = rope(k[:, m:m + LANES]).astype(BF16)
        v = jnp.dot(h, w_ref[:, 2 * D_MODEL + n:2 * D_MODEL + n + 512], preferred_element_type=F32)
        vt_ref[n:n + 512, :] = v.T.astype(BF16)


def _qkv(x, sh, sc, w, row_tab, col_tab):
    t = x.shape[0]
    grid_rows = ROW_TILE // GRID_W
    row = lambda width: pl.BlockSpec((ROW_TILE, width), lambda i: (i, 0))
    return pl.pallas_call(
        _qkv_kernel,
        grid=(t // ROW_TILE,),
        in_specs=[row(D_MODEL), _mod_spec(), _mod_spec(),
                  pl.BlockSpec((D_MODEL, 3 * D_MODEL), lambda i: (0, 0)),
                  pl.BlockSpec((3, grid_rows, 8, LANES), lambda i: (0, i, 0, 0)),
                  pl.BlockSpec((None, 3, GRID_W, LANES), lambda i: (jnp.minimum(i, 1), 0, 0, 0))],
        out_specs=[row(D_MODEL), row(D_MODEL), pl.BlockSpec((D_MODEL, ROW_TILE), lambda i: (0, i))],
        out_shape=[jax.ShapeDtypeStruct((t, D_MODEL), BF16), jax.ShapeDtypeStruct((t, D_MODEL), BF16),
                   jax.ShapeDtypeStruct((D_MODEL, t), BF16)],
        compiler_params=_cparams(("arbitrary",)),
        name="qkv",
    )(x, sh, sc, w, row_tab, col_tab)


def _attn_kernel(q_ref, k_ref, vt_ref, lam_ref, g_ref, o_ref, m_ref, l_ref, acc_ref, kmax_ref, *,
                 lam_init, n_keys):
    tq = q_ref.shape[0]
    q = q_ref[...]
    lane = lax.broadcasted_iota(jnp.int32, q.shape, 1)
    zero = jnp.zeros_like(q)
    qq = jnp.concatenate([jnp.where(lane < DA_DH, q, zero), jnp.where(lane >= DA_DH, q, zero)], axis=0)

    tk = ATT_KV_TILE
    n_chunks = n_keys // tk
    is_ctx = pl.program_id(1) == 0
    ones8 = jnp.ones((8, DA_DV), BF16)
    nt = (((1,), (1,)), ((), ()))

    @pl.when(is_ctx)
    def _():
        lane8 = lax.broadcasted_iota(jnp.int32, (8, DA_DV), 1)
        row8 = lax.broadcasted_iota(jnp.int32, (8, DA_DV), 0)
        sel = jnp.where((lane8 >= DA_DH) == (row8 == 1), 1.0, 0.0).astype(BF16)

        def body(c, best):
            kc = k_ref[pl.ds(pl.multiple_of(c * tk, tk), tk), :].astype(F32)
            n2 = lax.dot_general(sel, (kc * kc).astype(BF16), nt, preferred_element_type=F32)
            return jnp.maximum(best, n2)
        best = lax.fori_loop(0, n_chunks, body, jnp.zeros((8, tk), F32))
        kmax_ref[...] = jnp.broadcast_to(jnp.max(best, axis=1, keepdims=True), kmax_ref.shape)

    qf = qq.astype(F32)
    qn2 = lax.dot_general(ones8, (qf * qf).astype(BF16), nt, preferred_element_type=F32)[0:1]
    col = lax.broadcasted_iota(jnp.int32, (1, 2 * tq), 1)
    kn2 = jnp.where(col < tq, kmax_ref[0:1, 0:1], kmax_ref[1:2, 0:1])
    m_bound = jnp.sqrt(qn2 * kn2)

    def scores(off, size):
        k = k_ref[pl.ds(off, size), :]
        return lax.dot_general(k, qq, nt, preferred_element_type=F32)

    def fast_chunk(off, size):
        p = jnp.exp2(scores(off, size) - m_bound)
        l_ref[...] += jnp.sum(p, axis=0, keepdims=True)
        acc_ref[...] += jnp.dot(vt_ref[:, pl.ds(off, size)], p.astype(BF16), preferred_element_type=F32)

    def online_chunk(off, size):
        s = scores(off, size)
        m_old = m_ref[...]
        m_new = jnp.maximum(m_old, jnp.max(s, axis=0, keepdims=True))
        alpha = jnp.exp2(m_old - m_new)
        p = jnp.exp2(s - m_new)
        l_ref[...] = alpha * l_ref[...] + jnp.sum(p, axis=0, keepdims=True)
        acc_ref[...] = alpha * acc_ref[...] + jnp.dot(vt_ref[:, pl.ds(off, size)], p.astype(BF16),
                                                      preferred_element_type=F32)
        m_ref[...] = m_new

    def all_keys(chunk, unroll):
        @pl.when(is_ctx)
        def _():
            chunk(0, CTX_LEN)

        @pl.when(jnp.logical_not(is_ctx))
        def _():
            def body(jj, carry):
                for u in range(unroll):
                    chunk(pl.multiple_of((unroll * jj + u) * tk, tk), tk)
                return carry
            lax.fori_loop(0, n_chunks // unroll, body, 0)
            for c in range(n_chunks - n_chunks % unroll, n_chunks):
                chunk(c * tk, tk)

    l_ref[...] = jnp.zeros_like(l_ref)
    acc_ref[...] = jnp.zeros_like(acc_ref)
    all_keys(fast_chunk, ATT_UNROLL)

    @pl.when(jnp.logical_not(jnp.min(l_ref[...]) >= ATT_MIN_SUM))
    def _():
        m_ref[...] = jnp.full_like(m_ref, -jnp.inf)
        l_ref[...] = jnp.zeros_like(l_ref)
        acc_ref[...] = jnp.zeros_like(acc_ref)
        all_keys(online_chunk, 1)

    lam = lam_ref[...]
    lam_full = (jnp.exp(jnp.sum(lam[0:1] * lam[1:2], axis=-1, keepdims=True))
                - jnp.exp(jnp.sum(lam[2:3] * lam[3:4], axis=-1, keepdims=True)) + lam_init)
    o_all = acc_ref[...] / l_ref[...]
    o = (o_all[:, 0:tq] - lam_full * o_all[:, tq:2 * tq]).T
    ms = jnp.mean(o * o, axis=-1, keepdims=True)
    o = o * lax.rsqrt(ms + 1e-5) * g_ref[...] * (1.0 - lam_init)
    o_ref[...] = o.astype(o_ref.dtype)


def _attention(q, k, vt, lam, subln_g, lam_init):
    t = q.shape[0]
    assert t % ATT_KV_TILE == 0
    return pl.pallas_call(
        functools.partial(_attn_kernel, lam_init=lam_init, n_keys=t),
        grid=(DA_HEADS, t // ROW_TILE),
        in_specs=[
            pl.BlockSpec((ROW_TILE, DA_DV), lambda h, i: (i, h)),
            pl.BlockSpec((t, DA_DV), lambda h, i: (0, h)),
            pl.BlockSpec((DA_DV, t), lambda h, i: (h, 0)),
            pl.BlockSpec((4, DA_DH), lambda h, i: (0, 0)),
            pl.BlockSpec((1, DA_DV), lambda h, i: (0, 0)),
        ],
        out_specs=pl.BlockSpec((ROW_TILE, DA_DV), lambda h, i: (i, h)),
        out_shape=jax.ShapeDtypeStruct((t, D_MODEL), BF16),
        scratch_shapes=[pltpu.VMEM((1, 2 * ROW_TILE), F32), pltpu.VMEM((1, 2 * ROW_TILE), F32),
                        pltpu.VMEM((DA_DV, 2 * ROW_TILE), F32), pltpu.VMEM((8, LANES), F32)],
        compiler_params=_cparams(("arbitrary", "arbitrary")),
        name="attention",
    )(q, k, vt, lam, subln_g.reshape(1, DA_DV))


def _proj_res_kernel(a_ref, w_ref, x_ref, gate_ref, o_ref):
    y = jnp.dot(a_ref[...], w_ref[...], preferred_element_type=F32)
    o_ref[...] = x_ref[...] + gate_ref[...] * y


def _proj_res(a, w, x, gate):
    t = x.shape[0]
    row = pl.BlockSpec((ROW_TILE, D_MODEL), lambda i: (i, 0))
    return pl.pallas_call(
        _proj_res_kernel,
        grid=(t // ROW_TILE,),
        in_specs=[row, pl.BlockSpec((D_MODEL, D_MODEL), lambda i: (0, 0)), row, _mod_spec()],
        out_specs=row,
        out_shape=jax.ShapeDtypeStruct((t, D_MODEL), F32),
        compiler_params=_cparams(("arbitrary",)),
        name="proj_res",
    )(a, w, x, gate)


HALF = D_MODEL // 2


def _pack_bf16_pairs(h):
    bits = pltpu.bitcast(h.astype(BF16).astype(F32), jnp.uint32)
    return (bits[:, HALF:] & jnp.uint32(0xFFFF0000)) | (bits[:, :HALF] >> 16)


def _unpack_bf16_pairs(w):
    lo = pltpu.bitcast(w << 16, F32)
    hi = pltpu.bitcast(w & jnp.uint32(0xFFFF0000), F32)
    return jnp.concatenate([lo, hi], axis=1).astype(BF16)


def _moe_in_kernel(x_ref, sh_ref, sc_ref, wr_ref, h_ref, info_ref, w_ref, cnt_ref, run_ref):
    @pl.when(pl.program_id(0) == 0)
    def _():
        run_ref[...] = jnp.zeros_like(run_ref)

    h = _modulated(x_ref[...], sh_ref[...], sc_ref[...])
    h_ref[...] = _pack_bf16_pairs(h)
    logits = jnp.dot(h, wr_ref[...], preferred_element_type=F32, precision=lax.Precision.HIGHEST)
    rows = logits.shape[0]
    lane = lax.broadcasted_iota(jnp.int32, logits.shape, 1)
    masked = jnp.where(lane < N_EXPERTS, logits, -jnp.inf)

    def top(l):
        m = jnp.max(l, axis=-1, keepdims=True)
        return m, jnp.min(jnp.where(l == m, lane, LANES), axis=-1, keepdims=True)

    m1, e1 = top(masked)
    m2, e2 = top(jnp.where(lane == e1, -jnp.inf, masked))
    ratio = jnp.exp(m2 - m1)
    w1 = 1.0 / (1.0 + ratio)
    w_ref[...] = jnp.where(lane == 0, w1, jnp.where(lane == 1, ratio * w1, 0.0))

    hit1 = lane == e1
    hit2 = lane == e2
    hits = jnp.where(hit1, 1.0, 0.0) + jnp.where(hit2, 1.0, 0.0)
    earlier = (lax.broadcasted_iota(jnp.int32, (rows, rows), 1)
               < lax.broadcasted_iota(jnp.int32, (rows, rows), 0))
    before = run_ref[...] + jnp.dot(jnp.where(earlier, 1.0, 0.0).astype(BF16), hits.astype(BF16),
                                    preferred_element_type=F32)
    r1 = jnp.sum(jnp.where(hit1, before, 0.0), axis=-1, keepdims=True).astype(jnp.int32)
    r2 = jnp.sum(jnp.where(hit2, before, 0.0), axis=-1, keepdims=True).astype(jnp.int32)
    info_ref[...] = jnp.where(lane == 0, e1, jnp.where(lane == 1, e2, jnp.where(
        lane == 2, r1, jnp.where(lane == 3, r2, 0))))
    run_ref[...] += jnp.sum(hits, axis=0, keepdims=True)
    cnt_ref[...] = run_ref[...]


def _moe_in(x, sh, sc, w_router_padded):
    t = x.shape[0]
    row = lambda width: pl.BlockSpec((ROW_TILE, width), lambda i: (i, 0))
    return pl.pallas_call(
        _moe_in_kernel,
        grid=(t // ROW_TILE,),
        in_specs=[row(D_MODEL), _mod_spec(), _mod_spec(),
                  pl.BlockSpec((D_MODEL, LANES), lambda i: (0, 0))],
        out_specs=[row(HALF), row(LANES), row(LANES), pl.BlockSpec((1, LANES), lambda i: (0, 0))],
        out_shape=[jax.ShapeDtypeStruct((t, HALF), jnp.uint32), jax.ShapeDtypeStruct((t, LANES), jnp.int32),
                   jax.ShapeDtypeStruct((t, LANES), F32), jax.ShapeDtypeStruct((1, LANES), F32)],
        scratch_shapes=[pltpu.VMEM((1, LANES), F32)],
        compiler_params=_cparams(("arbitrary",)),
        name="moe_in",
    )(x, sh, sc, w_router_padded)


def _dispatch_kernel(tok_ref, h_ref, o_ref):
    def body(g, carry):
        base = pl.multiple_of(g * 8, 8)
        rows = [h_ref[pl.ds(tok_ref[0, 0, base + u], 1), :] for u in range(8)]
        o_ref[pl.ds(base, 8), :] = jnp.concatenate(rows, axis=0)
        return carry
    lax.fori_loop(0, o_ref.shape[0] // 8, body, 0)


def _dispatch(h_packed, row_tok):
    n_rows = row_tok.shape[0]
    n_blocks = n_rows // MOE_TILE
    return pl.pallas_call(
        _dispatch_kernel,
        grid=(n_blocks,),
        in_specs=[
            pl.BlockSpec((1, 1, MOE_TILE), lambda b: (b, 0, 0), memory_space=pltpu.SMEM),
            pl.BlockSpec(memory_space=pltpu.VMEM),
        ],
        out_specs=pl.BlockSpec((MOE_TILE, HALF), lambda b: (b, 0)),
        out_shape=jax.ShapeDtypeStruct((n_rows, HALF), jnp.uint32),
        compiler_params=_cparams(("arbitrary",)),
        name="moe_dispatch",
    )(row_tok.reshape(n_blocks, 1, MOE_TILE), h_packed)


def _experts_kernel(blk_e_ref, n_used_ref, x_ref, wg_ref, wu_ref, wd_ref, o_ref, xb_ref):
    b = pl.program_id(0)
    f = pl.program_id(1)

    @pl.when(f == 0)
    def _():
        o_ref[...] = jnp.zeros_like(o_ref)
        xb_ref[...] = _unpack_bf16_pairs(x_ref[...])

    @pl.when(b < n_used_ref[0])
    def _():
        x = xb_ref[...]
        g = jnp.dot(x, wg_ref[...].astype(BF16), preferred_element_type=F32)
        u = jnp.dot(x, wu_ref[...].astype(BF16), preferred_element_type=F32)
        a = (_silu(g) * u).astype(BF16)
        o_ref[...] += jnp.dot(a, wd_ref[...].astype(BF16), preferred_element_type=F32)


def _experts(xg, blk_e, n_used, wg, wu, wd, j):
    n_rows = xg.shape[0]
    n_blocks = n_rows // MOE_TILE
    n_ff = E_FF // MOE_FF_TILE
    ff = lambda b, f, nu: jnp.where(b < nu[0], f, n_ff - 1)
    grid_spec = pltpu.PrefetchScalarGridSpec(
        num_scalar_prefetch=2,
        grid=(n_blocks, n_ff),
        in_specs=[
            pl.BlockSpec((MOE_TILE, HALF), lambda b, f, be, nu: (b, 0)),
            pl.BlockSpec((None, None, D_MODEL, MOE_FF_TILE),
                         lambda b, f, be, nu: (j, be[b], 0, ff(b, f, nu))),
            pl.BlockSpec((None, None, D_MODEL, MOE_FF_TILE),
                         lambda b, f, be, nu: (j, be[b], 0, ff(b, f, nu))),
            pl.BlockSpec((None, None, MOE_FF_TILE, D_MODEL),
                         lambda b, f, be, nu: (j, be[b], ff(b, f, nu), 0)),
        ],
        out_specs=pl.BlockSpec((MOE_TILE, D_MODEL), lambda b, f, be, nu: (b, 0)),
        scratch_shapes=[pltpu.VMEM((MOE_TILE, D_MODEL), BF16)],
    )
    return pl.pallas_call(
        _experts_kernel,
        grid_spec=grid_spec,
        out_shape=jax.ShapeDtypeStruct((n_rows, D_MODEL), F32),
        compiler_params=_cparams(("arbitrary", "arbitrary")),
        name="experts",
    )(blk_e, n_used, xg, wg, wu, wd)


def _combine_kernel(x_ref, gate_ref, y0_ref, y1_ref, w_ref, o_ref):
    w = w_ref[...]
    y = w[:, 0:1] * y0_ref[...] + w[:, 1:2] * y1_ref[...]
    o_ref[...] = x_ref[...] + gate_ref[...] * y


def _combine(x, gate, y0, y1, w):
    t = x.shape[0]
    row = pl.BlockSpec((ROW_TILE, D_MODEL), lambda i: (i, 0))
    return pl.pallas_call(
        _combine_kernel,
        grid=(t // ROW_TILE,),
        in_specs=[row, _mod_spec(), row, row, pl.BlockSpec((ROW_TILE, LANES), lambda i: (i, 0))],
        out_specs=row,
        out_shape=jax.ShapeDtypeStruct((t, D_MODEL), F32),
        compiler_params=_cparams(("arbitrary",)),
        name="moe_combine",
    )(x, gate, y0, y1, w)


def _moe(x, sh, sc, gate, w_router, wg, wu, wd, j):
    t = x.shape[0]
    wr = jnp.zeros((D_MODEL, LANES), F32).at[:, :N_EXPERTS].set(w_router)
    h, info, top_w, counts = _moe_in(x, sh, sc, wr)
    counts = counts[0, :N_EXPERTS].astype(jnp.int32)
    flat_e = info[:, 0:TOP_K].reshape(-1)
    rank = info[:, TOP_K:2 * TOP_K].reshape(-1)
    padded = ((counts + MOE_TILE - 1) // MOE_TILE) * MOE_TILE
    pad_end = jnp.cumsum(padded)
    pad_start = pad_end - padded
    dest = (pad_start[flat_e] + rank).astype(jnp.int32)
    n_blocks = (t * TOP_K + N_EXPERTS * (MOE_TILE - 1)) // MOE_TILE
    n_rows = n_blocks * MOE_TILE
    flat_tok = jnp.repeat(jnp.arange(t, dtype=jnp.int32), TOP_K)
    row_tok = jnp.zeros((n_rows,), jnp.int32).at[dest].set(flat_tok)
    blk_start = jnp.arange(n_blocks, dtype=jnp.int32) * MOE_TILE
    blk_e = jnp.minimum(jnp.sum(pad_end[None, :] <= blk_start[:, None], axis=1), N_EXPERTS - 1)
    n_used = (pad_end[-1:] // MOE_TILE).astype(jnp.int32)
    xg = _dispatch(h, row_tok)
    out = _experts(xg, blk_e.astype(jnp.int32), n_used, wg, wu, wd, j)
    pos = dest.reshape(t, TOP_K)
    y0 = jnp.take(out, pos[:, 0], axis=0)
    y1 = jnp.take(out, pos[:, 1], axis=0)
    return _combine(x, gate, y0, y1, top_w)


def _final_kernel(x_ref, g_ref, o_ref):
    x = x_ref[...]
    ms = jnp.mean(x * x, axis=-1, keepdims=True)
    o_ref[...] = x * lax.rsqrt(ms + EPS) * g_ref[...]


def _final(x, g):
    n = x.shape[0] - CTX_LEN
    return pl.pallas_call(
        _final_kernel,
        grid=(n // ROW_TILE,),
        in_specs=[pl.BlockSpec((ROW_TILE, D_MODEL), lambda i: (i + 1, 0)),
                  pl.BlockSpec((1, D_MODEL), lambda i: (0, 0))],
        out_specs=pl.BlockSpec((ROW_TILE, D_MODEL), lambda i: (i, 0)),
        out_shape=jax.ShapeDtypeStruct((n, D_MODEL), F32),
        compiler_params=_cparams(("arbitrary",)),
        name="final_norm",
    )(x, g.reshape(1, D_MODEL))


def _rope_tables(n):
    n_rows = n // GRID_W
    inv = 1.0 / (ROPE_BASE ** (jnp.arange(0, ROPE_AXIS, 2, dtype=F32) / ROPE_AXIS))
    ar = jnp.arange(n_rows, dtype=F32)[:, None] * inv
    ac = jnp.arange(GRID_W, dtype=F32)[:, None] * inv
    lane = jnp.arange(LANES)
    by_row = (lane % DA_DH) < ROPE_AXIS
    first = (lane % ROPE_AXIS) < (ROPE_AXIS // 2)
    freq = lane % (ROPE_AXIS // 2)

    def lanes(ang):
        c, s = jnp.cos(ang)[:, freq], jnp.sin(ang)[:, freq]
        return jnp.stack([c, jnp.where(first, -s, 0.0), jnp.where(first, 0.0, s)])

    ident = jnp.stack([jnp.ones((1, LANES), F32), jnp.zeros((1, LANES), F32), jnp.zeros((1, LANES), F32)])
    ctx_rows = jnp.broadcast_to(jnp.where(by_row, ident, 0.0), (3, CTX_LEN // GRID_W, LANES))
    row_tab = jnp.concatenate([ctx_rows, jnp.where(by_row, lanes(ar), 0.0)], axis=1)
    row_tab = jnp.broadcast_to(row_tab[:, :, None, :], (3, row_tab.shape[1], 8, LANES))
    ctx_cols = jnp.broadcast_to(jnp.where(by_row, 0.0, ident), (3, GRID_W, LANES))
    col_tab = jnp.stack([ctx_cols, jnp.where(by_row, 0.0, lanes(ac))])
    return row_tab, col_tab


def kernel(x, c, ctx, c_ctx, mod_w, mod_b, hg_lb, ev_w_in, hg_norm_g, sg_ln_g, sg_ln_b, sg_w, sg_b, ev_w_out, da_w_qkv, da_lam, da_subln_g, da_w_out, ffn_w_gate, ffn_w_up, ffn_w_down, moe_w_router, moe_w_gate, moe_w_up, moe_w_down, final_g):
    n = x.shape[1]
    xs = jnp.concatenate([ctx[0], x[0]], axis=0)
    cc_t = jnp.stack([c_ctx, c[0]], axis=1)
    mods = _ada_all(cc_t, mod_w, mod_b).reshape(DEPTH, 2, 6, 1, D_MODEL)
    row_tab, col_tab = _rope_tables(n)
    lb_sm = jax.nn.softmax(hg_lb.astype(F32), axis=1)
    lb_all = jnp.cumsum(lb_sm, axis=1) - lb_sm[:, :1]

    for layer in range(DEPTH):
        j = layer // 2
        sh1, sc1, g1, sh2, sc2, g2 = (mods[layer, :, i] for i in range(6))
        if layer % 2 == 0:
            y = _even_in(xs, sh1, sc1, ev_w_in[j].astype(BF16))
            o_f = _hgrn(y, lb_all[0, j], rev=False)
            a = _hgrn(y, lb_all[1, j], rev=True, o_fwd=o_f, gain=hg_norm_g[j])
            xs = _even_out(a, y, sg_ln_g[j], sg_ln_b[j], sg_w[j].astype(BF16), sg_b[j].T,
                           ev_w_out[j].astype(BF16), xs, g1)
            xs = _ffn(xs, sh2, sc2, g2, ffn_w_gate[j].astype(BF16), ffn_w_up[j].astype(BF16),
                      ffn_w_down[j].astype(BF16))
        else:
            lam_init = 0.8 - 0.6 * math.exp(-0.3 * layer)
            q, k, vt = _qkv(xs, sh1, sc1, da_w_qkv[j].astype(BF16), row_tab, col_tab)
            a = _attention(q, k, vt, da_lam[j], da_subln_g[j], lam_init)
            xs = _proj_res(a, da_w_out[j].astype(BF16), xs, g1)
            xs = _moe(xs, sh2, sc2, g2, moe_w_router[j], moe_w_gate, moe_w_up, moe_w_down, j)
    return _final(xs, final_g)[None]
```

```python
import functools
import math

import jax
import jax.numpy as jnp
from jax import lax
from jax.experimental import pallas as pl
from jax.experimental.pallas import tpu as pltpu

F32 = jnp.float32
BF16 = jnp.bfloat16

D_MODEL = 1024
DEPTH = 4
GRID_W = 64
CTX_LEN = 256
EPS = 1e-6

HG_WIDTH = 512
HG_HEADS = 4
HG_D = 128
HG_SUB = 16
HG_SAFE_DECAY = 80.0
SG_WIDTH = 512
SG_CHUNK = 128
SG_GROUPS = 4
EVEN_IN = 5 * HG_WIDTH + 2 * SG_WIDTH

DA_HEADS = 8
DA_DH = 64
DA_DV = 128
ROPE_BASE = 10000.0
ROPE_AXIS = DA_DH // 2

D_FF = 2816
N_EXPERTS = 8
TOP_K = 2
E_FF = 3584

LOG2E = 1.4426950408889634
LANES = 128
ROW_TILE = CTX_LEN
ATT_Q_TILE = 512
ATT_KV_TILE = 1280
ATT_UNROLL = 13
ATT_MIN_SUM = 2.0 ** -80
MOE_TILE = 1024
MOE_FF_TILE = 512
FFN_FF_TILE = 256
VMEM_LIMIT = 56 * 1024 * 1024


def _cparams(sem):
    return pltpu.CompilerParams(dimension_semantics=sem, vmem_limit_bytes=VMEM_LIMIT)


def _sigmoid(x):
    return 1.0 / (1.0 + jnp.exp(-x))


def _silu(x):
    return x * _sigmoid(x)


def _gelu(x):
    return 0.5 * x * (1.0 + lax.erf(x * (2.0 ** -0.5)))


def _modulated(x, sh, sc):
    ms = jnp.mean(x * x, axis=-1, keepdims=True)
    return x * lax.rsqrt(ms + EPS) * (1.0 + sc) + sh


def _mod_spec(width=D_MODEL):
    return pl.BlockSpec((None, 1, width), lambda i, *_: (jnp.minimum(i, 1), 0, 0))


def _ada_kernel(c_ref, w_ref, b_ref, o_ref):
    s = _silu(c_ref[...])
    w = w_ref[...]
    r0 = jnp.sum(s[:, 0:1] * w, axis=0, keepdims=True)
    r1 = jnp.sum(s[:, 1:2] * w, axis=0, keepdims=True)
    o_ref[0:1, :] = r0 + b_ref[...]
    o_ref[1:2, :] = r1 + b_ref[...]


def _ada_all(cc_t, mod_w, mod_b):
    n = 6 * D_MODEL
    tn = 1536
    return pl.pallas_call(
        _ada_kernel,
        grid=(DEPTH, n // tn),
        in_specs=[
            pl.BlockSpec((D_MODEL, 2), lambda l, j: (0, 0)),
            pl.BlockSpec((None, D_MODEL, tn), lambda l, j: (l, 0, j)),
            pl.BlockSpec((None, 1, tn), lambda l, j: (l, 0, j)),
        ],
        out_specs=pl.BlockSpec((None, 2, tn), lambda l, j: (l, 0, j)),
        out_shape=jax.ShapeDtypeStruct((DEPTH, 2, n), F32),
        compiler_params=_cparams(("arbitrary", "arbitrary")),
        name="ada",
    )(cc_t, mod_w, mod_b.reshape(DEPTH, 1, n))


def _even_in_kernel(x_ref, sh_ref, sc_ref, w_ref, o_ref):
    h = _modulated(x_ref[...], sh_ref[...], sc_ref[...]).astype(BF16)
    for n in range(0, EVEN_IN, 512):
        o_ref[:, n:n + 512] = jnp.dot(h, w_ref[:, n:n + 512], preferred_element_type=F32)


def _even_in(x, sh, sc, w):
    t = x.shape[0]
    return pl.pallas_call(
        _even_in_kernel,
        grid=(t // ROW_TILE,),
        in_specs=[
            pl.BlockSpec((ROW_TILE, D_MODEL), lambda i: (i, 0)),
            _mod_spec(), _mod_spec(),
            pl.BlockSpec((D_MODEL, EVEN_IN), lambda i: (0, 0)),
        ],
        out_specs=pl.BlockSpec((ROW_TILE, EVEN_IN), lambda i: (i, 0)),
        out_shape=jax.ShapeDtypeStruct((t, EVEN_IN), F32),
        compiler_params=_cparams(("arbitrary",)),
        name="even_in",
    )(x, sh, sc, w)


def _group_scan(x, r_in_group, forward):
    rows = x.shape[0]
    for sh in (1, 2, 4, 8):
        if forward:
            rolled = pltpu.roll(x, sh, 0)
            ok = r_in_group >= sh
        else:
            rolled = pltpu.roll(x, rows - sh, 0)
            ok = r_in_group < HG_SUB - sh
        x = x + jnp.where(ok, rolled, 0.0)
    return x


def _hgrn_kernel(*refs, rev, readout):
    if readout:
        q_ref, f_ref, i_ref, lb_ref, of_ref, g_ref, gain_ref, o_ref, st_ref, acc_ref = refs
    else:
        q_ref, f_ref, i_ref, lb_ref, o_ref, st_ref, acc_ref = refs
    rows = q_ref.shape[0]
    n_groups = rows // HG_SUB

    @pl.when(pl.program_id(1) == 0)
    def _():
        st_ref[...] = jnp.zeros_like(st_ref)

    f = f_ref[...]
    lb = lb_ref[...]
    t_abs = jnp.exp(-jnp.abs(f))
    big = 1.0 / (1.0 + t_abs)
    small = t_abs * big
    log_f = jnp.log(lb + (1.0 - lb) * jnp.where(f >= 0, big, small))
    kk = (1.0 - lb) * jnp.where(f >= 0, small, big)
    qs = _silu(q_ref[...])
    v = i_ref[...]

    row = lax.broadcasted_iota(jnp.int32, (rows, HG_D), 0)
    r16 = row & (HG_SUB - 1)
    grp = row // HG_SUB
    pfx = _group_scan(log_f, r16, True)
    sfx = _group_scan(log_f, r16, False)
    cum, other = (sfx, pfx) if rev else (pfx, sfx)
    qd = (qs * jnp.exp(cum)).astype(BF16)
    kd = (kk * jnp.exp(other - log_f)).astype(BF16)

    kn = (kk * jnp.exp(-cum)).astype(BF16)
    sc = lax.dot_general(qd, kn, (((1,), (1,)), ((), ())), preferred_element_type=F32)
    ti = lax.broadcasted_iota(jnp.int32, (rows, rows), 0)
    si = lax.broadcasted_iota(jnp.int32, (rows, rows), 1)
    reach = ti & (HG_SUB - 1)
    dist, reach = (si - ti, HG_SUB - 1 - reach) if rev else (ti - si, reach)
    ok = dist.astype(jnp.uint32) <= reach.astype(jnp.uint32)
    acc_factored = jnp.dot(jnp.where(ok, sc, 0.0).astype(BF16), v.astype(BF16),
                           preferred_element_type=F32)

    v_t = v.T.astype(BF16)
    bd = jnp.concatenate(
        [jnp.where(grp == g, kd, jnp.zeros_like(kd)) for g in range(n_groups)], axis=1)
    u = jnp.dot(v_t, bd, preferred_element_type=F32)

    st = st_ref[...]
    outs = [None] * n_groups
    for g in (range(n_groups - 1, -1, -1) if rev else range(n_groups)):
        lo = g * HG_SUB
        outs[g] = lax.dot_general(qd[lo:lo + HG_SUB], st.astype(BF16),
                                  (((1,), (1,)), ((), ())), preferred_element_type=F32)
        edge = lo if rev else lo + HG_SUB - 1
        st = st * jnp.exp(cum[edge:edge + 1, :]) + u[:, g * HG_D:(g + 1) * HG_D]
    st_ref[...] = st
    o_inter = jnp.concatenate(outs, axis=0)
    acc_ref[...] = acc_factored + o_inter

    @pl.when(jnp.logical_not(jnp.min(cum) >= -HG_SAFE_DECAY))
    def _():
        ones = jnp.ones((HG_D, HG_D), BF16)
        acc = o_inter
        for d in range(HG_SUB):
            if d == 0:
                kk_s, cum_s, v_s = kk, cum, v
                p = qs * kk_s
            else:
                shift = rows - d if rev else d
                kk_s = pltpu.roll(kk, shift, 0)
                cum_s = pltpu.roll(cum, shift, 0)
                v_s = pltpu.roll(v, shift, 0)
                ok_d = (r16 < HG_SUB - d) if rev else (r16 >= d)
                p = jnp.where(ok_d, qs * kk_s * jnp.exp(cum - cum_s), 0.0)
            rsum = jnp.dot(p.astype(BF16), ones, preferred_element_type=F32)
            acc = acc + rsum * v_s
        acc_ref[...] = acc

    o = acc_ref[...]

    if readout:
        o = o + of_ref[...]
        ms = jnp.mean(o * o, axis=-1, keepdims=True)
        o = o * lax.rsqrt(ms + EPS) * gain_ref[...] * _silu(g_ref[...])
    o_ref[...] = o.astype(o_ref.dtype)


def _hgrn(y, lb, rev, o_fwd=None, gain=None):
    t = y.shape[0]
    nblk = t // ROW_TILE
    readout = o_fwd is not None
    if rev:
        blk = lambda c: jnp.where(c == 0, 0, nblk - c)
    else:
        blk = lambda c: c
    col = lambda slab: pl.BlockSpec((ROW_TILE, HG_D), lambda h, c: (blk(c), slab * HG_HEADS + h))
    in_specs = [col(0), col(2 if rev else 1), col(3),
                pl.BlockSpec((None, 1, HG_D), lambda h, c: (h, 0, 0))]
    args = [y, y, y, lb.reshape(HG_HEADS, 1, HG_D)]
    if readout:
        in_specs += [pl.BlockSpec((ROW_TILE, HG_D), lambda h, c: (blk(c), h)), col(4),
                     pl.BlockSpec((1, HG_D), lambda h, c: (0, 0))]
        args += [o_fwd, y, gain.reshape(1, HG_D)]
    return pl.pallas_call(
        functools.partial(_hgrn_kernel, rev=rev, readout=readout),
        grid=(HG_HEADS, nblk),
        in_specs=in_specs,
        out_specs=pl.BlockSpec((ROW_TILE, HG_D), lambda h, c: (blk(c), h)),
        out_shape=jax.ShapeDtypeStruct((t, HG_WIDTH), BF16 if readout else F32),
        scratch_shapes=[pltpu.VMEM((HG_D, HG_D), F32), pltpu.VMEM((ROW_TILE, HG_D), F32)],
        compiler_params=_cparams(("arbitrary", "arbitrary")),
        name="hgrn_bwd" if rev else "hgrn_fwd",
    )(*args)


def _even_out_kernel(a_ref, u_ref, v_ref, lng_ref, lnb_ref, sgw_ref, sgb_ref, w_ref, x_ref,
                     gate_ref, o_ref):
    u = _gelu(u_ref[...])
    v = _gelu(v_ref[...])
    mu = jnp.mean(v, axis=-1, keepdims=True)
    var = jnp.mean(jnp.square(v - mu), axis=-1, keepdims=True)
    vn = ((v - mu) * lax.rsqrt(var + 1e-5) * lng_ref[...] + lnb_ref[...]).astype(BF16)
    rows = u.shape[0]
    mixed_rows = []
    for c in range(rows // SG_CHUNK):
        lo = c * SG_CHUNK
        parts = []
        for g in range(SG_GROUPS):
            m = jnp.dot(sgw_ref[g], vn[lo:lo + SG_CHUNK, g * LANES:(g + 1) * LANES],
                        preferred_element_type=F32)
            parts.append(m + sgb_ref[:, g:g + 1])
        mixed_rows.append(jnp.concatenate(parts, axis=1))
    s = (u * jnp.concatenate(mixed_rows, axis=0)).astype(BF16)
    y = jnp.dot(a_ref[...], w_ref[0:HG_WIDTH, :], preferred_element_type=F32)
    y = y + jnp.dot(s, w_ref[HG_WIDTH:, :], preferred_element_type=F32)
    o_ref[...] = x_ref[...] + gate_ref[...] * y


def _even_out(a, y_proj, ln_g, ln_b, sg_w, sg_b_t, w_out, x, gate):
    t = x.shape[0]
    full = lambda shape: pl.BlockSpec(shape, lambda i: (0,) * len(shape))
    return pl.pallas_call(
        _even_out_kernel,
        grid=(t // ROW_TILE,),
        in_specs=[
            pl.BlockSpec((ROW_TILE, HG_WIDTH), lambda i: (i, 0)),
            pl.BlockSpec((ROW_TILE, SG_WIDTH), lambda i: (i, 5)),
            pl.BlockSpec((ROW_TILE, SG_WIDTH), lambda i: (i, 6)),
            full((1, SG_WIDTH)), full((1, SG_WIDTH)),
            full((SG_GROUPS, SG_CHUNK, SG_CHUNK)), full((SG_CHUNK, SG_GROUPS)),
            full((D_MODEL, D_MODEL)),
            pl.BlockSpec((ROW_TILE, D_MODEL), lambda i: (i, 0)),
            _mod_spec(),
        ],
        out_specs=pl.BlockSpec((ROW_TILE, D_MODEL), lambda i: (i, 0)),
        out_shape=jax.ShapeDtypeStruct((t, D_MODEL), F32),
        compiler_params=_cparams(("arbitrary",)),
        name="even_out",
    )(a, y_proj, y_proj, ln_g.reshape(1, -1), ln_b.reshape(1, -1), sg_w, sg_b_t, w_out, x, gate)


def _ffn_kernel(x_ref, sh_ref, sc_ref, gate_ref, wg_ref, wu_ref, wd_ref, o_ref):
    x = x_ref[...]
    h = _modulated(x, sh_ref[...], sc_ref[...]).astype(BF16)
    y = jnp.zeros(x.shape, F32)
    for n in range(0, D_FF, FFN_FF_TILE):
        g = jnp.dot(h, wg_ref[:, n:n + FFN_FF_TILE], preferred_element_type=F32)
        u = jnp.dot(h, wu_ref[:, n:n + FFN_FF_TILE], preferred_element_type=F32)
        a = (_silu(g) * u).astype(BF16)
        y = y + jnp.dot(a, wd_ref[n:n + FFN_FF_TILE, :], preferred_element_type=F32)
    o_ref[...] = x + gate_ref[...] * y


def _ffn(x, sh, sc, gate, wg, wu, wd):
    t = x.shape[0]
    full = lambda shape: pl.BlockSpec(shape, lambda i: (0,) * len(shape))
    return pl.pallas_call(
        _ffn_kernel,
        grid=(t // ROW_TILE,),
        in_specs=[
            pl.BlockSpec((ROW_TILE, D_MODEL), lambda i: (i, 0)),
            _mod_spec(), _mod_spec(), _mod_spec(),
            full((D_MODEL, D_FF)), full((D_MODEL, D_FF)), full((D_FF, D_MODEL)),
        ],
        out_specs=pl.BlockSpec((ROW_TILE, D_MODEL), lambda i: (i, 0)),
        out_shape=jax.ShapeDtypeStruct((t, D_MODEL), F32),
        compiler_params=_cparams(("arbitrary",)),
        name="ffn",
    )(x, sh, sc, gate, wg, wu, wd)


def _qkv_kernel(x_ref, sh_ref, sc_ref, w_ref, rt_ref, ct_ref, q_ref, k_ref, vt_ref):
    h = _modulated(x_ref[...], sh_ref[...], sc_ref[...]).astype(BF16)
    grid_rows = ROW_TILE // GRID_W

    def table(i):
        by_row = jnp.concatenate(
            [rt_ref[i, r] for r in range(grid_rows) for _ in range(GRID_W // 8)], axis=0)
        return by_row + jnp.concatenate([ct_ref[i]] * grid_rows, axis=0)

    cos, sa, sb = table(0), table(1), table(2)

    def rope(z):
        return z * cos + pltpu.roll(z, LANES - 16, 1) * sa + pltpu.roll(z, 16, 1) * sb

    for n in range(0, D_MODEL, 512):
        q = jnp.dot(h, w_ref[:, n:n + 512], preferred_element_type=F32)
        k = jnp.dot(h, w_ref[:, D_MODEL + n:D_MODEL + n + 512], preferred_element_type=F32)
        for m in range(0, 512, LANES):
            q_ref[:, n + m:n + m + LANES] = (
                rope(q[:, m:m + LANES]) * (DA_DH ** -0.5 * LOG2E)).astype(BF16)
            k_ref[:, n + m:n + m + LANES] = rope(k[:, m:m + LANES]).astype(BF16)
        v = jnp.dot(h, w_ref[:, 2 * D_MODEL + n:2 * D_MODEL + n + 512], preferred_element_type=F32)
        vt_ref[n:n + 512, :] = v.T.astype(BF16)


def _qkv(x, sh, sc, w, row_tab, col_tab):
    t = x.shape[0]
    grid_rows = ROW_TILE // GRID_W
    row = lambda width: pl.BlockSpec((ROW_TILE, width), lambda i: (i, 0))
    return pl.pallas_call(
        _qkv_kernel,
        grid=(t // ROW_TILE,),
        in_specs=[row(D_MODEL), _mod_spec(), _mod_spec(),
                  pl.BlockSpec((D_MODEL, 3 * D_MODEL), lambda i: (0, 0)),
                  pl.BlockSpec((3, grid_rows, 8, LANES), lambda i: (0, i, 0, 0)),
                  pl.BlockSpec((None, 3, GRID_W, LANES), lambda i: (jnp.minimum(i, 1), 0, 0, 0))],
        out_specs=[row(D_MODEL), row(D_MODEL), pl.BlockSpec((D_MODEL, ROW_TILE), lambda i: (0, i))],
        out_shape=[jax.ShapeDtypeStruct((t, D_MODEL), BF16), jax.ShapeDtypeStruct((t, D_MODEL), BF16),
                   jax.ShapeDtypeStruct((D_MODEL, t), BF16)],
        compiler_params=_cparams(("arbitrary",)),
        name="qkv",
    )(x, sh, sc, w, row_tab, col_tab)


def _attn_kernel(q_ref, k_ref, vt_ref, lam_ref, g_ref, o_ref, m_ref, l_ref, acc_ref, kmax_ref, *,
                 lam_init, n_keys):
    tq = q_ref.shape[0]
    q = q_ref[...]
    lane = lax.broadcasted_iota(jnp.int32, q.shape, 1)
    zero = jnp.zeros_like(q)
    qq = jnp.concatenate([jnp.where(lane < DA_DH, q, zero), jnp.where(lane >= DA_DH, q, zero)], axis=0)

    tk = min(ATT_KV_TILE, n_keys)
    n_chunks = n_keys // tk
    ones8 = jnp.ones((8, DA_DV), BF16)
    nt = (((1,), (1,)), ((), ()))

    @pl.when(pl.program_id(1) == 0)
    def _():
        lane8 = lax.broadcasted_iota(jnp.int32, (8, DA_DV), 1)
        row8 = lax.broadcasted_iota(jnp.int32, (8, DA_DV), 0)
        sel = jnp.where((lane8 >= DA_DH) == (row8 == 1), 1.0, 0.0).astype(BF16)

        def body(c, best):
            kc = k_ref[pl.ds(pl.multiple_of(c * tk, tk), tk), :].astype(F32)
            n2 = lax.dot_general(sel, (kc * kc).astype(BF16), nt, preferred_element_type=F32)
            return jnp.maximum(best, n2)
        best = lax.fori_loop(0, n_chunks, body, jnp.zeros((8, tk), F32))
        kmax_ref[...] = jnp.broadcast_to(jnp.max(best, axis=1, keepdims=True), kmax_ref.shape)

    qf = qq.astype(F32)
    qn2 = lax.dot_general(ones8, (qf * qf).astype(BF16), nt, preferred_element_type=F32)[0:1]
    col = lax.broadcasted_iota(jnp.int32, (1, 2 * tq), 1)
    kn2 = jnp.where(col < tq, kmax_ref[0:1, 0:1], kmax_ref[1:2, 0:1])
    m_bound = jnp.sqrt(qn2 * kn2)

    def scores(off, size):
        k = k_ref[pl.ds(off, size), :]
        return lax.dot_general(k, qq, nt, preferred_element_type=F32)

    def fast_chunk(off, size):
        p = jnp.exp2(scores(off, size) - m_bound)
        l_ref[...] += jnp.sum(p, axis=0, keepdims=True)
        acc_ref[...] += jnp.dot(vt_ref[:, pl.ds(off, size)], p.astype(BF16), preferred_element_type=F32)

    def online_chunk(off, size):
        s = scores(off, size)
        m_old = m_ref[...]
        m_new = jnp.maximum(m_old, jnp.max(s, axis=0, keepdims=True))
        alpha = jnp.exp2(m_old - m_new)
        p = jnp.exp2(s - m_new)
        l_ref[...] = alpha * l_ref[...] + jnp.sum(p, axis=0, keepdims=True)
        acc_ref[...] = alpha * acc_ref[...] + jnp.dot(vt_ref[:, pl.ds(off, size)], p.astype(BF16),
                                                      preferred_element_type=F32)
        m_ref[...] = m_new

    def all_keys(chunk, unroll):
        def body(jj, carry):
            for u in range(unroll):
                chunk(pl.multiple_of((unroll * jj + u) * tk, tk), tk)
            return carry
        if n_chunks // unroll > 0:
            lax.fori_loop(0, n_chunks // unroll, body, 0)
        for c in range(n_chunks - n_chunks % unroll, n_chunks):
            chunk(c * tk, tk)

    l_ref[...] = jnp.zeros_like(l_ref)
    acc_ref[...] = jnp.zeros_like(acc_ref)
    all_keys(fast_chunk, ATT_UNROLL)

    @pl.when(jnp.logical_not(jnp.min(l_ref[...]) >= ATT_MIN_SUM))
    def _():
        m_ref[...] = jnp.full_like(m_ref, -jnp.inf)
        l_ref[...] = jnp.zeros_like(l_ref)
        acc_ref[...] = jnp.zeros_like(acc_ref)
        all_keys(online_chunk, 1)

    lam = lam_ref[...]
    lam_full = (jnp.exp(jnp.sum(lam[0:1] * lam[1:2], axis=-1, keepdims=True))
                - jnp.exp(jnp.sum(lam[2:3] * lam[3:4], axis=-1, keepdims=True)) + lam_init)
    o_all = acc_ref[...] / l_ref[...]
    o = (o_all[:, 0:tq] - lam_full * o_all[:, tq:2 * tq]).T
    ms = jnp.mean(o * o, axis=-1, keepdims=True)
    o = o * lax.rsqrt(ms + 1e-5) * g_ref[...] * (1.0 - lam_init)
    o_ref[...] = o.astype(o_ref.dtype)


def _attention(q, k, vt, lam, subln_g, lam_init, tq):
    n_q, n_keys = q.shape[0], k.shape[0]
    assert n_q % tq == 0 and n_keys % min(ATT_KV_TILE, n_keys) == 0
    stat = pltpu.VMEM((1, 2 * tq), F32)
    return pl.pallas_call(
        functools.partial(_attn_kernel, lam_init=lam_init, n_keys=n_keys),
        grid=(DA_HEADS, n_q // tq),
        in_specs=[
            pl.BlockSpec((tq, DA_DV), lambda h, i: (i, h)),
            pl.BlockSpec((n_keys, DA_DV), lambda h, i: (0, h)),
            pl.BlockSpec((DA_DV, n_keys), lambda h, i: (h, 0)),
            pl.BlockSpec((4, DA_DH), lambda h, i: (0, 0)),
            pl.BlockSpec((1, DA_DV), lambda h, i: (0, 0)),
        ],
        out_specs=pl.BlockSpec((tq, DA_DV), lambda h, i: (i, h)),
        out_shape=jax.ShapeDtypeStruct((n_q, D_MODEL), BF16),
        scratch_shapes=[stat, stat, pltpu.VMEM((DA_DV, 2 * tq), F32), pltpu.VMEM((8, LANES), F32)],
        compiler_params=_cparams(("arbitrary", "arbitrary")),
        name="attention",
    )(q, k, vt, lam, subln_g.reshape(1, DA_DV))


def _proj_res_kernel(a_ref, w_ref, x_ref, gate_ref, o_ref):
    y = jnp.dot(a_ref[...], w_ref[...], preferred_element_type=F32)
    o_ref[...] = x_ref[...] + gate_ref[...] * y


def _proj_res(a, w, x, gate):
    t = x.shape[0]
    row = pl.BlockSpec((ROW_TILE, D_MODEL), lambda i: (i, 0))
    return pl.pallas_call(
        _proj_res_kernel,
        grid=(t // ROW_TILE,),
        in_specs=[row, pl.BlockSpec((D_MODEL, D_MODEL), lambda i: (0, 0)), row, _mod_spec()],
        out_specs=row,
        out_shape=jax.ShapeDtypeStruct((t, D_MODEL), F32),
        compiler_params=_cparams(("arbitrary",)),
        name="proj_res",
    )(a, w, x, gate)


HALF = D_MODEL // 2


def _pack_bf16_pairs(h):
    bits = pltpu.bitcast(h.astype(BF16).astype(F32), jnp.uint32)
    return (bits[:, HALF:] & jnp.uint32(0xFFFF0000)) | (bits[:, :HALF] >> 16)


def _unpack_bf16_pairs(w):
    lo = pltpu.bitcast(w << 16, F32)
    hi = pltpu.bitcast(w & jnp.uint32(0xFFFF0000), F32)
    return jnp.concatenate([lo, hi], axis=1).astype(BF16)


def _moe_in_kernel(x_ref, sh_ref, sc_ref, wr_ref, h_ref, info_ref, w_ref, cnt_ref, run_ref):
    @pl.when(pl.program_id(0) == 0)
    def _():
        run_ref[...] = jnp.zeros_like(run_ref)

    h = _modulated(x_ref[...], sh_ref[...], sc_ref[...])
    h_ref[...] = _pack_bf16_pairs(h)
    logits = jnp.dot(h, wr_ref[...], preferred_element_type=F32, precision=lax.Precision.HIGHEST)
    rows = logits.shape[0]
    lane = lax.broadcasted_iota(jnp.int32, logits.shape, 1)
    masked = jnp.where(lane < N_EXPERTS, logits, -jnp.inf)

    def top(l):
        m = jnp.max(l, axis=-1, keepdims=True)
        return m, jnp.min(jnp.where(l == m, lane, LANES), axis=-1, keepdims=True)

    m1, e1 = top(masked)
    m2, e2 = top(jnp.where(lane == e1, -jnp.inf, masked))
    ratio = jnp.exp(m2 - m1)
    w1 = 1.0 / (1.0 + ratio)
    w_ref[...] = jnp.where(lane == 0, w1, jnp.where(lane == 1, ratio * w1, 0.0))

    hit1 = lane == e1
    hit2 = lane == e2
    hits = jnp.where(hit1, 1.0, 0.0) + jnp.where(hit2, 1.0, 0.0)
    earlier = (lax.broadcasted_iota(jnp.int32, (rows, rows), 1)
               < lax.broadcasted_iota(jnp.int32, (rows, rows), 0))
    before = run_ref[...] + jnp.dot(jnp.where(earlier, 1.0, 0.0).astype(BF16), hits.astype(BF16),
                                    preferred_element_type=F32)
    r1 = jnp.sum(jnp.where(hit1, before, 0.0), axis=-1, keepdims=True).astype(jnp.int32)
    r2 = jnp.sum(jnp.where(hit2, before, 0.0), axis=-1, keepdims=True).astype(jnp.int32)
    info_ref[...] = jnp.where(lane == 0, e1, jnp.where(lane == 1, e2, jnp.where(
        lane == 2, r1, jnp.where(lane == 3, r2, 0))))
    run_ref[...] += jnp.sum(hits, axis=0, keepdims=True)
    cnt_ref[...] = run_ref[...]


def _moe_in(x, sh, sc, w_router_padded):
    t = x.shape[0]
    row = lambda width: pl.BlockSpec((ROW_TILE, width), lambda i: (i, 0))
    return pl.pallas_call(
        _moe_in_kernel,
        grid=(t // ROW_TILE,),
        in_specs=[row(D_MODEL), _mod_spec(), _mod_spec(),
                  pl.BlockSpec((D_MODEL, LANES), lambda i: (0, 0))],
        out_specs=[row(HALF), row(LANES), row(LANES), pl.BlockSpec((1, LANES), lambda i: (0, 0))],
        out_shape=[jax.ShapeDtypeStruct((t, HALF), jnp.uint32), jax.ShapeDtypeStruct((t, LANES), jnp.int32),
                   jax.ShapeDtypeStruct((t, LANES), F32), jax.ShapeDtypeStruct((1, LANES), F32)],
        scratch_shapes=[pltpu.VMEM((1, LANES), F32)],
        compiler_params=_cparams(("arbitrary",)),
        name="moe_in",
    )(x, sh, sc, w_router_padded)


def _dispatch_kernel(tok_ref, h_ref, o_ref):
    def body(g, carry):
        base = pl.multiple_of(g * 8, 8)
        rows = [h_ref[pl.ds(tok_ref[0, 0, base + u], 1), :] for u in range(8)]
        o_ref[pl.ds(base, 8), :] = jnp.concatenate(rows, axis=0)
        return carry
    lax.fori_loop(0, o_ref.shape[0] // 8, body, 0)


def _dispatch(h_packed, row_tok):
    n_rows = row_tok.shape[0]
    n_blocks = n_rows // MOE_TILE
    return pl.pallas_call(
        _dispatch_kernel,
        grid=(n_blocks,),
        in_specs=[
            pl.BlockSpec((1, 1, MOE_TILE), lambda b: (b, 0, 0), memory_space=pltpu.SMEM),
            pl.BlockSpec(memory_space=pltpu.VMEM),
        ],
        out_specs=pl.BlockSpec((MOE_TILE, HALF), lambda b: (b, 0)),
        out_shape=jax.ShapeDtypeStruct((n_rows, HALF), jnp.uint32),
        compiler_params=_cparams(("arbitrary",)),
        name="moe_dispatch",
    )(row_tok.reshape(n_blocks, 1, MOE_TILE), h_packed)


def _experts_kernel(blk_e_ref, n_used_ref, x_ref, wg_ref, wu_ref, wd_ref, o_ref, xb_ref):
    b = pl.program_id(0)
    f = pl.program_id(1)

    @pl.when(f == 0)
    def _():
        o_ref[...] = jnp.zeros_like(o_ref)
        xb_ref[...] = _unpack_bf16_pairs(x_ref[...])

    @pl.when(b < n_used_ref[0])
    def _():
        x = xb_ref[...]
        g = jnp.dot(x, wg_ref[...].astype(BF16), preferred_element_type=F32)
        u = jnp.dot(x, wu_ref[...].astype(BF16), preferred_element_type=F32)
        a = (_silu(g) * u).astype(BF16)
        o_ref[...] += jnp.dot(a, wd_ref[...].astype(BF16), preferred_element_type=F32)


def _experts(xg, blk_e, n_used, wg, wu, wd, j):
    n_rows = xg.shape[0]
    n_blocks = n_rows // MOE_TILE
    n_ff = E_FF // MOE_FF_TILE
    ff = lambda b, f, nu: jnp.where(b < nu[0], f, n_ff - 1)
    grid_spec = pltpu.PrefetchScalarGridSpec(
        num_scalar_prefetch=2,
        grid=(n_blocks, n_ff),
        in_specs=[
            pl.BlockSpec((MOE_TILE, HALF), lambda b, f, be, nu: (b, 0)),
            pl.BlockSpec((None, None, D_MODEL, MOE_FF_TILE),
                         lambda b, f, be, nu: (j, be[b], 0, ff(b, f, nu))),
            pl.BlockSpec((None, None, D_MODEL, MOE_FF_TILE),
                         lambda b, f, be, nu: (j, be[b], 0, ff(b, f, nu))),
            pl.BlockSpec((None, None, MOE_FF_TILE, D_MODEL),
                         lambda b, f, be, nu: (j, be[b], ff(b, f, nu), 0)),
        ],
        out_specs=pl.BlockSpec((MOE_TILE, D_MODEL), lambda b, f, be, nu: (b, 0)),
        scratch_shapes=[pltpu.VMEM((MOE_TILE, D_MODEL), BF16)],
    )
    return pl.pallas_call(
        _experts_kernel,
        grid_spec=grid_spec,
        out_shape=jax.ShapeDtypeStruct((n_rows, D_MODEL), F32),
        compiler_params=_cparams(("arbitrary", "arbitrary")),
        name="experts",
    )(blk_e, n_used, xg, wg, wu, wd)


def _combine_kernel(x_ref, gate_ref, y0_ref, y1_ref, w_ref, o_ref):
    w = w_ref[...]
    y = w[:, 0:1] * y0_ref[...] + w[:, 1:2] * y1_ref[...]
    o_ref[...] = x_ref[...] + gate_ref[...] * y


def _combine(x, gate, y0, y1, w):
    t = x.shape[0]
    row = pl.BlockSpec((ROW_TILE, D_MODEL), lambda i: (i, 0))
    return pl.pallas_call(
        _combine_kernel,
        grid=(t // ROW_TILE,),
        in_specs=[row, _mod_spec(), row, row, pl.BlockSpec((ROW_TILE, LANES), lambda i: (i, 0))],
        out_specs=row,
        out_shape=jax.ShapeDtypeStruct((t, D_MODEL), F32),
        compiler_params=_cparams(("arbitrary",)),
        name="moe_combine",
    )(x, gate, y0, y1, w)


def _moe(x, sh, sc, gate, w_router, wg, wu, wd, j):
    t = x.shape[0]
    wr = jnp.zeros((D_MODEL, LANES), F32).at[:, :N_EXPERTS].set(w_router)
    h, info, top_w, counts = _moe_in(x, sh, sc, wr)
    counts = counts[0, :N_EXPERTS].astype(jnp.int32)
    flat_e = info[:, 0:TOP_K].reshape(-1)
    rank = info[:, TOP_K:2 * TOP_K].reshape(-1)
    padded = ((counts + MOE_TILE - 1) // MOE_TILE) * MOE_TILE
    pad_end = jnp.cumsum(padded)
    pad_start = pad_end - padded
    dest = (pad_start[flat_e] + rank).astype(jnp.int32)
    n_blocks = (t * TOP_K + N_EXPERTS * (MOE_TILE - 1)) // MOE_TILE
    n_rows = n_blocks * MOE_TILE
    flat_tok = jnp.repeat(jnp.arange(t, dtype=jnp.int32), TOP_K)
    row_tok = jnp.zeros((n_rows,), jnp.int32).at[dest].set(
        flat_tok, mode="promise_in_bounds", unique_indices=True)
    blk_start = jnp.arange(n_blocks, dtype=jnp.int32) * MOE_TILE
    blk_e = jnp.minimum(jnp.sum(pad_end[None, :] <= blk_start[:, None], axis=1), N_EXPERTS - 1)
    n_used = (pad_end[-1:] // MOE_TILE).astype(jnp.int32)
    xg = _dispatch(h, row_tok)
    out = _experts(xg, blk_e.astype(jnp.int32), n_used, wg, wu, wd, j)
    pos = dest.reshape(t, TOP_K)
    y0 = out.at[pos[:, 0]].get(mode="promise_in_bounds")
    y1 = out.at[pos[:, 1]].get(mode="promise_in_bounds")
    return _combine(x, gate, y0, y1, top_w)


def _final_kernel(x_ref, g_ref, o_ref):
    x = x_ref[...]
    ms = jnp.mean(x * x, axis=-1, keepdims=True)
    o_ref[...] = x * lax.rsqrt(ms + EPS) * g_ref[...]


def _final(x, g):
    n = x.shape[0] - CTX_LEN
    return pl.pallas_call(
        _final_kernel,
        grid=(n // ROW_TILE,),
        in_specs=[pl.BlockSpec((ROW_TILE, D_MODEL), lambda i: (i + 1, 0)),
                  pl.BlockSpec((1, D_MODEL), lambda i: (0, 0))],
        out_specs=pl.BlockSpec((ROW_TILE, D_MODEL), lambda i: (i, 0)),
        out_shape=jax.ShapeDtypeStruct((n, D_MODEL), F32),
        compiler_params=_cparams(("arbitrary",)),
        name="final_norm",
    )(x, g.reshape(1, D_MODEL))


def _rope_tables(n):
    n_rows = n // GRID_W
    inv = 1.0 / (ROPE_BASE ** (jnp.arange(0, ROPE_AXIS, 2, dtype=F32) / ROPE_AXIS))
    ar = jnp.arange(n_rows, dtype=F32)[:, None] * inv
    ac = jnp.arange(GRID_W, dtype=F32)[:, None] * inv
    lane = jnp.arange(LANES)
    by_row = (lane % DA_DH) < ROPE_AXIS
    first = (lane % ROPE_AXIS) < (ROPE_AXIS // 2)
    freq = lane % (ROPE_AXIS // 2)

    def lanes(ang):
        c, s = jnp.cos(ang)[:, freq], jnp.sin(ang)[:, freq]
        return jnp.stack([c, jnp.where(first, -s, 0.0), jnp.where(first, 0.0, s)])

    ident = jnp.stack([jnp.ones((1, LANES), F32), jnp.zeros((1, LANES), F32), jnp.zeros((1, LANES), F32)])
    ctx_rows = jnp.broadcast_to(jnp.where(by_row, ident, 0.0), (3, CTX_LEN // GRID_W, LANES))
    row_tab = jnp.concatenate([ctx_rows, jnp.where(by_row, lanes(ar), 0.0)], axis=1)
    row_tab = jnp.broadcast_to(row_tab[:, :, None, :], (3, row_tab.shape[1], 8, LANES))
    ctx_cols = jnp.broadcast_to(jnp.where(by_row, 0.0, ident), (3, GRID_W, LANES))
    col_tab = jnp.stack([ctx_cols, jnp.where(by_row, 0.0, lanes(ac))])
    return row_tab, col_tab


def kernel(x, c, ctx, c_ctx, mod_w, mod_b, hg_lb, ev_w_in, hg_norm_g, sg_ln_g, sg_ln_b, sg_w, sg_b, ev_w_out, da_w_qkv, da_lam, da_subln_g, da_w_out, ffn_w_gate, ffn_w_up, ffn_w_down, moe_w_router, moe_w_gate, moe_w_up, moe_w_down, final_g):
    n = x.shape[1]
    xs = jnp.concatenate([ctx[0], x[0]], axis=0)
    cc_t = jnp.stack([c_ctx, c[0]], axis=1)
    mods = _ada_all(cc_t, mod_w, mod_b).reshape(DEPTH, 2, 6, 1, D_MODEL)
    row_tab, col_tab = _rope_tables(n)
    lb_sm = jax.nn.softmax(hg_lb.astype(F32), axis=1)
    lb_all = jnp.cumsum(lb_sm, axis=1) - lb_sm[:, :1]

    for layer in range(DEPTH):
        j = layer // 2
        sh1, sc1, g1, sh2, sc2, g2 = (mods[layer, :, i] for i in range(6))
        if layer % 2 == 0:
            y = _even_in(xs, sh1, sc1, ev_w_in[j].astype(BF16))
            o_f = _hgrn(y, lb_all[0, j], rev=False)
            a = _hgrn(y, lb_all[1, j], rev=True, o_fwd=o_f, gain=hg_norm_g[j])
            xs = _even_out(a, y, sg_ln_g[j], sg_ln_b[j], sg_w[j].astype(BF16), sg_b[j].T,
                           ev_w_out[j].astype(BF16), xs, g1)
            xs = _ffn(xs, sh2, sc2, g2, ffn_w_gate[j].astype(BF16), ffn_w_up[j].astype(BF16),
                      ffn_w_down[j].astype(BF16))
        else:
            lam_init = 0.8 - 0.6 * math.exp(-0.3 * layer)
            q, k, vt = _qkv(xs, sh1, sc1, da_w_qkv[j].astype(BF16), row_tab, col_tab)
            a_ctx = _attention(q[:CTX_LEN], k[:CTX_LEN], vt[:, :CTX_LEN], da_lam[j], da_subln_g[j],
                               lam_init, CTX_LEN)
            a_lat = _attention(q[CTX_LEN:], k, vt, da_lam[j], da_subln_g[j], lam_init, ATT_Q_TILE)
            a = jnp.concatenate([a_ctx, a_lat], axis=0)
            xs = _proj_res(a, da_w_out[j].astype(BF16), xs, g1)
            xs = _moe(xs, sh2, sc2, g2, moe_w_router[j], moe_w_gate, moe_w_up, moe_w_down, j)
    return _final(xs, final_g)[None]
```

```python
import functools
import math

import jax
import jax.numpy as jnp
from jax import lax
from jax.experimental import pallas as pl
from jax.experimental.pallas import tpu as pltpu

F32 = jnp.float32
BF16 = jnp.bfloat16

D_MODEL = 1024
DEPTH = 4
GRID_W = 64
CTX_LEN = 256
EPS = 1e-6

HG_WIDTH = 512
HG_HEADS = 4
HG_D = 128
HG_SUB = 16
HG_SAFE_DECAY = 80.0
SG_WIDTH = 512
SG_CHUNK = 128
SG_GROUPS = 4
EVEN_IN = 5 * HG_WIDTH + 2 * SG_WIDTH

DA_HEADS = 8
DA_DH = 64
DA_DV = 128
ROPE_BASE = 10000.0
ROPE_AXIS = DA_DH // 2

D_FF = 2816
N_EXPERTS = 8
TOP_K = 2
E_FF = 3584

LOG2E = 1.4426950408889634
LANES = 128
ROW_TILE = CTX_LEN
ATT_Q_TILE = 512
ATT_KV_TILE = 1280
ATT_UNROLL = 13
ATT_MIN_SUM = 2.0 ** -80
MOE_TILE = 1024
MOE_FF_TILE = 512
FFN_FF_TILE = 256
VMEM_LIMIT = 56 * 1024 * 1024


def _cparams(sem):
    return pltpu.CompilerParams(dimension_semantics=sem, vmem_limit_bytes=VMEM_LIMIT)


def _sigmoid(x):
    return 1.0 / (1.0 + jnp.exp(-x))


def _silu(x):
    return x * _sigmoid(x)


def _gelu(x):
    return 0.5 * x * (1.0 + lax.erf(x * (2.0 ** -0.5)))


def _modulated(x, sh, sc):
    ms = jnp.mean(x * x, axis=-1, keepdims=True)
    return x * lax.rsqrt(ms + EPS) * (1.0 + sc) + sh


def _mod_spec(width=D_MODEL):
    return pl.BlockSpec((None, 1, width), lambda i, *_: (jnp.minimum(i, 1), 0, 0))


def _ada_kernel(c_ref, w_ref, b_ref, o_ref):
    s = _silu(c_ref[...])
    w = w_ref[...]
    r0 = jnp.sum(s[:, 0:1] * w, axis=0, keepdims=True)
    r1 = jnp.sum(s[:, 1:2] * w, axis=0, keepdims=True)
    o_ref[0:1, :] = r0 + b_ref[...]
    o_ref[1:2, :] = r1 + b_ref[...]


def _ada_all(cc_t, mod_w, mod_b):
    n = 6 * D_MODEL
    tn = 1536
    return pl.pallas_call(
        _ada_kernel,
        grid=(DEPTH, n // tn),
        in_specs=[
            pl.BlockSpec((D_MODEL, 2), lambda l, j: (0, 0)),
            pl.BlockSpec((None, D_MODEL, tn), lambda l, j: (l, 0, j)),
            pl.BlockSpec((None, 1, tn), lambda l, j: (l, 0, j)),
        ],
        out_specs=pl.BlockSpec((None, 2, tn), lambda l, j: (l, 0, j)),
        out_shape=jax.ShapeDtypeStruct((DEPTH, 2, n), F32),
        compiler_params=_cparams(("arbitrary", "arbitrary")),
        name="ada",
    )(cc_t, mod_w, mod_b.reshape(DEPTH, 1, n))


def _even_in_kernel(x_ref, sh_ref, sc_ref, w_ref, o_ref):
    h = _modulated(x_ref[...], sh_ref[...], sc_ref[...]).astype(BF16)
    for n in range(0, EVEN_IN, 512):
        o_ref[:, n:n + 512] = jnp.dot(h, w_ref[:, n:n + 512], preferred_element_type=F32)


def _even_in(x, sh, sc, w):
    t = x.shape[0]
    return pl.pallas_call(
        _even_in_kernel,
        grid=(t // ROW_TILE,),
        in_specs=[
            pl.BlockSpec((ROW_TILE, D_MODEL), lambda i: (i, 0)),
            _mod_spec(), _mod_spec(),
            pl.BlockSpec((D_MODEL, EVEN_IN), lambda i: (0, 0)),
        ],
        out_specs=pl.BlockSpec((ROW_TILE, EVEN_IN), lambda i: (i, 0)),
        out_shape=jax.ShapeDtypeStruct((t, EVEN_IN), F32),
        compiler_params=_cparams(("arbitrary",)),
        name="even_in",
    )(x, sh, sc, w)


def _group_scan(x, r_in_group, forward):
    rows = x.shape[0]
    for sh in (1, 2, 4, 8):
        if forward:
            rolled = pltpu.roll(x, sh, 0)
            ok = r_in_group >= sh
        else:
            rolled = pltpu.roll(x, rows - sh, 0)
            ok = r_in_group < HG_SUB - sh
        x = x + jnp.where(ok, rolled, 0.0)
    return x


def _hgrn_kernel(*refs, rev, readout):
    if readout:
        q_ref, f_ref, i_ref, lb_ref, of_ref, g_ref, gain_ref, o_ref, st_ref, acc_ref = refs
    else:
        q_ref, f_ref, i_ref, lb_ref, o_ref, st_ref, acc_ref = refs
    rows = q_ref.shape[0]
    n_groups = rows // HG_SUB

    @pl.when(pl.program_id(1) == 0)
    def _():
        st_ref[...] = jnp.zeros_like(st_ref)

    f = f_ref[...]
    lb = lb_ref[...]
    t_abs = jnp.exp(-jnp.abs(f))
    big = 1.0 / (1.0 + t_abs)
    small = t_abs * big
    log_f = jnp.log(lb + (1.0 - lb) * jnp.where(f >= 0, big, small))
    kk = (1.0 - lb) * jnp.where(f >= 0, small, big)
    qs = _silu(q_ref[...])
    v = i_ref[...]

    row = lax.broadcasted_iota(jnp.int32, (rows, HG_D), 0)
    r16 = row & (HG_SUB - 1)
    grp = row // HG_SUB
    pfx = _group_scan(log_f, r16, True)
    sfx = _group_scan(log_f, r16, False)
    cum, other = (sfx, pfx) if rev else (pfx, sfx)
    qd = (qs * jnp.exp(cum)).astype(BF16)
    kd = (kk * jnp.exp(other - log_f)).astype(BF16)

    kn = (kk * jnp.exp(-cum)).astype(BF16)
    sc = lax.dot_general(qd, kn, (((1,), (1,)), ((), ())), preferred_element_type=F32)
    ti = lax.broadcasted_iota(jnp.int32, (rows, rows), 0)
    si = lax.broadcasted_iota(jnp.int32, (rows, rows), 1)
    reach = ti & (HG_SUB - 1)
    dist, reach = (si - ti, HG_SUB - 1 - reach) if rev else (ti - si, reach)
    ok = dist.astype(jnp.uint32) <= reach.astype(jnp.uint32)
    acc_factored = jnp.dot(jnp.where(ok, sc, 0.0).astype(BF16), v.astype(BF16),
                           preferred_element_type=F32)

    v_t = v.T.astype(BF16)
    bd = jnp.concatenate(
        [jnp.where(grp == g, kd, jnp.zeros_like(kd)) for g in range(n_groups)], axis=1)
    u = jnp.dot(v_t, bd, preferred_element_type=F32)

    st = st_ref[...]
    outs = [None] * n_groups
    for g in (range(n_groups - 1, -1, -1) if rev else range(n_groups)):
        lo = g * HG_SUB
        outs[g] = lax.dot_general(qd[lo:lo + HG_SUB], st.astype(BF16),
                                  (((1,), (1,)), ((), ())), preferred_element_type=F32)
        edge = lo if rev else lo + HG_SUB - 1
        st = st * jnp.exp(cum[edge:edge + 1, :]) + u[:, g * HG_D:(g + 1) * HG_D]
    st_ref[...] = st
    o_inter = jnp.concatenate(outs, axis=0)
    acc_ref[...] = acc_factored + o_inter

    @pl.when(jnp.logical_not(jnp.min(cum) >= -HG_SAFE_DECAY))
    def _():
        ones = jnp.ones((HG_D, HG_D), BF16)
        acc = o_inter
        for d in range(HG_SUB):
            if d == 0:
                kk_s, cum_s, v_s = kk, cum, v
                p = qs * kk_s
            else:
                shift = rows - d if rev else d
                kk_s = pltpu.roll(kk, shift, 0)
                cum_s = pltpu.roll(cum, shift, 0)
                v_s = pltpu.roll(v, shift, 0)
                ok_d = (r16 < HG_SUB - d) if rev else (r16 >= d)
                p = jnp.where(ok_d, qs * kk_s * jnp.exp(cum - cum_s), 0.0)
            rsum = jnp.dot(p.astype(BF16), ones, preferred_element_type=F32)
            acc = acc + rsum * v_s
        acc_ref[...] = acc

    o = acc_ref[...]

    if readout:
        o = o + of_ref[...]
        ms = jnp.mean(o * o, axis=-1, keepdims=True)
        o = o * lax.rsqrt(ms + EPS) * gain_ref[...] * _silu(g_ref[...])
    o_ref[...] = o.astype(o_ref.dtype)


def _hgrn(y, lb, rev, o_fwd=None, gain=None):
    t = y.shape[0]
    nblk = t // ROW_TILE
    readout = o_fwd is not None
    if rev:
        blk = lambda c: jnp.where(c == 0, 0, nblk - c)
    else:
        blk = lambda c: c
    col = lambda slab: pl.BlockSpec((ROW_TILE, HG_D), lambda h, c: (blk(c), slab * HG_HEADS + h))
    in_specs = [col(0), col(2 if rev else 1), col(3),
                pl.BlockSpec((None, 1, HG_D), lambda h, c: (h, 0, 0))]
    args = [y, y, y, lb.reshape(HG_HEADS, 1, HG_D)]
    if readout:
        in_specs += [pl.BlockSpec((ROW_TILE, HG_D), lambda h, c: (blk(c), h)), col(4),
                     pl.BlockSpec((1, HG_D), lambda h, c: (0, 0))]
        args += [o_fwd, y, gain.reshape(1, HG_D)]
    return pl.pallas_call(
        functools.partial(_hgrn_kernel, rev=rev, readout=readout),
        grid=(HG_HEADS, nblk),
        in_specs=in_specs,
        out_specs=pl.BlockSpec((ROW_TILE, HG_D), lambda h, c: (blk(c), h)),
        out_shape=jax.ShapeDtypeStruct((t, HG_WIDTH), BF16 if readout else F32),
        scratch_shapes=[pltpu.VMEM((HG_D, HG_D), F32), pltpu.VMEM((ROW_TILE, HG_D), F32)],
        compiler_params=_cparams(("arbitrary", "arbitrary")),
        name="hgrn_bwd" if rev else "hgrn_fwd",
    )(*args)


def _even_tail_kernel(a_ref, u_ref, v_ref, lng_ref, lnb_ref, sgw_ref, sgb_ref, w_ref, x_ref,
                      gate1_ref, sh_ref, sc_ref, gate2_ref, wg_ref, wu_ref, wd_ref, o_ref):
    u = _gelu(u_ref[...])
    v = _gelu(v_ref[...])
    mu = jnp.mean(v, axis=-1, keepdims=True)
    var = jnp.mean(jnp.square(v - mu), axis=-1, keepdims=True)
    vn = ((v - mu) * lax.rsqrt(var + 1e-5) * lng_ref[...] + lnb_ref[...]).astype(BF16)
    rows = u.shape[0]
    mixed_rows = []
    for c in range(rows // SG_CHUNK):
        lo = c * SG_CHUNK
        parts = []
        for g in range(SG_GROUPS):
            m = jnp.dot(sgw_ref[g], vn[lo:lo + SG_CHUNK, g * LANES:(g + 1) * LANES],
                        preferred_element_type=F32)
            parts.append(m + sgb_ref[:, g:g + 1])
        mixed_rows.append(jnp.concatenate(parts, axis=1))
    s = (u * jnp.concatenate(mixed_rows, axis=0)).astype(BF16)
    y = jnp.dot(jnp.concatenate([a_ref[...], s], axis=1), w_ref[...], preferred_element_type=F32)
    x = x_ref[...] + gate1_ref[...] * y

    h = _modulated(x, sh_ref[...], sc_ref[...]).astype(BF16)
    parts = []
    for n in range(0, D_FF, FFN_FF_TILE):
        g = jnp.dot(h, wg_ref[:, n:n + FFN_FF_TILE], preferred_element_type=F32)
        up = jnp.dot(h, wu_ref[:, n:n + FFN_FF_TILE], preferred_element_type=F32)
        parts.append((_silu(g) * up).astype(BF16))
    y = jnp.dot(jnp.concatenate(parts, axis=1), wd_ref[...], preferred_element_type=F32)
    o_ref[...] = x + gate2_ref[...] * y


def _even_tail(a, y_proj, ln_g, ln_b, sg_w, sg_b_t, w_out, x, gate1, sh, sc, gate2, wg, wu, wd):
    t = x.shape[0]
    full = lambda shape: pl.BlockSpec(shape, lambda i: (0,) * len(shape))
    return pl.pallas_call(
        _even_tail_kernel,
        grid=(t // ROW_TILE,),
        in_specs=[
            pl.BlockSpec((ROW_TILE, HG_WIDTH), lambda i: (i, 0)),
            pl.BlockSpec((ROW_TILE, SG_WIDTH), lambda i: (i, 5)),
            pl.BlockSpec((ROW_TILE, SG_WIDTH), lambda i: (i, 6)),
            full((1, SG_WIDTH)), full((1, SG_WIDTH)),
            full((SG_GROUPS, SG_CHUNK, SG_CHUNK)), full((SG_CHUNK, SG_GROUPS)),
            full((D_MODEL, D_MODEL)),
            pl.BlockSpec((ROW_TILE, D_MODEL), lambda i: (i, 0)),
            _mod_spec(), _mod_spec(), _mod_spec(), _mod_spec(),
            full((D_MODEL, D_FF)), full((D_MODEL, D_FF)), full((D_FF, D_MODEL)),
        ],
        out_specs=pl.BlockSpec((ROW_TILE, D_MODEL), lambda i: (i, 0)),
        out_shape=jax.ShapeDtypeStruct((t, D_MODEL), F32),
        compiler_params=_cparams(("arbitrary",)),
        name="even_tail",
    )(a, y_proj, y_proj, ln_g.reshape(1, -1), ln_b.reshape(1, -1), sg_w, sg_b_t, w_out, x, gate1,
      sh, sc, gate2, wg, wu, wd)


def _qkv_kernel(x_ref, sh_ref, sc_ref, w_ref, rt_ref, ct_ref, q_ref, k_ref, vt_ref):
    h = _modulated(x_ref[...], sh_ref[...], sc_ref[...]).astype(BF16)
    grid_rows = ROW_TILE // GRID_W

    def table(i):
        by_row = jnp.concatenate(
            [rt_ref[i, r] for r in range(grid_rows) for _ in range(GRID_W // 8)], axis=0)
        return by_row + jnp.concatenate([ct_ref[i]] * grid_rows, axis=0)

    cos, sa, sb = table(0), table(1), table(2)

    def rope(z):
        return z * cos + pltpu.roll(z, LANES - 16, 1) * sa + pltpu.roll(z, 16, 1) * sb

    for n in range(0, D_MODEL, 512):
        q = jnp.dot(h, w_ref[:, n:n + 512], preferred_element_type=F32)
        k = jnp.dot(h, w_ref[:, D_MODEL + n:D_MODEL + n + 512], preferred_element_type=F32)
        for m in range(0, 512, LANES):
            q_ref[:, n + m:n + m + LANES] = (
                rope(q[:, m:m + LANES]) * (DA_DH ** -0.5 * LOG2E)).astype(BF16)
            k_ref[:, n + m:n + m + LANES] = rope(k[:, m:m + LANES]).astype(BF16)
        v = jnp.dot(h, w_ref[:, 2 * D_MODEL + n:2 * D_MODEL + n + 512], preferred_element_type=F32)
        vt_ref[n:n + 512, :] = v.T.astype(BF16)


def _qkv(x, sh, sc, w, row_tab, col_tab):
    t = x.shape[0]
    grid_rows = ROW_TILE // GRID_W
    row = lambda width: pl.BlockSpec((ROW_TILE, width), lambda i: (i, 0))
    return pl.pallas_call(
        _qkv_kernel,
        grid=(t // ROW_TILE,),
        in_specs=[row(D_MODEL), _mod_spec(), _mod_spec(),
                  pl.BlockSpec((D_MODEL, 3 * D_MODEL), lambda i: (0, 0)),
                  pl.BlockSpec((3, grid_rows, 8, LANES), lambda i: (0, i, 0, 0)),
                  pl.BlockSpec((None, 3, GRID_W, LANES), lambda i: (jnp.minimum(i, 1), 0, 0, 0))],
        out_specs=[row(D_MODEL), row(D_MODEL), pl.BlockSpec((D_MODEL, ROW_TILE), lambda i: (0, i))],
        out_shape=[jax.ShapeDtypeStruct((t, D_MODEL), BF16), jax.ShapeDtypeStruct((t, D_MODEL), BF16),
                   jax.ShapeDtypeStruct((D_MODEL, t), BF16)],
        compiler_params=_cparams(("arbitrary",)),
        name="qkv",
    )(x, sh, sc, w, row_tab, col_tab)


def _attn_kernel(q_ref, k_ref, vt_ref, lam_ref, g_ref, o_ref, m_ref, l_ref, acc_ref, kmax_ref, *,
                 lam_init, n_keys):
    tq = q_ref.shape[0]
    q = q_ref[...]
    lane = lax.broadcasted_iota(jnp.int32, q.shape, 1)
    zero = jnp.zeros_like(q)
    qq = jnp.concatenate([jnp.where(lane < DA_DH, q, zero), jnp.where(lane >= DA_DH, q, zero)], axis=0)

    tk = min(ATT_KV_TILE, n_keys)
    n_chunks = n_keys // tk
    ones8 = jnp.ones((8, DA_DV), BF16)
    nt = (((1,), (1,)), ((), ()))

    @pl.when(pl.program_id(1) == 0)
    def _():
        lane8 = lax.broadcasted_iota(jnp.int32, (8, DA_DV), 1)
        row8 = lax.broadcasted_iota(jnp.int32, (8, DA_DV), 0)
        sel = jnp.where((lane8 >= DA_DH) == (row8 == 1), 1.0, 0.0).astype(BF16)

        def body(c, best):
            kc = k_ref[pl.ds(pl.multiple_of(c * tk, tk), tk), :].astype(F32)
            n2 = lax.dot_general(sel, (kc * kc).astype(BF16), nt, preferred_element_type=F32)
            return jnp.maximum(best, n2)
        best = lax.fori_loop(0, n_chunks, body, jnp.zeros((8, tk), F32))
        kmax_ref[...] = jnp.broadcast_to(jnp.max(best, axis=1, keepdims=True), kmax_ref.shape)

    qf = qq.astype(F32)
    qn2 = lax.dot_general(ones8, (qf * qf).astype(BF16), nt, preferred_element_type=F32)[0:1]
    col = lax.broadcasted_iota(jnp.int32, (1, 2 * tq), 1)
    kn2 = jnp.where(col < tq, kmax_ref[0:1, 0:1], kmax_ref[1:2, 0:1])
    m_bound = jnp.sqrt(qn2 * kn2)

    def scores(off, size):
        k = k_ref[pl.ds(off, size), :]
        return lax.dot_general(k, qq, nt, preferred_element_type=F32)

    def fast_chunk(off, size):
        p = jnp.exp2(scores(off, size) - m_bound)
        l_ref[...] += jnp.sum(p, axis=0, keepdims=True)
        acc_ref[...] += jnp.dot(vt_ref[:, pl.ds(off, size)], p.astype(BF16), preferred_element_type=F32)

    def online_chunk(off, size):
        s = scores(off, size)
        m_old = m_ref[...]
        m_new = jnp.maximum(m_old, jnp.max(s, axis=0, keepdims=True))
        alpha = jnp.exp2(m_old - m_new)
        p = jnp.exp2(s - m_new)
        l_ref[...] = alpha * l_ref[...] + jnp.sum(p, axis=0, keepdims=True)
        acc_ref[...] = alpha * acc_ref[...] + jnp.dot(vt_ref[:, pl.ds(off, size)], p.astype(BF16),
                                                      preferred_element_type=F32)
        m_ref[...] = m_new

    def all_keys(chunk, unroll):
        def body(jj, carry):
            for u in range(unroll):
                chunk(pl.multiple_of((unroll * jj + u) * tk, tk), tk)
            return carry
        if n_chunks // unroll > 0:
            lax.fori_loop(0, n_chunks // unroll, body, 0)
        for c in range(n_chunks - n_chunks % unroll, n_chunks):
            chunk(c * tk, tk)

    l_ref[...] = jnp.zeros_like(l_ref)
    acc_ref[...] = jnp.zeros_like(acc_ref)
    all_keys(fast_chunk, ATT_UNROLL)

    @pl.when(jnp.logical_not(jnp.min(l_ref[...]) >= ATT_MIN_SUM))
    def _():
        m_ref[...] = jnp.full_like(m_ref, -jnp.inf)
        l_ref[...] = jnp.zeros_like(l_ref)
        acc_ref[...] = jnp.zeros_like(acc_ref)
        all_keys(online_chunk, 1)

    lam = lam_ref[...]
    lam_full = (jnp.exp(jnp.sum(lam[0:1] * lam[1:2], axis=-1, keepdims=True))
                - jnp.exp(jnp.sum(lam[2:3] * lam[3:4], axis=-1, keepdims=True)) + lam_init)
    o_all = acc_ref[...] / l_ref[...]
    o = (o_all[:, 0:tq] - lam_full * o_all[:, tq:2 * tq]).T
    ms = jnp.mean(o * o, axis=-1, keepdims=True)
    o = o * lax.rsqrt(ms + 1e-5) * g_ref[...] * (1.0 - lam_init)
    o_ref[...] = o.astype(o_ref.dtype)


def _attention(q, k, vt, lam, subln_g, lam_init, tq):
    n_q, n_keys = q.shape[0], k.shape[0]
    assert n_q % tq == 0 and n_keys % min(ATT_KV_TILE, n_keys) == 0
    stat = pltpu.VMEM((1, 2 * tq), F32)
    return pl.pallas_call(
        functools.partial(_attn_kernel, lam_init=lam_init, n_keys=n_keys),
        grid=(DA_HEADS, n_q // tq),
        in_specs=[
            pl.BlockSpec((tq, DA_DV), lambda h, i: (i, h)),
            pl.BlockSpec((n_keys, DA_DV), lambda h, i: (0, h)),
            pl.BlockSpec((DA_DV, n_keys), lambda h, i: (h, 0)),
            pl.BlockSpec((4, DA_DH), lambda h, i: (0, 0)),
            pl.BlockSpec((1, DA_DV), lambda h, i: (0, 0)),
        ],
        out_specs=pl.BlockSpec((tq, DA_DV), lambda h, i: (i, h)),
        out_shape=jax.ShapeDtypeStruct((n_q, D_MODEL), BF16),
        scratch_shapes=[stat, stat, pltpu.VMEM((DA_DV, 2 * tq), F32), pltpu.VMEM((8, LANES), F32)],
        compiler_params=_cparams(("arbitrary", "arbitrary")),
        name="attention",
    )(q, k, vt, lam, subln_g.reshape(1, DA_DV))


def _proj_res_kernel(a_ref, w_ref, x_ref, gate_ref, o_ref):
    y = jnp.dot(a_ref[...], w_ref[...], preferred_element_type=F32)
    o_ref[...] = x_ref[...] + gate_ref[...] * y


def _proj_res(a, w, x, gate):
    t = x.shape[0]
    row = pl.BlockSpec((ROW_TILE, D_MODEL), lambda i: (i, 0))
    return pl.pallas_call(
        _proj_res_kernel,
        grid=(t // ROW_TILE,),
        in_specs=[row, pl.BlockSpec((D_MODEL, D_MODEL), lambda i: (0, 0)), row, _mod_spec()],
        out_specs=row,
        out_shape=jax.ShapeDtypeStruct((t, D_MODEL), F32),
        compiler_params=_cparams(("arbitrary",)),
        name="proj_res",
    )(a, w, x, gate)


HALF = D_MODEL // 2


def _pack_bf16_pairs(h):
    bits = pltpu.bitcast(h.astype(BF16).astype(F32), jnp.uint32)
    return (bits[:, HALF:] & jnp.uint32(0xFFFF0000)) | (bits[:, :HALF] >> 16)


def _unpack_bf16_pairs(w):
    lo = pltpu.bitcast(w << 16, F32)
    hi = pltpu.bitcast(w & jnp.uint32(0xFFFF0000), F32)
    return jnp.concatenate([lo, hi], axis=1).astype(BF16)


def _moe_in_kernel(x_ref, sh_ref, sc_ref, wr_ref, h_ref, info_ref, w_ref, cnt_ref, run_ref):
    @pl.when(pl.program_id(0) == 0)
    def _():
        run_ref[...] = jnp.zeros_like(run_ref)

    h = _modulated(x_ref[...], sh_ref[...], sc_ref[...])
    h_ref[...] = _pack_bf16_pairs(h)
    logits = jnp.dot(h, wr_ref[...], preferred_element_type=F32, precision=lax.Precision.HIGHEST)
    rows = logits.shape[0]
    lane = lax.broadcasted_iota(jnp.int32, logits.shape, 1)
    masked = jnp.where(lane < N_EXPERTS, logits, -jnp.inf)

    def top(l):
        m = jnp.max(l, axis=-1, keepdims=True)
        return m, jnp.min(jnp.where(l == m, lane, LANES), axis=-1, keepdims=True)

    m1, e1 = top(masked)
    m2, e2 = top(jnp.where(lane == e1, -jnp.inf, masked))
    ratio = jnp.exp(m2 - m1)
    w1 = 1.0 / (1.0 + ratio)
    w_ref[...] = jnp.where(lane == 0, w1, jnp.where(lane == 1, ratio * w1, 0.0))

    hit1 = lane == e1
    hit2 = lane == e2
    hits = jnp.where(hit1, 1.0, 0.0) + jnp.where(hit2, 1.0, 0.0)
    earlier = (lax.broadcasted_iota(jnp.int32, (rows, rows), 1)
               < lax.broadcasted_iota(jnp.int32, (rows, rows), 0))
    before = run_ref[...] + jnp.dot(jnp.where(earlier, 1.0, 0.0).astype(BF16), hits.astype(BF16),
                                    preferred_element_type=F32)
    r1 = jnp.sum(jnp.where(hit1, before, 0.0), axis=-1, keepdims=True).astype(jnp.int32)
    r2 = jnp.sum(jnp.where(hit2, before, 0.0), axis=-1, keepdims=True).astype(jnp.int32)
    info_ref[...] = jnp.where(lane == 0, e1, jnp.where(lane == 1, e2, jnp.where(
        lane == 2, r1, jnp.where(lane == 3, r2, 0))))
    run_ref[...] += jnp.sum(hits, axis=0, keepdims=True)
    cnt_ref[...] = run_ref[...]


def _moe_in(x, sh, sc, w_router_padded):
    t = x.shape[0]
    row = lambda width: pl.BlockSpec((ROW_TILE, width), lambda i: (i, 0))
    return pl.pallas_call(
        _moe_in_kernel,
        grid=(t // ROW_TILE,),
        in_specs=[row(D_MODEL), _mod_spec(), _mod_spec(),
                  pl.BlockSpec((D_MODEL, LANES), lambda i: (0, 0))],
        out_specs=[row(HALF), row(LANES), row(LANES), pl.BlockSpec((1, LANES), lambda i: (0, 0))],
        out_shape=[jax.ShapeDtypeStruct((t, HALF), jnp.uint32), jax.ShapeDtypeStruct((t, LANES), jnp.int32),
                   jax.ShapeDtypeStruct((t, LANES), F32), jax.ShapeDtypeStruct((1, LANES), F32)],
        scratch_shapes=[pltpu.VMEM((1, LANES), F32)],
        compiler_params=_cparams(("arbitrary",)),
        name="moe_in",
    )(x, sh, sc, w_router_padded)


def _dispatch_kernel(tok_ref, h_ref, o_ref):
    def body(g, carry):
        base = pl.multiple_of(g * 8, 8)
        rows = [h_ref[pl.ds(tok_ref[0, 0, base + u], 1), :] for u in range(8)]
        o_ref[pl.ds(base, 8), :] = jnp.concatenate(rows, axis=0)
        return carry
    lax.fori_loop(0, o_ref.shape[0] // 8, body, 0)


def _dispatch(h_packed, row_tok):
    n_rows = row_tok.shape[0]
    n_blocks = n_rows // MOE_TILE
    return pl.pallas_call(
        _dispatch_kernel,
        grid=(n_blocks,),
        in_specs=[
            pl.BlockSpec((1, 1, MOE_TILE), lambda b: (b, 0, 0), memory_space=pltpu.SMEM),
            pl.BlockSpec(memory_space=pltpu.VMEM),
        ],
        out_specs=pl.BlockSpec((MOE_TILE, HALF), lambda b: (b, 0)),
        out_shape=jax.ShapeDtypeStruct((n_rows, HALF), jnp.uint32),
        compiler_params=_cparams(("arbitrary",)),
        name="moe_dispatch",
    )(row_tok.reshape(n_blocks, 1, MOE_TILE), h_packed)


def _experts_kernel(blk_e_ref, n_used_ref, x_ref, wg_ref, wu_ref, wd_ref, o_ref, xb_ref):
    b = pl.program_id(0)
    f = pl.program_id(1)

    @pl.when(f == 0)
    def _():
        o_ref[...] = jnp.zeros_like(o_ref)
        xb_ref[...] = _unpack_bf16_pairs(x_ref[...])

    @pl.when(b < n_used_ref[0])
    def _():
        x = xb_ref[...]
        g = jnp.dot(x, wg_ref[...].astype(BF16), preferred_element_type=F32)
        u = jnp.dot(x, wu_ref[...].astype(BF16), preferred_element_type=F32)
        a = (_silu(g) * u).astype(BF16)
        o_ref[...] += jnp.dot(a, wd_ref[...].astype(BF16), preferred_element_type=F32)


def _experts(xg, blk_e, n_used, wg, wu, wd, j):
    n_rows = xg.shape[0]
    n_blocks = n_rows // MOE_TILE
    n_ff = E_FF // MOE_FF_TILE
    ff = lambda b, f, nu: jnp.where(b < nu[0], f, n_ff - 1)
    grid_spec = pltpu.PrefetchScalarGridSpec(
        num_scalar_prefetch=2,
        grid=(n_blocks, n_ff),
        in_specs=[
            pl.BlockSpec((MOE_TILE, HALF), lambda b, f, be, nu: (b, 0)),
            pl.BlockSpec((None, None, D_MODEL, MOE_FF_TILE),
                         lambda b, f, be, nu: (j, be[b], 0, ff(b, f, nu))),
            pl.BlockSpec((None, None, D_MODEL, MOE_FF_TILE),
                         lambda b, f, be, nu: (j, be[b], 0, ff(b, f, nu))),
            pl.BlockSpec((None, None, MOE_FF_TILE, D_MODEL),
                         lambda b, f, be, nu: (j, be[b], ff(b, f, nu), 0)),
        ],
        out_specs=pl.BlockSpec((MOE_TILE, D_MODEL), lambda b, f, be, nu: (b, 0)),
        scratch_shapes=[pltpu.VMEM((MOE_TILE, D_MODEL), BF16)],
    )
    return pl.pallas_call(
        _experts_kernel,
        grid_spec=grid_spec,
        out_shape=jax.ShapeDtypeStruct((n_rows, D_MODEL), F32),
        compiler_params=_cparams(("arbitrary", "arbitrary")),
        name="experts",
    )(blk_e, n_used, xg, wg, wu, wd)


def _combine_kernel(x_ref, gate_ref, y0_ref, y1_ref, w_ref, *rest):
    o_ref = rest[-1]
    w = w_ref[...]
    y = w[:, 0:1] * y0_ref[...] + w[:, 1:2] * y1_ref[...]
    x = x_ref[...] + gate_ref[...] * y
    if len(rest) == 2:
        ms = jnp.mean(x * x, axis=-1, keepdims=True)
        x = x * lax.rsqrt(ms + EPS) * rest[0][...]
    o_ref[...] = x


def _combine(x, gate, y0, y1, w, final_g=None):
    t = x.shape[0]
    skip = 0 if final_g is None else CTX_LEN // ROW_TILE
    row = pl.BlockSpec((ROW_TILE, D_MODEL), lambda i: (i + skip, 0))
    in_specs = [row, pl.BlockSpec((None, 1, D_MODEL), lambda i: (jnp.minimum(i + skip, 1), 0, 0)),
                row, row, pl.BlockSpec((ROW_TILE, LANES), lambda i: (i + skip, 0))]
    args = [x, gate, y0, y1, w]
    if final_g is not None:
        in_specs.append(pl.BlockSpec((1, D_MODEL), lambda i: (0, 0)))
        args.append(final_g.reshape(1, D_MODEL))
    return pl.pallas_call(
        _combine_kernel,
        grid=(t // ROW_TILE - skip,),
        in_specs=in_specs,
        out_specs=pl.BlockSpec((ROW_TILE, D_MODEL), lambda i: (i, 0)),
        out_shape=jax.ShapeDtypeStruct((t - skip * ROW_TILE, D_MODEL), F32),
        compiler_params=_cparams(("arbitrary",)),
        name="moe_combine",
    )(*args)


def _moe(x, sh, sc, gate, w_router, wg, wu, wd, j, final_g=None):
    t = x.shape[0]
    wr = jnp.zeros((D_MODEL, LANES), F32).at[:, :N_EXPERTS].set(w_router)
    h, info, top_w, counts = _moe_in(x, sh, sc, wr)
    counts = counts[0, :N_EXPERTS].astype(jnp.int32)
    flat_e = info[:, 0:TOP_K].reshape(-1)
    rank = info[:, TOP_K:2 * TOP_K].reshape(-1)
    padded = ((counts + MOE_TILE - 1) // MOE_TILE) * MOE_TILE
    pad_end = jnp.cumsum(padded)
    pad_start = pad_end - padded
    dest = (pad_start[flat_e] + rank).astype(jnp.int32)
    n_blocks = (t * TOP_K + N_EXPERTS * (MOE_TILE - 1)) // MOE_TILE
    n_rows = n_blocks * MOE_TILE
    flat_tok = jnp.repeat(jnp.arange(t, dtype=jnp.int32), TOP_K)
    row_tok = jnp.zeros((n_rows,), jnp.int32).at[dest].set(
        flat_tok, mode="promise_in_bounds", unique_indices=True)
    blk_start = jnp.arange(n_blocks, dtype=jnp.int32) * MOE_TILE
    blk_e = jnp.minimum(jnp.sum(pad_end[None, :] <= blk_start[:, None], axis=1), N_EXPERTS - 1)
    n_used = (pad_end[-1:] // MOE_TILE).astype(jnp.int32)
    xg = _dispatch(h, row_tok)
    out = _experts(xg, blk_e.astype(jnp.int32), n_used, wg, wu, wd, j)
    pos = dest.reshape(t, TOP_K)
    y0 = out.at[pos[:, 0]].get(mode="promise_in_bounds")
    y1 = out.at[pos[:, 1]].get(mode="promise_in_bounds")
    return _combine(x, gate, y0, y1, top_w, final_g)


def _rope_tables(n):
    n_rows = n // GRID_W
    inv = 1.0 / (ROPE_BASE ** (jnp.arange(0, ROPE_AXIS, 2, dtype=F32) / ROPE_AXIS))
    ar = jnp.arange(n_rows, dtype=F32)[:, None] * inv
    ac = jnp.arange(GRID_W, dtype=F32)[:, None] * inv
    lane = jnp.arange(LANES)
    by_row = (lane % DA_DH) < ROPE_AXIS
    first = (lane % ROPE_AXIS) < (ROPE_AXIS // 2)
    freq = lane % (ROPE_AXIS // 2)

    def lanes(ang):
        c, s = jnp.cos(ang)[:, freq], jnp.sin(ang)[:, freq]
        return jnp.stack([c, jnp.where(first, -s, 0.0), jnp.where(first, 0.0, s)])

    ident = jnp.stack([jnp.ones((1, LANES), F32), jnp.zeros((1, LANES), F32), jnp.zeros((1, LANES), F32)])
    ctx_rows = jnp.broadcast_to(jnp.where(by_row, ident, 0.0), (3, CTX_LEN // GRID_W, LANES))
    row_tab = jnp.concatenate([ctx_rows, jnp.where(by_row, lanes(ar), 0.0)], axis=1)
    row_tab = jnp.broadcast_to(row_tab[:, :, None, :], (3, row_tab.shape[1], 8, LANES))
    ctx_cols = jnp.broadcast_to(jnp.where(by_row, 0.0, ident), (3, GRID_W, LANES))
    col_tab = jnp.stack([ctx_cols, jnp.where(by_row, 0.0, lanes(ac))])
    return row_tab, col_tab


def kernel(x, c, ctx, c_ctx, mod_w, mod_b, hg_lb, ev_w_in, hg_norm_g, sg_ln_g, sg_ln_b, sg_w, sg_b, ev_w_out, da_w_qkv, da_lam, da_subln_g, da_w_out, ffn_w_gate, ffn_w_up, ffn_w_down, moe_w_router, moe_w_gate, moe_w_up, moe_w_down, final_g):
    n = x.shape[1]
    xs = jnp.concatenate([ctx[0], x[0]], axis=0)
    cc_t = jnp.stack([c_ctx, c[0]], axis=1)
    mods = _ada_all(cc_t, mod_w, mod_b).reshape(DEPTH, 2, 6, 1, D_MODEL)
    row_tab, col_tab = _rope_tables(n)
    lb_sm = jax.nn.softmax(hg_lb.astype(F32), axis=1)
    lb_all = jnp.cumsum(lb_sm, axis=1) - lb_sm[:, :1]

    for layer in range(DEPTH):
        j = layer // 2
        sh1, sc1, g1, sh2, sc2, g2 = (mods[layer, :, i] for i in range(6))
        if layer % 2 == 0:
            y = _even_in(xs, sh1, sc1, ev_w_in[j].astype(BF16))
            o_f = _hgrn(y, lb_all[0, j], rev=False)
            a = _hgrn(y, lb_all[1, j], rev=True, o_fwd=o_f, gain=hg_norm_g[j])
            xs = _even_tail(a, y, sg_ln_g[j], sg_ln_b[j], sg_w[j].astype(BF16), sg_b[j].T,
                            ev_w_out[j].astype(BF16), xs, g1, sh2, sc2, g2,
                            ffn_w_gate[j].astype(BF16), ffn_w_up[j].astype(BF16),
                            ffn_w_down[j].astype(BF16))
        else:
            lam_init = 0.8 - 0.6 * math.exp(-0.3 * layer)
            q, k, vt = _qkv(xs, sh1, sc1, da_w_qkv[j].astype(BF16), row_tab, col_tab)
            a_ctx = _attention(q[:CTX_LEN], k[:CTX_LEN], vt[:, :CTX_LEN], da_lam[j], da_subln_g[j],
                               lam_init, CTX_LEN)
            a_lat = _attention(q[CTX_LEN:], k, vt, da_lam[j], da_subln_g[j], lam_init, ATT_Q_TILE)
            a = jnp.concatenate([a_ctx, a_lat], axis=0)
            xs = _proj_res(a, da_w_out[j].astype(BF16), xs, g1)
            xs = _moe(xs, sh2, sc2, g2, moe_w_router[j], moe_w_gate, moe_w_up, moe_w_down, j,
                      final_g if layer == DEPTH - 1 else None)
    return xs[None]
```

```python
import functools
import math

import jax
import jax.numpy as jnp
from jax import lax
from jax.experimental import pallas as pl
from jax.experimental.pallas import tpu as pltpu

F32 = jnp.float32
BF16 = jnp.bfloat16

D_MODEL = 1024
DEPTH = 4
GRID_W = 64
CTX_LEN = 256
EPS = 1e-6

HG_WIDTH = 512
HG_HEADS = 4
HG_D = 128
HG_SUB = 16
HG_PACK = 4
HG_SAFE_DECAY = 80.0
SG_WIDTH = 512
SG_CHUNK = 128
SG_GROUPS = 4
EVEN_IN = 5 * HG_WIDTH + 2 * SG_WIDTH

DA_HEADS = 8
DA_DH = 64
DA_DV = 128
ROPE_BASE = 10000.0
ROPE_AXIS = DA_DH // 2

D_FF = 2816
N_EXPERTS = 8
TOP_K = 2
E_FF = 3584

LOG2E = 1.4426950408889634
LANES = 128
ROW_TILE = CTX_LEN
ATT_Q_TILE = 512
ATT_KV_TILE = 1280
ATT_UNROLL = 13
ATT_MIN_SUM = 2.0 ** -80
MOE_TILE = 1024
MOE_FF_TILE = 512
FFN_FF_TILE = 256
VMEM_LIMIT = 56 * 1024 * 1024


def _cparams(sem):
    return pltpu.CompilerParams(dimension_semantics=sem, vmem_limit_bytes=VMEM_LIMIT)


def _sigmoid(x):
    return 1.0 / (1.0 + jnp.exp(-x))


def _silu(x):
    return x * _sigmoid(x)


def _gelu(x):
    return 0.5 * x * (1.0 + lax.erf(x * (2.0 ** -0.5)))


def _modulated(x, sh, sc):
    ms = jnp.mean(x * x, axis=-1, keepdims=True)
    return x * lax.rsqrt(ms + EPS) * (1.0 + sc) + sh


def _mod_spec(width=D_MODEL):
    return pl.BlockSpec((None, 1, width), lambda i, *_: (jnp.minimum(i, 1), 0, 0))


def _ada_kernel(c_ref, w_ref, b_ref, o_ref):
    s = _silu(c_ref[...])
    w = w_ref[...]
    r0 = jnp.sum(s[:, 0:1] * w, axis=0, keepdims=True)
    r1 = jnp.sum(s[:, 1:2] * w, axis=0, keepdims=True)
    o_ref[0:1, :] = r0 + b_ref[...]
    o_ref[1:2, :] = r1 + b_ref[...]


def _ada_all(cc_t, mod_w, mod_b):
    n = 6 * D_MODEL
    tn = 1536
    return pl.pallas_call(
        _ada_kernel,
        grid=(DEPTH, n // tn),
        in_specs=[
            pl.BlockSpec((D_MODEL, 2), lambda l, j: (0, 0)),
            pl.BlockSpec((None, D_MODEL, tn), lambda l, j: (l, 0, j)),
            pl.BlockSpec((None, 1, tn), lambda l, j: (l, 0, j)),
        ],
        out_specs=pl.BlockSpec((None, 2, tn), lambda l, j: (l, 0, j)),
        out_shape=jax.ShapeDtypeStruct((DEPTH, 2, n), F32),
        compiler_params=_cparams(("arbitrary", "arbitrary")),
        name="ada",
    )(cc_t, mod_w, mod_b.reshape(DEPTH, 1, n))


def _even_in_kernel(x_ref, sh_ref, sc_ref, w_ref, o_ref):
    h = _modulated(x_ref[...], sh_ref[...], sc_ref[...]).astype(BF16)
    for n in range(0, EVEN_IN, 512):
        o_ref[:, n:n + 512] = jnp.dot(h, w_ref[:, n:n + 512], preferred_element_type=F32)


def _even_in(x, sh, sc, w):
    t = x.shape[0]
    return pl.pallas_call(
        _even_in_kernel,
        grid=(t // ROW_TILE,),
        in_specs=[
            pl.BlockSpec((ROW_TILE, D_MODEL), lambda i: (i, 0)),
            _mod_spec(), _mod_spec(),
            pl.BlockSpec((D_MODEL, EVEN_IN), lambda i: (0, 0)),
        ],
        out_specs=pl.BlockSpec((ROW_TILE, EVEN_IN), lambda i: (i, 0)),
        out_shape=jax.ShapeDtypeStruct((t, EVEN_IN), F32),
        compiler_params=_cparams(("arbitrary",)),
        name="even_in",
    )(x, sh, sc, w)


def _group_scan(x, r_in_group, forward):
    rows = x.shape[0]
    for sh in (1, 2, 4, 8):
        if forward:
            rolled = pltpu.roll(x, sh, 0)
            ok = r_in_group >= sh
        else:
            rolled = pltpu.roll(x, rows - sh, 0)
            ok = r_in_group < HG_SUB - sh
        x = x + jnp.where(ok, rolled, 0.0)
    return x


def _hgrn_head(q, f, v, lb, st_ref, acc_ref, rev):
    rows = q.shape[0]
    n_groups = rows // HG_SUB
    t_abs = jnp.exp(-jnp.abs(f))
    big = 1.0 / (1.0 + t_abs)
    small = t_abs * big
    log_f = jnp.log(lb + (1.0 - lb) * jnp.where(f >= 0, big, small))
    kk = (1.0 - lb) * jnp.where(f >= 0, small, big)
    qs = _silu(q)

    row = lax.broadcasted_iota(jnp.int32, (rows, HG_D), 0)
    r16 = row & (HG_SUB - 1)
    grp = row // HG_SUB
    pfx = _group_scan(log_f, r16, True)
    sfx = _group_scan(log_f, r16, False)
    cum, other = (sfx, pfx) if rev else (pfx, sfx)
    qd = (qs * jnp.exp(cum)).astype(BF16)
    kd = (kk * jnp.exp(other - log_f)).astype(BF16)

    kn = (kk * jnp.exp(-cum)).astype(BF16)
    sc = lax.dot_general(qd, kn, (((1,), (1,)), ((), ())), preferred_element_type=F32)
    ti = lax.broadcasted_iota(jnp.int32, (rows, rows), 0)
    si = lax.broadcasted_iota(jnp.int32, (rows, rows), 1)
    reach = ti & (HG_SUB - 1)
    dist, reach = (si - ti, HG_SUB - 1 - reach) if rev else (ti - si, reach)
    ok = dist.astype(jnp.uint32) <= reach.astype(jnp.uint32)
    acc_factored = jnp.dot(jnp.where(ok, sc, 0.0).astype(BF16), v.astype(BF16),
                           preferred_element_type=F32)

    v_t = v.T.astype(BF16)
    bd = jnp.concatenate(
        [jnp.where(grp == g, kd, jnp.zeros_like(kd)) for g in range(n_groups)], axis=1)
    u = jnp.dot(v_t, bd, preferred_element_type=F32)

    st = st_ref[...]
    outs = [None] * n_groups
    for g in (range(n_groups - 1, -1, -1) if rev else range(n_groups)):
        lo = g * HG_SUB
        outs[g] = lax.dot_general(qd[lo:lo + HG_SUB], st.astype(BF16),
                                  (((1,), (1,)), ((), ())), preferred_element_type=F32)
        edge = lo if rev else lo + HG_SUB - 1
        st = st * jnp.exp(cum[edge:edge + 1, :]) + u[:, g * HG_D:(g + 1) * HG_D]
    st_ref[...] = st
    o_inter = jnp.concatenate(outs, axis=0)
    acc_ref[...] = acc_factored + o_inter

    def exact():
        ones = jnp.ones((HG_D, HG_D), BF16)
        acc = o_inter
        for d in range(HG_SUB):
            if d == 0:
                kk_s, cum_s, v_s = kk, cum, v
                p = qs * kk_s
            else:
                shift = rows - d if rev else d
                kk_s = pltpu.roll(kk, shift, 0)
                cum_s = pltpu.roll(cum, shift, 0)
                v_s = pltpu.roll(v, shift, 0)
                ok_d = (r16 < HG_SUB - d) if rev else (r16 >= d)
                p = jnp.where(ok_d, qs * kk_s * jnp.exp(cum - cum_s), 0.0)
            rsum = jnp.dot(p.astype(BF16), ones, preferred_element_type=F32)
            acc = acc + rsum * v_s
        acc_ref[...] = acc

    return jnp.logical_not(jnp.min(cum) >= -HG_SAFE_DECAY), exact


def _hgrn_kernel(*refs, rev, readout):
    if readout:
        q_ref, f_ref, i_ref, lb_ref, of_ref, g_ref, gain_ref, o_ref, st_ref, acc_ref = refs
    else:
        q_ref, f_ref, i_ref, lb_ref, o_ref, st_ref, acc_ref = refs

    @pl.when(pl.program_id(1) == 0)
    def _():
        st_ref[...] = jnp.zeros_like(st_ref)

    lanes = [slice(h * HG_D, (h + 1) * HG_D) for h in range(HG_PACK)]
    redo = [_hgrn_head(q_ref[:, sl], f_ref[:, sl], i_ref[:, sl], lb_ref[h], st_ref.at[h],
                       acc_ref.at[h], rev) for h, sl in enumerate(lanes)]
    for needs_exact, exact in redo:
        pl.when(needs_exact)(exact)

    for h, sl in enumerate(lanes):
        o = acc_ref[h]
        if readout:
            o = o + of_ref[:, sl]
            ms = jnp.mean(o * o, axis=-1, keepdims=True)
            o = o * lax.rsqrt(ms + EPS) * gain_ref[...] * _silu(g_ref[:, sl])
        o_ref[:, sl] = o.astype(o_ref.dtype)


def _hgrn(y, lb, rev, o_fwd=None, gain=None):
    t = y.shape[0]
    nblk = t // ROW_TILE
    width = HG_PACK * HG_D
    per_slab = HG_WIDTH // width
    readout = o_fwd is not None
    if rev:
        blk = lambda c: jnp.where(c == 0, 0, nblk - c)
    else:
        blk = lambda c: c
    col = lambda slab: pl.BlockSpec((ROW_TILE, width), lambda h, c: (blk(c), slab * per_slab + h))
    in_specs = [col(0), col(2 if rev else 1), col(3),
                pl.BlockSpec((HG_PACK, 1, HG_D), lambda h, c: (h, 0, 0))]
    args = [y, y, y, lb.reshape(HG_HEADS, 1, HG_D)]
    if readout:
        in_specs += [pl.BlockSpec((ROW_TILE, width), lambda h, c: (blk(c), h)), col(4),
                     pl.BlockSpec((1, HG_D), lambda h, c: (0, 0))]
        args += [o_fwd, y, gain.reshape(1, HG_D)]
    return pl.pallas_call(
        functools.partial(_hgrn_kernel, rev=rev, readout=readout),
        grid=(per_slab, nblk),
        in_specs=in_specs,
        out_specs=pl.BlockSpec((ROW_TILE, width), lambda h, c: (blk(c), h)),
        out_shape=jax.ShapeDtypeStruct((t, HG_WIDTH), BF16 if readout else F32),
        scratch_shapes=[pltpu.VMEM((HG_PACK, HG_D, HG_D), F32),
                        pltpu.VMEM((HG_PACK, ROW_TILE, HG_D), F32)],
        compiler_params=_cparams(("arbitrary", "arbitrary")),
        name="hgrn_bwd" if rev else "hgrn_fwd",
    )(*args)


def _even_tail_kernel(a_ref, u_ref, v_ref, lng_ref, lnb_ref, sgw_ref, sgb_ref, w_ref, x_ref,
                      gate1_ref, sh_ref, sc_ref, gate2_ref, wg_ref, wu_ref, wd_ref, o_ref):
    u = _gelu(u_ref[...])
    v = _gelu(v_ref[...])
    mu = jnp.mean(v, axis=-1, keepdims=True)
    var = jnp.mean(jnp.square(v - mu), axis=-1, keepdims=True)
    vn = ((v - mu) * lax.rsqrt(var + 1e-5) * lng_ref[...] + lnb_ref[...]).astype(BF16)
    rows = u.shape[0]
    mixed_rows = []
    for c in range(rows // SG_CHUNK):
        lo = c * SG_CHUNK
        parts = []
        for g in range(SG_GROUPS):
            m = jnp.dot(sgw_ref[g], vn[lo:lo + SG_CHUNK, g * LANES:(g + 1) * LANES],
                        preferred_element_type=F32)
            parts.append(m + sgb_ref[:, g:g + 1])
        mixed_rows.append(jnp.concatenate(parts, axis=1))
    s = (u * jnp.concatenate(mixed_rows, axis=0)).astype(BF16)
    y = jnp.dot(jnp.concatenate([a_ref[...], s], axis=1), w_ref[...], preferred_element_type=F32)
    x = x_ref[...] + gate1_ref[...] * y

    h = _modulated(x, sh_ref[...], sc_ref[...]).astype(BF16)
    parts = []
    for n in range(0, D_FF, FFN_FF_TILE):
        g = jnp.dot(h, wg_ref[:, n:n + FFN_FF_TILE], preferred_element_type=F32)
        up = jnp.dot(h, wu_ref[:, n:n + FFN_FF_TILE], preferred_element_type=F32)
        parts.append((_silu(g) * up).astype(BF16))
    y = jnp.dot(jnp.concatenate(parts, axis=1), wd_ref[...], preferred_element_type=F32)
    o_ref[...] = x + gate2_ref[...] * y


def _even_tail(a, y_proj, ln_g, ln_b, sg_w, sg_b_t, w_out, x, gate1, sh, sc, gate2, wg, wu, wd):
    t = x.shape[0]
    full = lambda shape: pl.BlockSpec(shape, lambda i: (0,) * len(shape))
    return pl.pallas_call(
        _even_tail_kernel,
        grid=(t // ROW_TILE,),
        in_specs=[
            pl.BlockSpec((ROW_TILE, HG_WIDTH), lambda i: (i, 0)),
            pl.BlockSpec((ROW_TILE, SG_WIDTH), lambda i: (i, 5)),
            pl.BlockSpec((ROW_TILE, SG_WIDTH), lambda i: (i, 6)),
            full((1, SG_WIDTH)), full((1, SG_WIDTH)),
            full((SG_GROUPS, SG_CHUNK, SG_CHUNK)), full((SG_CHUNK, SG_GROUPS)),
            full((D_MODEL, D_MODEL)),
            pl.BlockSpec((ROW_TILE, D_MODEL), lambda i: (i, 0)),
            _mod_spec(), _mod_spec(), _mod_spec(), _mod_spec(),
            full((D_MODEL, D_FF)), full((D_MODEL, D_FF)), full((D_FF, D_MODEL)),
        ],
        out_specs=pl.BlockSpec((ROW_TILE, D_MODEL), lambda i: (i, 0)),
        out_shape=jax.ShapeDtypeStruct((t, D_MODEL), F32),
        compiler_params=_cparams(("arbitrary",)),
        name="even_tail",
    )(a, y_proj, y_proj, ln_g.reshape(1, -1), ln_b.reshape(1, -1), sg_w, sg_b_t, w_out, x, gate1,
      sh, sc, gate2, wg, wu, wd)


def _qkv_kernel(x_ref, sh_ref, sc_ref, w_ref, rt_ref, ct_ref, q_ref, k_ref, vt_ref):
    h = _modulated(x_ref[...], sh_ref[...], sc_ref[...]).astype(BF16)
    grid_rows = ROW_TILE // GRID_W

    def table(i):
        by_row = jnp.concatenate(
            [rt_ref[i, r] for r in range(grid_rows) for _ in range(GRID_W // 8)], axis=0)
        return by_row + jnp.concatenate([ct_ref[i]] * grid_rows, axis=0)

    cos, sa, sb = table(0), table(1), table(2)

    def rope(z):
        return z * cos + pltpu.roll(z, LANES - 16, 1) * sa + pltpu.roll(z, 16, 1) * sb

    for n in range(0, D_MODEL, 512):
        q = jnp.dot(h, w_ref[:, n:n + 512], preferred_element_type=F32)
        k = jnp.dot(h, w_ref[:, D_MODEL + n:D_MODEL + n + 512], preferred_element_type=F32)
        for m in range(0, 512, LANES):
            q_ref[:, n + m:n + m + LANES] = (
                rope(q[:, m:m + LANES]) * (DA_DH ** -0.5 * LOG2E)).astype(BF16)
            k_ref[:, n + m:n + m + LANES] = rope(k[:, m:m + LANES]).astype(BF16)
        v = jnp.dot(h, w_ref[:, 2 * D_MODEL + n:2 * D_MODEL + n + 512], preferred_element_type=F32)
        vt_ref[n:n + 512, :] = v.T.astype(BF16)


def _qkv(x, sh, sc, w, row_tab, col_tab):
    t = x.shape[0]
    grid_rows = ROW_TILE // GRID_W
    row = lambda width: pl.BlockSpec((ROW_TILE, width), lambda i: (i, 0))
    return pl.pallas_call(
        _qkv_kernel,
        grid=(t // ROW_TILE,),
        in_specs=[row(D_MODEL), _mod_spec(), _mod_spec(),
                  pl.BlockSpec((D_MODEL, 3 * D_MODEL), lambda i: (0, 0)),
                  pl.BlockSpec((3, grid_rows, 8, LANES), lambda i: (0, i, 0, 0)),
                  pl.BlockSpec((None, 3, GRID_W, LANES), lambda i: (jnp.minimum(i, 1), 0, 0, 0))],
        out_specs=[row(D_MODEL), row(D_MODEL), pl.BlockSpec((D_MODEL, ROW_TILE), lambda i: (0, i))],
        out_shape=[jax.ShapeDtypeStruct((t, D_MODEL), BF16), jax.ShapeDtypeStruct((t, D_MODEL), BF16),
                   jax.ShapeDtypeStruct((D_MODEL, t), BF16)],
        compiler_params=_cparams(("arbitrary",)),
        name="qkv",
    )(x, sh, sc, w, row_tab, col_tab)


def _attn_kernel(q_ref, k_ref, vt_ref, lam_ref, g_ref, o_ref, m_ref, l_ref, acc_ref, kmax_ref, *,
                 lam_init, n_keys):
    tq = q_ref.shape[0]
    q = q_ref[...]
    lane = lax.broadcasted_iota(jnp.int32, q.shape, 1)
    zero = jnp.zeros_like(q)
    qq = jnp.concatenate([jnp.where(lane < DA_DH, q, zero), jnp.where(lane >= DA_DH, q, zero)], axis=0)

    tk = min(ATT_KV_TILE, n_keys)
    n_chunks = n_keys // tk
    ones8 = jnp.ones((8, DA_DV), BF16)
    nt = (((1,), (1,)), ((), ()))

    @pl.when(pl.program_id(1) == 0)
    def _():
        lane8 = lax.broadcasted_iota(jnp.int32, (8, DA_DV), 1)
        row8 = lax.broadcasted_iota(jnp.int32, (8, DA_DV), 0)
        sel = jnp.where((lane8 >= DA_DH) == (row8 == 1), 1.0, 0.0).astype(BF16)

        def body(c, best):
            kc = k_ref[pl.ds(pl.multiple_of(c * tk, tk), tk), :].astype(F32)
            n2 = lax.dot_general(sel, (kc * kc).astype(BF16), nt, preferred_element_type=F32)
            return jnp.maximum(best, n2)
        best = lax.fori_loop(0, n_chunks, body, jnp.zeros((8, tk), F32))
        kmax_ref[...] = jnp.broadcast_to(jnp.max(best, axis=1, keepdims=True), kmax_ref.shape)

    qf = qq.astype(F32)
    qn2 = lax.dot_general(ones8, (qf * qf).astype(BF16), nt, preferred_element_type=F32)[0:1]
    col = lax.broadcasted_iota(jnp.int32, (1, 2 * tq), 1)
    kn2 = jnp.where(col < tq, kmax_ref[0:1, 0:1], kmax_ref[1:2, 0:1])
    m_bound = jnp.sqrt(qn2 * kn2)

    def scores(off, size):
        k = k_ref[pl.ds(off, size), :]
        return lax.dot_general(k, qq, nt, preferred_element_type=F32)

    def fast_chunk(off, size):
        p = jnp.exp2(scores(off, size) - m_bound)
        l_ref[...] += jnp.sum(p, axis=0, keepdims=True)
        acc_ref[...] += jnp.dot(vt_ref[:, pl.ds(off, size)], p.astype(BF16), preferred_element_type=F32)

    def online_chunk(off, size):
        s = scores(off, size)
        m_old = m_ref[...]
        m_new = jnp.maximum(m_old, jnp.max(s, axis=0, keepdims=True))
        alpha = jnp.exp2(m_old - m_new)
        p = jnp.exp2(s - m_new)
        l_ref[...] = alpha * l_ref[...] + jnp.sum(p, axis=0, keepdims=True)
        acc_ref[...] = alpha * acc_ref[...] + jnp.dot(vt_ref[:, pl.ds(off, size)], p.astype(BF16),
                                                      preferred_element_type=F32)
        m_ref[...] = m_new

    def all_keys(chunk, unroll):
        def body(jj, carry):
            for u in range(unroll):
                chunk(pl.multiple_of((unroll * jj + u) * tk, tk), tk)
            return carry
        if n_chunks // unroll > 0:
            lax.fori_loop(0, n_chunks // unroll, body, 0)
        for c in range(n_chunks - n_chunks % unroll, n_chunks):
            chunk(c * tk, tk)

    l_ref[...] = jnp.zeros_like(l_ref)
    acc_ref[...] = jnp.zeros_like(acc_ref)
    all_keys(fast_chunk, ATT_UNROLL)

    @pl.when(jnp.logical_not(jnp.min(l_ref[...]) >= ATT_MIN_SUM))
    def _():
        m_ref[...] = jnp.full_like(m_ref, -jnp.inf)
        l_ref[...] = jnp.zeros_like(l_ref)
        acc_ref[...] = jnp.zeros_like(acc_ref)
        all_keys(online_chunk, 1)

    lam = lam_ref[...]
    lam_full = (jnp.exp(jnp.sum(lam[0:1] * lam[1:2], axis=-1, keepdims=True))
                - jnp.exp(jnp.sum(lam[2:3] * lam[3:4], axis=-1, keepdims=True)) + lam_init)
    o_all = acc_ref[...] / l_ref[...]
    o = (o_all[:, 0:tq] - lam_full * o_all[:, tq:2 * tq]).T
    ms = jnp.mean(o * o, axis=-1, keepdims=True)
    o = o * lax.rsqrt(ms + 1e-5) * g_ref[...] * (1.0 - lam_init)
    o_ref[...] = o.astype(o_ref.dtype)


def _attention(q, k, vt, lam, subln_g, lam_init, tq):
    n_q, n_keys = q.shape[0], k.shape[0]
    assert n_q % tq == 0 and n_keys % min(ATT_KV_TILE, n_keys) == 0
    stat = pltpu.VMEM((1, 2 * tq), F32)
    return pl.pallas_call(
        functools.partial(_attn_kernel, lam_init=lam_init, n_keys=n_keys),
        grid=(DA_HEADS, n_q // tq),
        in_specs=[
            pl.BlockSpec((tq, DA_DV), lambda h, i: (i, h)),
            pl.BlockSpec((n_keys, DA_DV), lambda h, i: (0, h)),
            pl.BlockSpec((DA_DV, n_keys), lambda h, i: (h, 0)),
            pl.BlockSpec((4, DA_DH), lambda h, i: (0, 0)),
            pl.BlockSpec((1, DA_DV), lambda h, i: (0, 0)),
        ],
        out_specs=pl.BlockSpec((tq, DA_DV), lambda h, i: (i, h)),
        out_shape=jax.ShapeDtypeStruct((n_q, D_MODEL), BF16),
        scratch_shapes=[stat, stat, pltpu.VMEM((DA_DV, 2 * tq), F32), pltpu.VMEM((8, LANES), F32)],
        compiler_params=_cparams(("arbitrary", "arbitrary")),
        name="attention",
    )(q, k, vt, lam, subln_g.reshape(1, DA_DV))


def _proj_res_kernel(a_ref, w_ref, x_ref, gate_ref, o_ref):
    y = jnp.dot(a_ref[...], w_ref[...], preferred_element_type=F32)
    o_ref[...] = x_ref[...] + gate_ref[...] * y


def _proj_res(a, w, x, gate):
    t = x.shape[0]
    row = pl.BlockSpec((ROW_TILE, D_MODEL), lambda i: (i, 0))
    return pl.pallas_call(
        _proj_res_kernel,
        grid=(t // ROW_TILE,),
        in_specs=[row, pl.BlockSpec((D_MODEL, D_MODEL), lambda i: (0, 0)), row, _mod_spec()],
        out_specs=row,
        out_shape=jax.ShapeDtypeStruct((t, D_MODEL), F32),
        compiler_params=_cparams(("arbitrary",)),
        name="proj_res",
    )(a, w, x, gate)


HALF = D_MODEL // 2


def _pack_bf16_pairs(h):
    bits = pltpu.bitcast(h.astype(BF16).astype(F32), jnp.uint32)
    return (bits[:, HALF:] & jnp.uint32(0xFFFF0000)) | (bits[:, :HALF] >> 16)


def _unpack_bf16_pairs(w):
    lo = pltpu.bitcast(w << 16, F32)
    hi = pltpu.bitcast(w & jnp.uint32(0xFFFF0000), F32)
    return jnp.concatenate([lo, hi], axis=1).astype(BF16)


def _moe_in_kernel(x_ref, sh_ref, sc_ref, wr_ref, h_ref, info_ref, w_ref, cnt_ref, run_ref):
    @pl.when(pl.program_id(0) == 0)
    def _():
        run_ref[...] = jnp.zeros_like(run_ref)

    h = _modulated(x_ref[...], sh_ref[...], sc_ref[...])
    h_ref[...] = _pack_bf16_pairs(h)
    logits = jnp.dot(h, wr_ref[...], preferred_element_type=F32, precision=lax.Precision.HIGHEST)
    rows = logits.shape[0]
    lane = lax.broadcasted_iota(jnp.int32, logits.shape, 1)
    masked = jnp.where(lane < N_EXPERTS, logits, -jnp.inf)

    def top(l):
        m = jnp.max(l, axis=-1, keepdims=True)
        return m, jnp.min(jnp.where(l == m, lane, LANES), axis=-1, keepdims=True)

    m1, e1 = top(masked)
    m2, e2 = top(jnp.where(lane == e1, -jnp.inf, masked))
    ratio = jnp.exp(m2 - m1)
    w1 = 1.0 / (1.0 + ratio)
    w_ref[...] = jnp.where(lane == 0, w1, jnp.where(lane == 1, ratio * w1, 0.0))

    hit1 = lane == e1
    hit2 = lane == e2
    hits = jnp.where(hit1, 1.0, 0.0) + jnp.where(hit2, 1.0, 0.0)
    earlier = (lax.broadcasted_iota(jnp.int32, (rows, rows), 1)
               < lax.broadcasted_iota(jnp.int32, (rows, rows), 0))
    before = run_ref[...] + jnp.dot(jnp.where(earlier, 1.0, 0.0).astype(BF16), hits.astype(BF16),
                                    preferred_element_type=F32)
    r1 = jnp.sum(jnp.where(hit1, before, 0.0), axis=-1, keepdims=True).astype(jnp.int32)
    r2 = jnp.sum(jnp.where(hit2, before, 0.0), axis=-1, keepdims=True).astype(jnp.int32)
    info_ref[...] = jnp.where(lane == 0, e1, jnp.where(lane == 1, e2, jnp.where(
        lane == 2, r1, jnp.where(lane == 3, r2, 0))))
    run_ref[...] += jnp.sum(hits, axis=0, keepdims=True)
    cnt_ref[...] = run_ref[...]


def _moe_in(x, sh, sc, w_router_padded):
    t = x.shape[0]
    row = lambda width: pl.BlockSpec((ROW_TILE, width), lambda i: (i, 0))
    return pl.pallas_call(
        _moe_in_kernel,
        grid=(t // ROW_TILE,),
        in_specs=[row(D_MODEL), _mod_spec(), _mod_spec(),
                  pl.BlockSpec((D_MODEL, LANES), lambda i: (0, 0))],
        out_specs=[row(HALF), row(LANES), row(LANES), pl.BlockSpec((1, LANES), lambda i: (0, 0))],
        out_shape=[jax.ShapeDtypeStruct((t, HALF), jnp.uint32), jax.ShapeDtypeStruct((t, LANES), jnp.int32),
                   jax.ShapeDtypeStruct((t, LANES), F32), jax.ShapeDtypeStruct((1, LANES), F32)],
        scratch_shapes=[pltpu.VMEM((1, LANES), F32)],
        compiler_params=_cparams(("arbitrary",)),
        name="moe_in",
    )(x, sh, sc, w_router_padded)


def _dispatch_kernel(tok_ref, h_ref, o_ref):
    def body(g, carry):
        base = pl.multiple_of(g * 8, 8)
        rows = [h_ref[pl.ds(tok_ref[0, 0, base + u], 1), :] for u in range(8)]
        o_ref[pl.ds(base, 8), :] = jnp.concatenate(rows, axis=0)
        return carry
    lax.fori_loop(0, o_ref.shape[0] // 8, body, 0)


def _dispatch(h_packed, row_tok):
    n_rows = row_tok.shape[0]
    n_blocks = n_rows // MOE_TILE
    return pl.pallas_call(
        _dispatch_kernel,
        grid=(n_blocks,),
        in_specs=[
            pl.BlockSpec((1, 1, MOE_TILE), lambda b: (b, 0, 0), memory_space=pltpu.SMEM),
            pl.BlockSpec(memory_space=pltpu.VMEM),
        ],
        out_specs=pl.BlockSpec((MOE_TILE, HALF), lambda b: (b, 0)),
        out_shape=jax.ShapeDtypeStruct((n_rows, HALF), jnp.uint32),
        compiler_params=_cparams(("arbitrary",)),
        name="moe_dispatch",
    )(row_tok.reshape(n_blocks, 1, MOE_TILE), h_packed)


def _experts_kernel(blk_e_ref, n_used_ref, x_ref, wg_ref, wu_ref, wd_ref, o_ref, xb_ref):
    b = pl.program_id(0)
    f = pl.program_id(1)

    @pl.when(f == 0)
    def _():
        o_ref[...] = jnp.zeros_like(o_ref)
        xb_ref[...] = _unpack_bf16_pairs(x_ref[...])

    @pl.when(b < n_used_ref[0])
    def _():
        x = xb_ref[...]
        g = jnp.dot(x, wg_ref[...].astype(BF16), preferred_element_type=F32)
        u = jnp.dot(x, wu_ref[...].astype(BF16), preferred_element_type=F32)
        a = (_silu(g) * u).astype(BF16)
        o_ref[...] += jnp.dot(a, wd_ref[...].astype(BF16), preferred_element_type=F32)


def _experts(xg, blk_e, n_used, wg, wu, wd, j):
    n_rows = xg.shape[0]
    n_blocks = n_rows // MOE_TILE
    n_ff = E_FF // MOE_FF_TILE
    ff = lambda b, f, nu: jnp.where(b < nu[0], f, n_ff - 1)
    grid_spec = pltpu.PrefetchScalarGridSpec(
        num_scalar_prefetch=2,
        grid=(n_blocks, n_ff),
        in_specs=[
            pl.BlockSpec((MOE_TILE, HALF), lambda b, f, be, nu: (b, 0)),
            pl.BlockSpec((None, None, D_MODEL, MOE_FF_TILE),
                         lambda b, f, be, nu: (j, be[b], 0, ff(b, f, nu))),
            pl.BlockSpec((None, None, D_MODEL, MOE_FF_TILE),
                         lambda b, f, be, nu: (j, be[b], 0, ff(b, f, nu))),
            pl.BlockSpec((None, None, MOE_FF_TILE, D_MODEL),
                         lambda b, f, be, nu: (j, be[b], ff(b, f, nu), 0)),
        ],
        out_specs=pl.BlockSpec((MOE_TILE, D_MODEL), lambda b, f, be, nu: (b, 0)),
        scratch_shapes=[pltpu.VMEM((MOE_TILE, D_MODEL), BF16)],
    )
    return pl.pallas_call(
        _experts_kernel,
        grid_spec=grid_spec,
        out_shape=jax.ShapeDtypeStruct((n_rows, D_MODEL), F32),
        compiler_params=_cparams(("arbitrary", "arbitrary")),
        name="experts",
    )(blk_e, n_used, xg, wg, wu, wd)


def _combine_kernel(x_ref, gate_ref, y0_ref, y1_ref, w_ref, *rest):
    o_ref = rest[-1]
    w = w_ref[...]
    y = w[:, 0:1] * y0_ref[...] + w[:, 1:2] * y1_ref[...]
    x = x_ref[...] + gate_ref[...] * y
    if len(rest) == 2:
        ms = jnp.mean(x * x, axis=-1, keepdims=True)
        x = x * lax.rsqrt(ms + EPS) * rest[0][...]
    o_ref[...] = x


def _combine(x, gate, y0, y1, w, final_g=None):
    t = x.shape[0]
    skip = 0 if final_g is None else CTX_LEN // ROW_TILE
    row = pl.BlockSpec((ROW_TILE, D_MODEL), lambda i: (i + skip, 0))
    in_specs = [row, pl.BlockSpec((None, 1, D_MODEL), lambda i: (jnp.minimum(i + skip, 1), 0, 0)),
                row, row, pl.BlockSpec((ROW_TILE, LANES), lambda i: (i + skip, 0))]
    args = [x, gate, y0, y1, w]
    if final_g is not None:
        in_specs.append(pl.BlockSpec((1, D_MODEL), lambda i: (0, 0)))
        args.append(final_g.reshape(1, D_MODEL))
    return pl.pallas_call(
        _combine_kernel,
        grid=(t // ROW_TILE - skip,),
        in_specs=in_specs,
        out_specs=pl.BlockSpec((ROW_TILE, D_MODEL), lambda i: (i, 0)),
        out_shape=jax.ShapeDtypeStruct((t - skip * ROW_TILE, D_MODEL), F32),
        compiler_params=_cparams(("arbitrary",)),
        name="moe_combine",
    )(*args)


def _moe(x, sh, sc, gate, w_router, wg, wu, wd, j, final_g=None):
    t = x.shape[0]
    wr = jnp.zeros((D_MODEL, LANES), F32).at[:, :N_EXPERTS].set(w_router)
    h, info, top_w, counts = _moe_in(x, sh, sc, wr)
    counts = counts[0, :N_EXPERTS].astype(jnp.int32)
    flat_e = info[:, 0:TOP_K].reshape(-1)
    rank = info[:, TOP_K:2 * TOP_K].reshape(-1)
    padded = ((counts + MOE_TILE - 1) // MOE_TILE) * MOE_TILE
    pad_end = jnp.cumsum(padded)
    pad_start = pad_end - padded
    dest = (pad_start[flat_e] + rank).astype(jnp.int32)
    n_blocks = (t * TOP_K + N_EXPERTS * (MOE_TILE - 1)) // MOE_TILE
    n_rows = n_blocks * MOE_TILE
    flat_tok = jnp.repeat(jnp.arange(t, dtype=jnp.int32), TOP_K)
    row_tok = jnp.zeros((n_rows,), jnp.int32).at[dest].set(
        flat_tok, mode="promise_in_bounds", unique_indices=True)
    blk_start = jnp.arange(n_blocks, dtype=jnp.int32) * MOE_TILE
    blk_e = jnp.minimum(jnp.sum(pad_end[None, :] <= blk_start[:, None], axis=1), N_EXPERTS - 1)
    n_used = (pad_end[-1:] // MOE_TILE).astype(jnp.int32)
    xg = _dispatch(h, row_tok)
    out = _experts(xg, blk_e.astype(jnp.int32), n_used, wg, wu, wd, j)
    pos = dest.reshape(t, TOP_K)
    y0 = out.at[pos[:, 0]].get(mode="promise_in_bounds")
    y1 = out.at[pos[:, 1]].get(mode="promise_in_bounds")
    return _combine(x, gate, y0, y1, top_w, final_g)


def _rope_tables(n):
    n_rows = n // GRID_W
    inv = 1.0 / (ROPE_BASE ** (jnp.arange(0, ROPE_AXIS, 2, dtype=F32) / ROPE_AXIS))
    ar = jnp.arange(n_rows, dtype=F32)[:, None] * inv
    ac = jnp.arange(GRID_W, dtype=F32)[:, None] * inv
    lane = jnp.arange(LANES)
    by_row = (lane % DA_DH) < ROPE_AXIS
    first = (lane % ROPE_AXIS) < (ROPE_AXIS // 2)
    freq = lane % (ROPE_AXIS // 2)

    def lanes(ang):
        c, s = jnp.cos(ang)[:, freq], jnp.sin(ang)[:, freq]
        return jnp.stack([c, jnp.where(first, -s, 0.0), jnp.where(first, 0.0, s)])

    ident = jnp.stack([jnp.ones((1, LANES), F32), jnp.zeros((1, LANES), F32), jnp.zeros((1, LANES), F32)])
    ctx_rows = jnp.broadcast_to(jnp.where(by_row, ident, 0.0), (3, CTX_LEN // GRID_W, LANES))
    row_tab = jnp.concatenate([ctx_rows, jnp.where(by_row, lanes(ar), 0.0)], axis=1)
    row_tab = jnp.broadcast_to(row_tab[:, :, None, :], (3, row_tab.shape[1], 8, LANES))
    ctx_cols = jnp.broadcast_to(jnp.where(by_row, 0.0, ident), (3, GRID_W, LANES))
    col_tab = jnp.stack([ctx_cols, jnp.where(by_row, 0.0, lanes(ac))])
    return row_tab, col_tab


def kernel(x, c, ctx, c_ctx, mod_w, mod_b, hg_lb, ev_w_in, hg_norm_g, sg_ln_g, sg_ln_b, sg_w, sg_b, ev_w_out, da_w_qkv, da_lam, da_subln_g, da_w_out, ffn_w_gate, ffn_w_up, ffn_w_down, moe_w_router, moe_w_gate, moe_w_up, moe_w_down, final_g):
    n = x.shape[1]
    xs = jnp.concatenate([ctx[0], x[0]], axis=0)
    cc_t = jnp.stack([c_ctx, c[0]], axis=1)
    mods = _ada_all(cc_t, mod_w, mod_b).reshape(DEPTH, 2, 6, 1, D_MODEL)
    row_tab, col_tab = _rope_tables(n)
    lb_sm = jax.nn.softmax(hg_lb.astype(F32), axis=1)
    lb_all = jnp.cumsum(lb_sm, axis=1) - lb_sm[:, :1]

    for layer in range(DEPTH):
        j = layer // 2
        sh1, sc1, g1, sh2, sc2, g2 = (mods[layer, :, i] for i in range(6))
        if layer % 2 == 0:
            y = _even_in(xs, sh1, sc1, ev_w_in[j].astype(BF16))
            o_f = _hgrn(y, lb_all[0, j], rev=False)
            a = _hgrn(y, lb_all[1, j], rev=True, o_fwd=o_f, gain=hg_norm_g[j])
            xs = _even_tail(a, y, sg_ln_g[j], sg_ln_b[j], sg_w[j].astype(BF16), sg_b[j].T,
                            ev_w_out[j].astype(BF16), xs, g1, sh2, sc2, g2,
                            ffn_w_gate[j].astype(BF16), ffn_w_up[j].astype(BF16),
                            ffn_w_down[j].astype(BF16))
        else:
            lam_init = 0.8 - 0.6 * math.exp(-0.3 * layer)
            q, k, vt = _qkv(xs, sh1, sc1, da_w_qkv[j].astype(BF16), row_tab, col_tab)
            a_ctx = _attention(q[:CTX_LEN], k[:CTX_LEN], vt[:, :CTX_LEN], da_lam[j], da_subln_g[j],
                               lam_init, CTX_LEN)
            a_lat = _attention(q[CTX_LEN:], k, vt, da_lam[j], da_subln_g[j], lam_init, ATT_Q_TILE)
            a = jnp.concatenate([a_ctx, a_lat], axis=0)
            xs = _proj_res(a, da_w_out[j].astype(BF16), xs, g1)
            xs = _moe(xs, sh2, sc2, g2, moe_w_router[j], moe_w_gate, moe_w_up, moe_w_down, j,
                      final_g if layer == DEPTH - 1 else None)
    return xs[None]
```

```python
import functools
import math

import jax
import jax.numpy as jnp
from jax import lax
from jax.experimental import pallas as pl
from jax.experimental.pallas import tpu as pltpu

F32 = jnp.float32
BF16 = jnp.bfloat16

D_MODEL = 1024
DEPTH = 4
GRID_W = 64
CTX_LEN = 256
EPS = 1e-6

HG_WIDTH = 512
HG_HEADS = 4
HG_D = 128
HG_SUB = 16
HG_PACK = 4
HG_SAFE_DECAY = 80.0
SG_WIDTH = 512
SG_CHUNK = 128
SG_GROUPS = 4
EVEN_IN = 5 * HG_WIDTH + 2 * SG_WIDTH

DA_HEADS = 8
DA_DH = 64
DA_DV = 128
ROPE_BASE = 10000.0
ROPE_AXIS = DA_DH // 2

D_FF = 2816
N_EXPERTS = 8
TOP_K = 2
E_FF = 3584

LOG2E = 1.4426950408889634
LANES = 128
ROW_TILE = CTX_LEN
ATT_Q_TILE = 512
ATT_KV_TILE = 1280
ATT_UNROLL = 13
ATT_MIN_SUM = 2.0 ** -80
MOE_TILE = 2048
MOE_SUB = 512
MOE_FF_TILE = 512
FFN_FF_TILE = 256
VMEM_LIMIT = 56 * 1024 * 1024


def _cparams(sem):
    return pltpu.CompilerParams(dimension_semantics=sem, vmem_limit_bytes=VMEM_LIMIT)


def _sigmoid(x):
    return 1.0 / (1.0 + jnp.exp(-x))


def _silu(x):
    return x * _sigmoid(x)


def _gelu(x):
    return 0.5 * x * (1.0 + lax.erf(x * (2.0 ** -0.5)))


def _modulated(x, sh, sc):
    ms = jnp.mean(x * x, axis=-1, keepdims=True)
    return x * lax.rsqrt(ms + EPS) * (1.0 + sc) + sh


def _mod_spec(width=D_MODEL):
    return pl.BlockSpec((None, 1, width), lambda i, *_: (jnp.minimum(i, 1), 0, 0))


def _ada_kernel(c_ref, w_ref, b_ref, o_ref):
    s = _silu(c_ref[...])
    w = w_ref[...]
    r0 = jnp.sum(s[:, 0:1] * w, axis=0, keepdims=True)
    r1 = jnp.sum(s[:, 1:2] * w, axis=0, keepdims=True)
    o_ref[0:1, :] = r0 + b_ref[...]
    o_ref[1:2, :] = r1 + b_ref[...]


def _ada_all(cc_t, mod_w, mod_b):
    n = 6 * D_MODEL
    tn = 1536
    return pl.pallas_call(
        _ada_kernel,
        grid=(DEPTH, n // tn),
        in_specs=[
            pl.BlockSpec((D_MODEL, 2), lambda l, j: (0, 0)),
            pl.BlockSpec((None, D_MODEL, tn), lambda l, j: (l, 0, j)),
            pl.BlockSpec((None, 1, tn), lambda l, j: (l, 0, j)),
        ],
        out_specs=pl.BlockSpec((None, 2, tn), lambda l, j: (l, 0, j)),
        out_shape=jax.ShapeDtypeStruct((DEPTH, 2, n), F32),
        compiler_params=_cparams(("arbitrary", "arbitrary")),
        name="ada",
    )(cc_t, mod_w, mod_b.reshape(DEPTH, 1, n))


def _even_in_kernel(x_ref, sh_ref, sc_ref, w_ref, o_ref):
    h = _modulated(x_ref[...], sh_ref[...], sc_ref[...]).astype(BF16)
    for n in range(0, EVEN_IN, 512):
        o_ref[:, n:n + 512] = jnp.dot(h, w_ref[:, n:n + 512], preferred_element_type=F32)


def _even_in(x, sh, sc, w):
    t = x.shape[0]
    return pl.pallas_call(
        _even_in_kernel,
        grid=(t // ROW_TILE,),
        in_specs=[
            pl.BlockSpec((ROW_TILE, D_MODEL), lambda i: (i, 0)),
            _mod_spec(), _mod_spec(),
            pl.BlockSpec((D_MODEL, EVEN_IN), lambda i: (0, 0)),
        ],
        out_specs=pl.BlockSpec((ROW_TILE, EVEN_IN), lambda i: (i, 0)),
        out_shape=jax.ShapeDtypeStruct((t, EVEN_IN), F32),
        compiler_params=_cparams(("arbitrary",)),
        name="even_in",
    )(x, sh, sc, w)


def _group_scan(x, r_in_group, forward):
    rows = x.shape[0]
    for sh in (1, 2, 4, 8):
        if forward:
            rolled = pltpu.roll(x, sh, 0)
            ok = r_in_group >= sh
        else:
            rolled = pltpu.roll(x, rows - sh, 0)
            ok = r_in_group < HG_SUB - sh
        x = x + jnp.where(ok, rolled, 0.0)
    return x


def _hgrn_head(q, f, v, lb, st_ref, acc_ref, rev):
    rows = q.shape[0]
    n_groups = rows // HG_SUB
    t_abs = jnp.exp(-jnp.abs(f))
    big = 1.0 / (1.0 + t_abs)
    small = t_abs * big
    log_f = jnp.log(lb + (1.0 - lb) * jnp.where(f >= 0, big, small))
    kk = (1.0 - lb) * jnp.where(f >= 0, small, big)
    qs = _silu(q)

    row = lax.broadcasted_iota(jnp.int32, (rows, HG_D), 0)
    r16 = row & (HG_SUB - 1)
    grp = row // HG_SUB
    pfx = _group_scan(log_f, r16, True)
    sfx = _group_scan(log_f, r16, False)
    cum, other = (sfx, pfx) if rev else (pfx, sfx)
    qd = (qs * jnp.exp(cum)).astype(BF16)
    kd = (kk * jnp.exp(other - log_f)).astype(BF16)

    kn = (kk * jnp.exp(-cum)).astype(BF16)
    sc = lax.dot_general(qd, kn, (((1,), (1,)), ((), ())), preferred_element_type=F32)
    ti = lax.broadcasted_iota(jnp.int32, (rows, rows), 0)
    si = lax.broadcasted_iota(jnp.int32, (rows, rows), 1)
    reach = ti & (HG_SUB - 1)
    dist, reach = (si - ti, HG_SUB - 1 - reach) if rev else (ti - si, reach)
    ok = dist.astype(jnp.uint32) <= reach.astype(jnp.uint32)
    acc_factored = jnp.dot(jnp.where(ok, sc, 0.0).astype(BF16), v.astype(BF16),
                           preferred_element_type=F32)

    v_t = v.T.astype(BF16)
    bd = jnp.concatenate(
        [jnp.where(grp == g, kd, jnp.zeros_like(kd)) for g in range(n_groups)], axis=1)
    u = jnp.dot(v_t, bd, preferred_element_type=F32)

    st = st_ref[...]
    outs = [None] * n_groups
    for g in (range(n_groups - 1, -1, -1) if rev else range(n_groups)):
        lo = g * HG_SUB
        outs[g] = lax.dot_general(qd[lo:lo + HG_SUB], st.astype(BF16),
                                  (((1,), (1,)), ((), ())), preferred_element_type=F32)
        edge = lo if rev else lo + HG_SUB - 1
        st = st * jnp.exp(cum[edge:edge + 1, :]) + u[:, g * HG_D:(g + 1) * HG_D]
    st_ref[...] = st
    o_inter = jnp.concatenate(outs, axis=0)
    acc_ref[...] = acc_factored + o_inter

    def exact():
        ones = jnp.ones((HG_D, HG_D), BF16)
        acc = o_inter
        for d in range(HG_SUB):
            if d == 0:
                kk_s, cum_s, v_s = kk, cum, v
                p = qs * kk_s
            else:
                shift = rows - d if rev else d
                kk_s = pltpu.roll(kk, shift, 0)
                cum_s = pltpu.roll(cum, shift, 0)
                v_s = pltpu.roll(v, shift, 0)
                ok_d = (r16 < HG_SUB - d) if rev else (r16 >= d)
                p = jnp.where(ok_d, qs * kk_s * jnp.exp(cum - cum_s), 0.0)
            rsum = jnp.dot(p.astype(BF16), ones, preferred_element_type=F32)
            acc = acc + rsum * v_s
        acc_ref[...] = acc

    return jnp.logical_not(jnp.min(cum) >= -HG_SAFE_DECAY), exact


def _hgrn_kernel(*refs, rev, readout):
    if readout:
        q_ref, f_ref, i_ref, lb_ref, of_ref, g_ref, gain_ref, o_ref, st_ref, acc_ref = refs
    else:
        q_ref, f_ref, i_ref, lb_ref, o_ref, st_ref, acc_ref = refs

    @pl.when(pl.program_id(1) == 0)
    def _():
        st_ref[...] = jnp.zeros_like(st_ref)

    lanes = [slice(h * HG_D, (h + 1) * HG_D) for h in range(HG_PACK)]
    redo = [_hgrn_head(q_ref[:, sl], f_ref[:, sl], i_ref[:, sl], lb_ref[h], st_ref.at[h],
                       acc_ref.at[h], rev) for h, sl in enumerate(lanes)]
    for needs_exact, exact in redo:
        pl.when(needs_exact)(exact)

    for h, sl in enumerate(lanes):
        o = acc_ref[h]
        if readout:
            o = o + of_ref[:, sl]
            ms = jnp.mean(o * o, axis=-1, keepdims=True)
            o = o * lax.rsqrt(ms + EPS) * gain_ref[...] * _silu(g_ref[:, sl])
        o_ref[:, sl] = o.astype(o_ref.dtype)


def _hgrn(y, lb, rev, o_fwd=None, gain=None):
    t = y.shape[0]
    nblk = t // ROW_TILE
    width = HG_PACK * HG_D
    per_slab = HG_WIDTH // width
    readout = o_fwd is not None
    if rev:
        blk = lambda c: jnp.where(c == 0, 0, nblk - c)
    else:
        blk = lambda c: c
    col = lambda slab: pl.BlockSpec((ROW_TILE, width), lambda h, c: (blk(c), slab * per_slab + h))
    in_specs = [col(0), col(2 if rev else 1), col(3),
                pl.BlockSpec((HG_PACK, 1, HG_D), lambda h, c: (h, 0, 0))]
    args = [y, y, y, lb.reshape(HG_HEADS, 1, HG_D)]
    if readout:
        in_specs += [pl.BlockSpec((ROW_TILE, width), lambda h, c: (blk(c), h)), col(4),
                     pl.BlockSpec((1, HG_D), lambda h, c: (0, 0))]
        args += [o_fwd, y, gain.reshape(1, HG_D)]
    return pl.pallas_call(
        functools.partial(_hgrn_kernel, rev=rev, readout=readout),
        grid=(per_slab, nblk),
        in_specs=in_specs,
        out_specs=pl.BlockSpec((ROW_TILE, width), lambda h, c: (blk(c), h)),
        out_shape=jax.ShapeDtypeStruct((t, HG_WIDTH), BF16 if readout else F32),
        scratch_shapes=[pltpu.VMEM((HG_PACK, HG_D, HG_D), F32),
                        pltpu.VMEM((HG_PACK, ROW_TILE, HG_D), F32)],
        compiler_params=_cparams(("arbitrary", "arbitrary")),
        name="hgrn_bwd" if rev else "hgrn_fwd",
    )(*args)


def _even_tail_kernel(a_ref, u_ref, v_ref, lng_ref, lnb_ref, sgw_ref, sgb_ref, w_ref, x_ref,
                      gate1_ref, sh_ref, sc_ref, gate2_ref, wg_ref, wu_ref, wd_ref, o_ref):
    u = _gelu(u_ref[...])
    v = _gelu(v_ref[...])
    mu = jnp.mean(v, axis=-1, keepdims=True)
    var = jnp.mean(jnp.square(v - mu), axis=-1, keepdims=True)
    vn = ((v - mu) * lax.rsqrt(var + 1e-5) * lng_ref[...] + lnb_ref[...]).astype(BF16)
    rows = u.shape[0]
    mixed_rows = []
    for c in range(rows // SG_CHUNK):
        lo = c * SG_CHUNK
        parts = []
        for g in range(SG_GROUPS):
            m = jnp.dot(sgw_ref[g], vn[lo:lo + SG_CHUNK, g * LANES:(g + 1) * LANES],
                        preferred_element_type=F32)
            parts.append(m + sgb_ref[:, g:g + 1])
        mixed_rows.append(jnp.concatenate(parts, axis=1))
    s = (u * jnp.concatenate(mixed_rows, axis=0)).astype(BF16)
    y = jnp.dot(jnp.concatenate([a_ref[...], s], axis=1), w_ref[...], preferred_element_type=F32)
    x = x_ref[...] + gate1_ref[...] * y

    h = _modulated(x, sh_ref[...], sc_ref[...]).astype(BF16)
    parts = []
    for n in range(0, D_FF, FFN_FF_TILE):
        g = jnp.dot(h, wg_ref[:, n:n + FFN_FF_TILE], preferred_element_type=F32)
        up = jnp.dot(h, wu_ref[:, n:n + FFN_FF_TILE], preferred_element_type=F32)
        parts.append((_silu(g) * up).astype(BF16))
    y = jnp.dot(jnp.concatenate(parts, axis=1), wd_ref[...], preferred_element_type=F32)
    o_ref[...] = x + gate2_ref[...] * y


def _even_tail(a, y_proj, ln_g, ln_b, sg_w, sg_b_t, w_out, x, gate1, sh, sc, gate2, wg, wu, wd):
    t = x.shape[0]
    full = lambda shape: pl.BlockSpec(shape, lambda i: (0,) * len(shape))
    return pl.pallas_call(
        _even_tail_kernel,
        grid=(t // ROW_TILE,),
        in_specs=[
            pl.BlockSpec((ROW_TILE, HG_WIDTH), lambda i: (i, 0)),
            pl.BlockSpec((ROW_TILE, SG_WIDTH), lambda i: (i, 5)),
            pl.BlockSpec((ROW_TILE, SG_WIDTH), lambda i: (i, 6)),
            full((1, SG_WIDTH)), full((1, SG_WIDTH)),
            full((SG_GROUPS, SG_CHUNK, SG_CHUNK)), full((SG_CHUNK, SG_GROUPS)),
            full((D_MODEL, D_MODEL)),
            pl.BlockSpec((ROW_TILE, D_MODEL), lambda i: (i, 0)),
            _mod_spec(), _mod_spec(), _mod_spec(), _mod_spec(),
            full((D_MODEL, D_FF)), full((D_MODEL, D_FF)), full((D_FF, D_MODEL)),
        ],
        out_specs=pl.BlockSpec((ROW_TILE, D_MODEL), lambda i: (i, 0)),
        out_shape=jax.ShapeDtypeStruct((t, D_MODEL), F32),
        compiler_params=_cparams(("arbitrary",)),
        name="even_tail",
    )(a, y_proj, y_proj, ln_g.reshape(1, -1), ln_b.reshape(1, -1), sg_w, sg_b_t, w_out, x, gate1,
      sh, sc, gate2, wg, wu, wd)


def _qkv_kernel(x_ref, sh_ref, sc_ref, w_ref, rt_ref, ct_ref, q_ref, k_ref, vt_ref):
    h = _modulated(x_ref[...], sh_ref[...], sc_ref[...]).astype(BF16)
    grid_rows = ROW_TILE // GRID_W

    def table(i):
        by_row = jnp.concatenate(
            [rt_ref[i, r] for r in range(grid_rows) for _ in range(GRID_W // 8)], axis=0)
        return by_row + jnp.concatenate([ct_ref[i]] * grid_rows, axis=0)

    cos, sa, sb = table(0), table(1), table(2)

    def rope(z):
        return z * cos + pltpu.roll(z, LANES - 16, 1) * sa + pltpu.roll(z, 16, 1) * sb

    for n in range(0, D_MODEL, 512):
        q = jnp.dot(h, w_ref[:, n:n + 512], preferred_element_type=F32)
        k = jnp.dot(h, w_ref[:, D_MODEL + n:D_MODEL + n + 512], preferred_element_type=F32)
        for m in range(0, 512, LANES):
            q_ref[:, n + m:n + m + LANES] = (
                rope(q[:, m:m + LANES]) * (DA_DH ** -0.5 * LOG2E)).astype(BF16)
            k_ref[:, n + m:n + m + LANES] = rope(k[:, m:m + LANES]).astype(BF16)
        v = jnp.dot(h, w_ref[:, 2 * D_MODEL + n:2 * D_MODEL + n + 512], preferred_element_type=F32)
        vt_ref[n:n + 512, :] = v.T.astype(BF16)


def _qkv(x, sh, sc, w, row_tab, col_tab):
    t = x.shape[0]
    grid_rows = ROW_TILE // GRID_W
    row = lambda width: pl.BlockSpec((ROW_TILE, width), lambda i: (i, 0))
    return pl.pallas_call(
        _qkv_kernel,
        grid=(t // ROW_TILE,),
        in_specs=[row(D_MODEL), _mod_spec(), _mod_spec(),
                  pl.BlockSpec((D_MODEL, 3 * D_MODEL), lambda i: (0, 0)),
                  pl.BlockSpec((3, grid_rows, 8, LANES), lambda i: (0, i, 0, 0)),
                  pl.BlockSpec((None, 3, GRID_W, LANES), lambda i: (jnp.minimum(i, 1), 0, 0, 0))],
        out_specs=[row(D_MODEL), row(D_MODEL), pl.BlockSpec((D_MODEL, ROW_TILE), lambda i: (0, i))],
        out_shape=[jax.ShapeDtypeStruct((t, D_MODEL), BF16), jax.ShapeDtypeStruct((t, D_MODEL), BF16),
                   jax.ShapeDtypeStruct((D_MODEL, t), BF16)],
        compiler_params=_cparams(("arbitrary",)),
        name="qkv",
    )(x, sh, sc, w, row_tab, col_tab)


def _attn_kernel(q_ref, k_ref, vt_ref, lam_ref, g_ref, o_ref, m_ref, l_ref, acc_ref, kmax_ref, *,
                 lam_init, n_keys):
    tq = q_ref.shape[0]
    q = q_ref[...]
    lane = lax.broadcasted_iota(jnp.int32, q.shape, 1)
    zero = jnp.zeros_like(q)
    qq = jnp.concatenate([jnp.where(lane < DA_DH, q, zero), jnp.where(lane >= DA_DH, q, zero)], axis=0)

    tk = min(ATT_KV_TILE, n_keys)
    n_chunks = n_keys // tk
    ones8 = jnp.ones((8, DA_DV), BF16)
    nt = (((1,), (1,)), ((), ()))

    @pl.when(pl.program_id(1) == 0)
    def _():
        lane8 = lax.broadcasted_iota(jnp.int32, (8, DA_DV), 1)
        row8 = lax.broadcasted_iota(jnp.int32, (8, DA_DV), 0)
        sel = jnp.where((lane8 >= DA_DH) == (row8 == 1), 1.0, 0.0).astype(BF16)

        def body(c, best):
            kc = k_ref[pl.ds(pl.multiple_of(c * tk, tk), tk), :].astype(F32)
            n2 = lax.dot_general(sel, (kc * kc).astype(BF16), nt, preferred_element_type=F32)
            return jnp.maximum(best, n2)
        best = lax.fori_loop(0, n_chunks, body, jnp.zeros((8, tk), F32))
        kmax_ref[...] = jnp.broadcast_to(jnp.max(best, axis=1, keepdims=True), kmax_ref.shape)

    qf = qq.astype(F32)
    qn2 = lax.dot_general(ones8, (qf * qf).astype(BF16), nt, preferred_element_type=F32)[0:1]
    col = lax.broadcasted_iota(jnp.int32, (1, 2 * tq), 1)
    kn2 = jnp.where(col < tq, kmax_ref[0:1, 0:1], kmax_ref[1:2, 0:1])
    m_bound = jnp.sqrt(qn2 * kn2)

    def scores(off, size):
        k = k_ref[pl.ds(off, size), :]
        return lax.dot_general(k, qq, nt, preferred_element_type=F32)

    def fast_chunk(off, size):
        p = jnp.exp2(scores(off, size) - m_bound)
        l_ref[...] += jnp.sum(p, axis=0, keepdims=True)
        acc_ref[...] += jnp.dot(vt_ref[:, pl.ds(off, size)], p.astype(BF16), preferred_element_type=F32)

    def online_chunk(off, size):
        s = scores(off, size)
        m_old = m_ref[...]
        m_new = jnp.maximum(m_old, jnp.max(s, axis=0, keepdims=True))
        alpha = jnp.exp2(m_old - m_new)
        p = jnp.exp2(s - m_new)
        l_ref[...] = alpha * l_ref[...] + jnp.sum(p, axis=0, keepdims=True)
        acc_ref[...] = alpha * acc_ref[...] + jnp.dot(vt_ref[:, pl.ds(off, size)], p.astype(BF16),
                                                      preferred_element_type=F32)
        m_ref[...] = m_new

    def all_keys(chunk, unroll):
        def body(jj, carry):
            for u in range(unroll):
                chunk(pl.multiple_of((unroll * jj + u) * tk, tk), tk)
            return carry
        if n_chunks // unroll > 0:
            lax.fori_loop(0, n_chunks // unroll, body, 0)
        for c in range(n_chunks - n_chunks % unroll, n_chunks):
            chunk(c * tk, tk)

    l_ref[...] = jnp.zeros_like(l_ref)
    acc_ref[...] = jnp.zeros_like(acc_ref)
    all_keys(fast_chunk, ATT_UNROLL)

    @pl.when(jnp.logical_not(jnp.min(l_ref[...]) >= ATT_MIN_SUM))
    def _():
        m_ref[...] = jnp.full_like(m_ref, -jnp.inf)
        l_ref[...] = jnp.zeros_like(l_ref)
        acc_ref[...] = jnp.zeros_like(acc_ref)
        all_keys(online_chunk, 1)

    lam = lam_ref[...]
    lam_full = (jnp.exp(jnp.sum(lam[0:1] * lam[1:2], axis=-1, keepdims=True))
                - jnp.exp(jnp.sum(lam[2:3] * lam[3:4], axis=-1, keepdims=True)) + lam_init)
    o_all = acc_ref[...] / l_ref[...]
    o = (o_all[:, 0:tq] - lam_full * o_all[:, tq:2 * tq]).T
    ms = jnp.mean(o * o, axis=-1, keepdims=True)
    o = o * lax.rsqrt(ms + 1e-5) * g_ref[...] * (1.0 - lam_init)
    o_ref[...] = o.astype(o_ref.dtype)


def _attention(q, k, vt, lam, subln_g, lam_init, tq):
    n_q, n_keys = q.shape[0], k.shape[0]
    assert n_q % tq == 0 and n_keys % min(ATT_KV_TILE, n_keys) == 0
    stat = pltpu.VMEM((1, 2 * tq), F32)
    return pl.pallas_call(
        functools.partial(_attn_kernel, lam_init=lam_init, n_keys=n_keys),
        grid=(DA_HEADS, n_q // tq),
        in_specs=[
            pl.BlockSpec((tq, DA_DV), lambda h, i: (i, h)),
            pl.BlockSpec((n_keys, DA_DV), lambda h, i: (0, h)),
            pl.BlockSpec((DA_DV, n_keys), lambda h, i: (h, 0)),
            pl.BlockSpec((4, DA_DH), lambda h, i: (0, 0)),
            pl.BlockSpec((1, DA_DV), lambda h, i: (0, 0)),
        ],
        out_specs=pl.BlockSpec((tq, DA_DV), lambda h, i: (i, h)),
        out_shape=jax.ShapeDtypeStruct((n_q, D_MODEL), BF16),
        scratch_shapes=[stat, stat, pltpu.VMEM((DA_DV, 2 * tq), F32), pltpu.VMEM((8, LANES), F32)],
        compiler_params=_cparams(("arbitrary", "arbitrary")),
        name="attention",
    )(q, k, vt, lam, subln_g.reshape(1, DA_DV))


def _proj_res_kernel(a_ref, w_ref, x_ref, gate_ref, o_ref):
    y = jnp.dot(a_ref[...], w_ref[...], preferred_element_type=F32)
    o_ref[...] = x_ref[...] + gate_ref[...] * y


def _proj_res(a, w, x, gate):
    t = x.shape[0]
    row = pl.BlockSpec((ROW_TILE, D_MODEL), lambda i: (i, 0))
    return pl.pallas_call(
        _proj_res_kernel,
        grid=(t // ROW_TILE,),
        in_specs=[row, pl.BlockSpec((D_MODEL, D_MODEL), lambda i: (0, 0)), row, _mod_spec()],
        out_specs=row,
        out_shape=jax.ShapeDtypeStruct((t, D_MODEL), F32),
        compiler_params=_cparams(("arbitrary",)),
        name="proj_res",
    )(a, w, x, gate)


HALF = D_MODEL // 2


def _pack_bf16_pairs(h):
    bits = pltpu.bitcast(h.astype(BF16).astype(F32), jnp.uint32)
    return (bits[:, HALF:] & jnp.uint32(0xFFFF0000)) | (bits[:, :HALF] >> 16)


def _unpack_bf16_pairs(w):
    lo = pltpu.bitcast(w << 16, F32)
    hi = pltpu.bitcast(w & jnp.uint32(0xFFFF0000), F32)
    return jnp.concatenate([lo, hi], axis=1).astype(BF16)


def _moe_in_kernel(x_ref, sh_ref, sc_ref, wr_ref, h_ref, info_ref, w_ref, cnt_ref, run_ref):
    @pl.when(pl.program_id(0) == 0)
    def _():
        run_ref[...] = jnp.zeros_like(run_ref)

    h = _modulated(x_ref[...], sh_ref[...], sc_ref[...])
    h_ref[...] = _pack_bf16_pairs(h)
    logits = jnp.dot(h, wr_ref[...], preferred_element_type=F32, precision=lax.Precision.HIGHEST)
    rows = logits.shape[0]
    lane = lax.broadcasted_iota(jnp.int32, logits.shape, 1)
    masked = jnp.where(lane < N_EXPERTS, logits, -jnp.inf)

    def top(l):
        m = jnp.max(l, axis=-1, keepdims=True)
        return m, jnp.min(jnp.where(l == m, lane, LANES), axis=-1, keepdims=True)

    m1, e1 = top(masked)
    m2, e2 = top(jnp.where(lane == e1, -jnp.inf, masked))
    ratio = jnp.exp(m2 - m1)
    w1 = 1.0 / (1.0 + ratio)
    w_ref[...] = jnp.where(lane == 0, w1, jnp.where(lane == 1, ratio * w1, 0.0))

    hit1 = lane == e1
    hit2 = lane == e2
    hits = jnp.where(hit1, 1.0, 0.0) + jnp.where(hit2, 1.0, 0.0)
    earlier = (lax.broadcasted_iota(jnp.int32, (rows, rows), 1)
               < lax.broadcasted_iota(jnp.int32, (rows, rows), 0))
    before = run_ref[...] + jnp.dot(jnp.where(earlier, 1.0, 0.0).astype(BF16), hits.astype(BF16),
                                    preferred_element_type=F32)
    r1 = jnp.sum(jnp.where(hit1, before, 0.0), axis=-1, keepdims=True).astype(jnp.int32)
    r2 = jnp.sum(jnp.where(hit2, before, 0.0), axis=-1, keepdims=True).astype(jnp.int32)
    info_ref[...] = jnp.where(lane == 0, e1, jnp.where(lane == 1, e2, jnp.where(
        lane == 2, r1, jnp.where(lane == 3, r2, 0))))
    run_ref[...] += jnp.sum(hits, axis=0, keepdims=True)
    cnt_ref[...] = run_ref[...]


def _moe_in(x, sh, sc, w_router_padded):
    t = x.shape[0]
    row = lambda width: pl.BlockSpec((ROW_TILE, width), lambda i: (i, 0))
    return pl.pallas_call(
        _moe_in_kernel,
        grid=(t // ROW_TILE,),
        in_specs=[row(D_MODEL), _mod_spec(), _mod_spec(),
                  pl.BlockSpec((D_MODEL, LANES), lambda i: (0, 0))],
        out_specs=[row(HALF), row(LANES), row(LANES), pl.BlockSpec((1, LANES), lambda i: (0, 0))],
        out_shape=[jax.ShapeDtypeStruct((t, HALF), jnp.uint32), jax.ShapeDtypeStruct((t, LANES), jnp.int32),
                   jax.ShapeDtypeStruct((t, LANES), F32), jax.ShapeDtypeStruct((1, LANES), F32)],
        scratch_shapes=[pltpu.VMEM((1, LANES), F32)],
        compiler_params=_cparams(("arbitrary",)),
        name="moe_in",
    )(x, sh, sc, w_router_padded)


def _dispatch_kernel(tok_ref, h_ref, o_ref):
    def body(g, carry):
        base = pl.multiple_of(g * 8, 8)
        rows = [h_ref[pl.ds(tok_ref[0, 0, base + u], 1), :] for u in range(8)]
        o_ref[pl.ds(base, 8), :] = jnp.concatenate(rows, axis=0)
        return carry
    lax.fori_loop(0, o_ref.shape[0] // 8, body, 0)


def _dispatch(h_packed, row_tok):
    n_rows = row_tok.shape[0]
    n_blocks = n_rows // MOE_TILE
    return pl.pallas_call(
        _dispatch_kernel,
        grid=(n_blocks,),
        in_specs=[
            pl.BlockSpec((1, 1, MOE_TILE), lambda b: (b, 0, 0), memory_space=pltpu.SMEM),
            pl.BlockSpec(memory_space=pltpu.VMEM),
        ],
        out_specs=pl.BlockSpec((MOE_TILE, HALF), lambda b: (b, 0)),
        out_shape=jax.ShapeDtypeStruct((n_rows, HALF), jnp.uint32),
        compiler_params=_cparams(("arbitrary",)),
        name="moe_dispatch",
    )(row_tok.reshape(n_blocks, 1, MOE_TILE), h_packed)


def _experts_kernel(blk_e_ref, n_used_ref, n_sub_ref, x_ref, wg_ref, wu_ref, wd_ref, o_ref, xb_ref):
    b = pl.program_id(0)
    f = pl.program_id(1)

    @pl.when(f == 0)
    def _():
        o_ref[...] = jnp.zeros_like(o_ref)
        xb_ref[...] = _unpack_bf16_pairs(x_ref[...])

    for s in range(MOE_TILE // MOE_SUB):
        @pl.when(s < n_sub_ref[b])
        def _():
            rows = pl.ds(s * MOE_SUB, MOE_SUB)
            x = xb_ref[rows, :]
            g = jnp.dot(x, wg_ref[...].astype(BF16), preferred_element_type=F32)
            u = jnp.dot(x, wu_ref[...].astype(BF16), preferred_element_type=F32)
            a = (_silu(g) * u).astype(BF16)
            o_ref[rows, :] += jnp.dot(a, wd_ref[...].astype(BF16), preferred_element_type=F32)


def _experts(xg, blk_e, n_used, n_sub, wg, wu, wd, j):
    n_rows = xg.shape[0]
    n_blocks = n_rows // MOE_TILE
    n_ff = E_FF // MOE_FF_TILE
    ff = lambda b, f, nu: jnp.where(b < nu[0], f, n_ff - 1)
    grid_spec = pltpu.PrefetchScalarGridSpec(
        num_scalar_prefetch=3,
        grid=(n_blocks, n_ff),
        in_specs=[
            pl.BlockSpec((MOE_TILE, HALF), lambda b, f, be, nu, ns: (b, 0)),
            pl.BlockSpec((None, None, D_MODEL, MOE_FF_TILE),
                         lambda b, f, be, nu, ns: (j, be[b], 0, ff(b, f, nu))),
            pl.BlockSpec((None, None, D_MODEL, MOE_FF_TILE),
                         lambda b, f, be, nu, ns: (j, be[b], 0, ff(b, f, nu))),
            pl.BlockSpec((None, None, MOE_FF_TILE, D_MODEL),
                         lambda b, f, be, nu, ns: (j, be[b], ff(b, f, nu), 0)),
        ],
        out_specs=pl.BlockSpec((MOE_TILE, D_MODEL), lambda b, f, be, nu, ns: (b, 0)),
        scratch_shapes=[pltpu.VMEM((MOE_TILE, D_MODEL), BF16)],
    )
    return pl.pallas_call(
        _experts_kernel,
        grid_spec=grid_spec,
        out_shape=jax.ShapeDtypeStruct((n_rows, D_MODEL), F32),
        compiler_params=_cparams(("arbitrary", "arbitrary")),
        name="experts",
    )(blk_e, n_used, n_sub, xg, wg, wu, wd)


def _combine_kernel(x_ref, gate_ref, y0_ref, y1_ref, w_ref, *rest):
    o_ref = rest[-1]
    w = w_ref[...]
    y = w[:, 0:1] * y0_ref[...] + w[:, 1:2] * y1_ref[...]
    x = x_ref[...] + gate_ref[...] * y
    if len(rest) == 2:
        ms = jnp.mean(x * x, axis=-1, keepdims=True)
        x = x * lax.rsqrt(ms + EPS) * rest[0][...]
    o_ref[...] = x


def _combine(x, gate, y0, y1, w, final_g=None):
    t = x.shape[0]
    skip = 0 if final_g is None else CTX_LEN // ROW_TILE
    row = pl.BlockSpec((ROW_TILE, D_MODEL), lambda i: (i + skip, 0))
    in_specs = [row, pl.BlockSpec((None, 1, D_MODEL), lambda i: (jnp.minimum(i + skip, 1), 0, 0)),
                row, row, pl.BlockSpec((ROW_TILE, LANES), lambda i: (i + skip, 0))]
    args = [x, gate, y0, y1, w]
    if final_g is not None:
        in_specs.append(pl.BlockSpec((1, D_MODEL), lambda i: (0, 0)))
        args.append(final_g.reshape(1, D_MODEL))
    return pl.pallas_call(
        _combine_kernel,
        grid=(t // ROW_TILE - skip,),
        in_specs=in_specs,
        out_specs=pl.BlockSpec((ROW_TILE, D_MODEL), lambda i: (i, 0)),
        out_shape=jax.ShapeDtypeStruct((t - skip * ROW_TILE, D_MODEL), F32),
        compiler_params=_cparams(("arbitrary",)),
        name="moe_combine",
    )(*args)


def _moe(x, sh, sc, gate, w_router, wg, wu, wd, j, final_g=None):
    t = x.shape[0]
    wr = jnp.zeros((D_MODEL, LANES), F32).at[:, :N_EXPERTS].set(w_router)
    h, info, top_w, counts = _moe_in(x, sh, sc, wr)
    counts = counts[0, :N_EXPERTS].astype(jnp.int32)
    flat_e = info[:, 0:TOP_K].reshape(-1)
    rank = info[:, TOP_K:2 * TOP_K].reshape(-1)
    padded = ((counts + MOE_TILE - 1) // MOE_TILE) * MOE_TILE
    pad_end = jnp.cumsum(padded)
    pad_start = pad_end - padded
    dest = (pad_start[flat_e] + rank).astype(jnp.int32)
    n_blocks = (t * TOP_K + N_EXPERTS * (MOE_TILE - 1)) // MOE_TILE
    n_rows = n_blocks * MOE_TILE
    flat_tok = jnp.repeat(jnp.arange(t, dtype=jnp.int32), TOP_K)
    row_tok = jnp.zeros((n_rows,), jnp.int32).at[dest].set(
        flat_tok, mode="promise_in_bounds", unique_indices=True)
    blk_start = jnp.arange(n_blocks, dtype=jnp.int32) * MOE_TILE
    blk_e = jnp.minimum(jnp.sum(pad_end[None, :] <= blk_start[:, None], axis=1), N_EXPERTS - 1)
    n_used = (pad_end[-1:] // MOE_TILE).astype(jnp.int32)
    real = jnp.clip(counts[blk_e] - (blk_start - pad_start[blk_e]), 0, MOE_TILE)
    n_sub = ((real + MOE_SUB - 1) // MOE_SUB).astype(jnp.int32)
    xg = _dispatch(h, row_tok)
    out = _experts(xg, blk_e.astype(jnp.int32), n_used, n_sub, wg, wu, wd, j)
    pos = dest.reshape(t, TOP_K)
    y0 = out.at[pos[:, 0]].get(mode="promise_in_bounds")
    y1 = out.at[pos[:, 1]].get(mode="promise_in_bounds")
    return _combine(x, gate, y0, y1, top_w, final_g)


def _rope_tables(n):
    n_rows = n // GRID_W
    inv = 1.0 / (ROPE_BASE ** (jnp.arange(0, ROPE_AXIS, 2, dtype=F32) / ROPE_AXIS))
    ar = jnp.arange(n_rows, dtype=F32)[:, None] * inv
    ac = jnp.arange(GRID_W, dtype=F32)[:, None] * inv
    lane = jnp.arange(LANES)
    by_row = (lane % DA_DH) < ROPE_AXIS
    first = (lane % ROPE_AXIS) < (ROPE_AXIS // 2)
    freq = lane % (ROPE_AXIS // 2)

    def lanes(ang):
        c, s = jnp.cos(ang)[:, freq], jnp.sin(ang)[:, freq]
        return jnp.stack([c, jnp.where(first, -s, 0.0), jnp.where(first, 0.0, s)])

    ident = jnp.stack([jnp.ones((1, LANES), F32), jnp.zeros((1, LANES), F32), jnp.zeros((1, LANES), F32)])
    ctx_rows = jnp.broadcast_to(jnp.where(by_row, ident, 0.0), (3, CTX_LEN // GRID_W, LANES))
    row_tab = jnp.concatenate([ctx_rows, jnp.where(by_row, lanes(ar), 0.0)], axis=1)
    row_tab = jnp.broadcast_to(row_tab[:, :, None, :], (3, row_tab.shape[1], 8, LANES))
    ctx_cols = jnp.broadcast_to(jnp.where(by_row, 0.0, ident), (3, GRID_W, LANES))
    col_tab = jnp.stack([ctx_cols, jnp.where(by_row, 0.0, lanes(ac))])
    return row_tab, col_tab


def kernel(x, c, ctx, c_ctx, mod_w, mod_b, hg_lb, ev_w_in, hg_norm_g, sg_ln_g, sg_ln_b, sg_w, sg_b, ev_w_out, da_w_qkv, da_lam, da_subln_g, da_w_out, ffn_w_gate, ffn_w_up, ffn_w_down, moe_w_router, moe_w_gate, moe_w_up, moe_w_down, final_g):
    n = x.shape[1]
    xs = jnp.concatenate([ctx[0], x[0]], axis=0)
    cc_t = jnp.stack([c_ctx, c[0]], axis=1)
    mods = _ada_all(cc_t, mod_w, mod_b).reshape(DEPTH, 2, 6, 1, D_MODEL)
    row_tab, col_tab = _rope_tables(n)
    lb_sm = jax.nn.softmax(hg_lb.astype(F32), axis=1)
    lb_all = jnp.cumsum(lb_sm, axis=1) - lb_sm[:, :1]

    for layer in range(DEPTH):
        j = layer // 2
        sh1, sc1, g1, sh2, sc2, g2 = (mods[layer, :, i] for i in range(6))
        if layer % 2 == 0:
            y = _even_in(xs, sh1, sc1, ev_w_in[j].astype(BF16))
            o_f = _hgrn(y, lb_all[0, j], rev=False)
            a = _hgrn(y, lb_all[1, j], rev=True, o_fwd=o_f, gain=hg_norm_g[j])
            xs = _even_tail(a, y, sg_ln_g[j], sg_ln_b[j], sg_w[j].astype(BF16), sg_b[j].T,
                            ev_w_out[j].astype(BF16), xs, g1, sh2, sc2, g2,
                            ffn_w_gate[j].astype(BF16), ffn_w_up[j].astype(BF16),
                            ffn_w_down[j].astype(BF16))
        else:
            lam_init = 0.8 - 0.6 * math.exp(-0.3 * layer)
            q, k, vt = _qkv(xs, sh1, sc1, da_w_qkv[j].astype(BF16), row_tab, col_tab)
            a_ctx = _attention(q[:CTX_LEN], k[:CTX_LEN], vt[:, :CTX_LEN], da_lam[j], da_subln_g[j],
                               lam_init, CTX_LEN)
            a_lat = _attention(q[CTX_LEN:], k, vt, da_lam[j], da_subln_g[j], lam_init, ATT_Q_TILE)
            a = jnp.concatenate([a_ctx, a_lat], axis=0)
            xs = _proj_res(a, da_w_out[j].astype(BF16), xs, g1)
            xs = _moe(xs, sh2, sc2, g2, moe_w_router[j], moe_w_gate, moe_w_up, moe_w_down, j,
                      final_g if layer == DEPTH - 1 else None)
    return xs[None]
```

```python
import functools
import math

import jax
import jax.numpy as jnp
from jax import lax
from jax.experimental import pallas as pl
from jax.experimental.pallas import tpu as pltpu

F32 = jnp.float32
BF16 = jnp.bfloat16

D_MODEL = 1024
DEPTH = 4
GRID_W = 64
CTX_LEN = 256
EPS = 1e-6

HG_WIDTH = 512
HG_HEADS = 4
HG_D = 128
HG_SUB = 16
HG_PACK = 4
HG_SAFE_DECAY = 80.0
SG_WIDTH = 512
SG_CHUNK = 128
SG_GROUPS = 4
EVEN_IN = 5 * HG_WIDTH + 2 * SG_WIDTH

DA_HEADS = 8
DA_DH = 64
DA_DV = 128
ROPE_BASE = 10000.0
ROPE_AXIS = DA_DH // 2

D_FF = 2816
N_EXPERTS = 8
TOP_K = 2
E_FF = 3584

LOG2E = 1.4426950408889634
LANES = 128
ROW_TILE = CTX_LEN
WIDE_ROW_TILE = 640
ATT_Q_TILE = 1024
ATT_KV_TILE = 1280
ATT_UNROLL = 13
ATT_MIN_SUM = 2.0 ** -80
MOE_TILE = 1024
MOE_FF_TILE = 512
FFN_FF_TILE = 256
VMEM_LIMIT = 56 * 1024 * 1024


def _cparams(sem):
    return pltpu.CompilerParams(dimension_semantics=sem, vmem_limit_bytes=VMEM_LIMIT)


def _sigmoid(x):
    return 1.0 / (1.0 + jnp.exp(-x))


def _silu(x):
    return x * _sigmoid(x)


def _gelu(x):
    return 0.5 * x * (1.0 + lax.erf(x * (2.0 ** -0.5)))


def _modulated(x, sh, sc):
    ms = jnp.mean(x * x, axis=-1, keepdims=True)
    return x * lax.rsqrt(ms + EPS) * (1.0 + sc) + sh


def _mod_spec(width=D_MODEL):
    return pl.BlockSpec((None, 1, width), lambda i, *_: (jnp.minimum(i, 1), 0, 0))


def _ada_kernel(c_ref, w_ref, b_ref, o_ref):
    s = _silu(c_ref[...])
    w = w_ref[...]
    r0 = jnp.sum(s[:, 0:1] * w, axis=0, keepdims=True)
    r1 = jnp.sum(s[:, 1:2] * w, axis=0, keepdims=True)
    o_ref[0:1, :] = r0 + b_ref[...]
    o_ref[1:2, :] = r1 + b_ref[...]


def _ada_all(cc_t, mod_w, mod_b):
    n = 6 * D_MODEL
    tn = 1536
    return pl.pallas_call(
        _ada_kernel,
        grid=(DEPTH, n // tn),
        in_specs=[
            pl.BlockSpec((D_MODEL, 2), lambda l, j: (0, 0)),
            pl.BlockSpec((None, D_MODEL, tn), lambda l, j: (l, 0, j)),
            pl.BlockSpec((None, 1, tn), lambda l, j: (l, 0, j)),
        ],
        out_specs=pl.BlockSpec((None, 2, tn), lambda l, j: (l, 0, j)),
        out_shape=jax.ShapeDtypeStruct((DEPTH, 2, n), F32),
        compiler_params=_cparams(("arbitrary", "arbitrary")),
        name="ada",
    )(cc_t, mod_w, mod_b.reshape(DEPTH, 1, n))


def _even_in_kernel(x_ref, sh_ref, sc_ref, w_ref, o_ref):
    x = x_ref[...]
    row = pl.program_id(0) * x.shape[0] + lax.broadcasted_iota(jnp.int32, (x.shape[0], 1), 0)
    is_ctx = row < CTX_LEN
    sh = jnp.where(is_ctx, sh_ref[0], sh_ref[1])
    sc = jnp.where(is_ctx, sc_ref[0], sc_ref[1])
    h = _modulated(x, sh, sc).astype(BF16)
    for n in range(0, EVEN_IN, 512):
        o_ref[:, n:n + 512] = jnp.dot(h, w_ref[:, n:n + 512], preferred_element_type=F32)


def _even_in(x, sh, sc, w):
    t = x.shape[0]
    both = pl.BlockSpec((2, 1, D_MODEL), lambda i: (0, 0, 0))
    return pl.pallas_call(
        _even_in_kernel,
        grid=(t // WIDE_ROW_TILE,),
        in_specs=[
            pl.BlockSpec((WIDE_ROW_TILE, D_MODEL), lambda i: (i, 0)),
            both, both,
            pl.BlockSpec((D_MODEL, EVEN_IN), lambda i: (0, 0)),
        ],
        out_specs=pl.BlockSpec((WIDE_ROW_TILE, EVEN_IN), lambda i: (i, 0)),
        out_shape=jax.ShapeDtypeStruct((t, EVEN_IN), F32),
        compiler_params=_cparams(("arbitrary",)),
        name="even_in",
    )(x, sh, sc, w)


def _group_scan(x, r_in_group, forward):
    rows = x.shape[0]
    for sh in (1, 2, 4, 8):
        if forward:
            rolled = pltpu.roll(x, sh, 0)
            ok = r_in_group >= sh
        else:
            rolled = pltpu.roll(x, rows - sh, 0)
            ok = r_in_group < HG_SUB - sh
        x = x + jnp.where(ok, rolled, 0.0)
    return x


def _hgrn_head(q, f, v, lb, st_ref, acc_ref, rev):
    rows = q.shape[0]
    n_groups = rows // HG_SUB
    t_abs = jnp.exp(-jnp.abs(f))
    big = 1.0 / (1.0 + t_abs)
    small = t_abs * big
    log_f = jnp.log(lb + (1.0 - lb) * jnp.where(f >= 0, big, small))
    kk = (1.0 - lb) * jnp.where(f >= 0, small, big)
    qs = _silu(q)

    row = lax.broadcasted_iota(jnp.int32, (rows, HG_D), 0)
    r16 = row & (HG_SUB - 1)
    grp = row // HG_SUB
    pfx = _group_scan(log_f, r16, True)
    sfx = _group_scan(log_f, r16, False)
    cum, other = (sfx, pfx) if rev else (pfx, sfx)
    qd = (qs * jnp.exp(cum)).astype(BF16)
    kd = (kk * jnp.exp(other - log_f)).astype(BF16)

    kn = (kk * jnp.exp(-cum)).astype(BF16)
    sc = lax.dot_general(qd, kn, (((1,), (1,)), ((), ())), preferred_element_type=F32)
    ti = lax.broadcasted_iota(jnp.int32, (rows, rows), 0)
    si = lax.broadcasted_iota(jnp.int32, (rows, rows), 1)
    reach = ti & (HG_SUB - 1)
    dist, reach = (si - ti, HG_SUB - 1 - reach) if rev else (ti - si, reach)
    ok = dist.astype(jnp.uint32) <= reach.astype(jnp.uint32)
    acc_factored = jnp.dot(jnp.where(ok, sc, 0.0).astype(BF16), v.astype(BF16),
                           preferred_element_type=F32)

    v_t = v.T.astype(BF16)
    bd = jnp.concatenate(
        [jnp.where(grp == g, kd, jnp.zeros_like(kd)) for g in range(n_groups)], axis=1)
    u = jnp.dot(v_t, bd, preferred_element_type=F32)

    st = st_ref[...]
    outs = [None] * n_groups
    for g in (range(n_groups - 1, -1, -1) if rev else range(n_groups)):
        lo = g * HG_SUB
        outs[g] = lax.dot_general(qd[lo:lo + HG_SUB], st.astype(BF16),
                                  (((1,), (1,)), ((), ())), preferred_element_type=F32)
        edge = lo if rev else lo + HG_SUB - 1
        st = st * jnp.exp(cum[edge:edge + 1, :]) + u[:, g * HG_D:(g + 1) * HG_D]
    st_ref[...] = st
    o_inter = jnp.concatenate(outs, axis=0)
    acc_ref[...] = acc_factored + o_inter

    def exact():
        ones = jnp.ones((HG_D, HG_D), BF16)
        acc = o_inter
        for d in range(HG_SUB):
            if d == 0:
                kk_s, cum_s, v_s = kk, cum, v
                p = qs * kk_s
            else:
                shift = rows - d if rev else d
                kk_s = pltpu.roll(kk, shift, 0)
                cum_s = pltpu.roll(cum, shift, 0)
                v_s = pltpu.roll(v, shift, 0)
                ok_d = (r16 < HG_SUB - d) if rev else (r16 >= d)
                p = jnp.where(ok_d, qs * kk_s * jnp.exp(cum - cum_s), 0.0)
            rsum = jnp.dot(p.astype(BF16), ones, preferred_element_type=F32)
            acc = acc + rsum * v_s
        acc_ref[...] = acc

    return jnp.logical_not(jnp.min(cum) >= -HG_SAFE_DECAY), exact


def _hgrn_kernel(*refs, rev, readout):
    if readout:
        q_ref, f_ref, i_ref, lb_ref, of_ref, g_ref, gain_ref, o_ref, st_ref, acc_ref = refs
    else:
        q_ref, f_ref, i_ref, lb_ref, o_ref, st_ref, acc_ref = refs

    @pl.when(pl.program_id(1) == 0)
    def _():
        st_ref[...] = jnp.zeros_like(st_ref)

    lanes = [slice(h * HG_D, (h + 1) * HG_D) for h in range(HG_PACK)]
    redo = [_hgrn_head(q_ref[:, sl], f_ref[:, sl], i_ref[:, sl], lb_ref[h], st_ref.at[h],
                       acc_ref.at[h], rev) for h, sl in enumerate(lanes)]
    for needs_exact, exact in redo:
        pl.when(needs_exact)(exact)

    for h, sl in enumerate(lanes):
        o = acc_ref[h]
        if readout:
            o = o + of_ref[:, sl]
            ms = jnp.mean(o * o, axis=-1, keepdims=True)
            o = o * lax.rsqrt(ms + EPS) * gain_ref[...] * _silu(g_ref[:, sl])
        o_ref[:, sl] = o.astype(o_ref.dtype)


def _hgrn(y, lb, rev, o_fwd=None, gain=None):
    t = y.shape[0]
    nblk = t // ROW_TILE
    width = HG_PACK * HG_D
    per_slab = HG_WIDTH // width
    readout = o_fwd is not None
    if rev:
        blk = lambda c: jnp.where(c == 0, 0, nblk - c)
    else:
        blk = lambda c: c
    col = lambda slab: pl.BlockSpec((ROW_TILE, width), lambda h, c: (blk(c), slab * per_slab + h))
    in_specs = [col(0), col(2 if rev else 1), col(3),
                pl.BlockSpec((HG_PACK, 1, HG_D), lambda h, c: (h, 0, 0))]
    args = [y, y, y, lb.reshape(HG_HEADS, 1, HG_D)]
    if readout:
        in_specs += [pl.BlockSpec((ROW_TILE, width), lambda h, c: (blk(c), h)), col(4),
                     pl.BlockSpec((1, HG_D), lambda h, c: (0, 0))]
        args += [o_fwd, y, gain.reshape(1, HG_D)]
    return pl.pallas_call(
        functools.partial(_hgrn_kernel, rev=rev, readout=readout),
        grid=(per_slab, nblk),
        in_specs=in_specs,
        out_specs=pl.BlockSpec((ROW_TILE, width), lambda h, c: (blk(c), h)),
        out_shape=jax.ShapeDtypeStruct((t, HG_WIDTH), BF16 if readout else F32),
        scratch_shapes=[pltpu.VMEM((HG_PACK, HG_D, HG_D), F32),
                        pltpu.VMEM((HG_PACK, ROW_TILE, HG_D), F32)],
        compiler_params=_cparams(("arbitrary", "arbitrary")),
        name="hgrn_bwd" if rev else "hgrn_fwd",
    )(*args)


def _even_tail_kernel(a_ref, u_ref, v_ref, lng_ref, lnb_ref, sgw_ref, sgb_ref, w_ref, x_ref,
                      gate1_ref, sh_ref, sc_ref, gate2_ref, wg_ref, wu_ref, wd_ref, o_ref):
    u = _gelu(u_ref[...])
    v = _gelu(v_ref[...])
    mu = jnp.mean(v, axis=-1, keepdims=True)
    var = jnp.mean(jnp.square(v - mu), axis=-1, keepdims=True)
    vn = ((v - mu) * lax.rsqrt(var + 1e-5) * lng_ref[...] + lnb_ref[...]).astype(BF16)
    rows = u.shape[0]
    mixed_rows = []
    for c in range(rows // SG_CHUNK):
        lo = c * SG_CHUNK
        parts = []
        for g in range(SG_GROUPS):
            m = jnp.dot(sgw_ref[g], vn[lo:lo + SG_CHUNK, g * LANES:(g + 1) * LANES],
                        preferred_element_type=F32)
            parts.append(m + sgb_ref[:, g:g + 1])
        mixed_rows.append(jnp.concatenate(parts, axis=1))
    s = (u * jnp.concatenate(mixed_rows, axis=0)).astype(BF16)
    y = jnp.dot(jnp.concatenate([a_ref[...], s], axis=1), w_ref[...], preferred_element_type=F32)
    x = x_ref[...] + gate1_ref[...] * y

    h = _modulated(x, sh_ref[...], sc_ref[...]).astype(BF16)
    parts = []
    for n in range(0, D_FF, FFN_FF_TILE):
        g = jnp.dot(h, wg_ref[:, n:n + FFN_FF_TILE], preferred_element_type=F32)
        up = jnp.dot(h, wu_ref[:, n:n + FFN_FF_TILE], preferred_element_type=F32)
        parts.append((_silu(g) * up).astype(BF16))
    y = jnp.dot(jnp.concatenate(parts, axis=1), wd_ref[...], preferred_element_type=F32)
    o_ref[...] = x + gate2_ref[...] * y


def _even_tail(a, y_proj, ln_g, ln_b, sg_w, sg_b_t, w_out, x, gate1, sh, sc, gate2, wg, wu, wd):
    t = x.shape[0]
    full = lambda shape: pl.BlockSpec(shape, lambda i: (0,) * len(shape))
    return pl.pallas_call(
        _even_tail_kernel,
        grid=(t // ROW_TILE,),
        in_specs=[
            pl.BlockSpec((ROW_TILE, HG_WIDTH), lambda i: (i, 0)),
            pl.BlockSpec((ROW_TILE, SG_WIDTH), lambda i: (i, 5)),
            pl.BlockSpec((ROW_TILE, SG_WIDTH), lambda i: (i, 6)),
            full((1, SG_WIDTH)), full((1, SG_WIDTH)),
            full((SG_GROUPS, SG_CHUNK, SG_CHUNK)), full((SG_CHUNK, SG_GROUPS)),
            full((D_MODEL, D_MODEL)),
            pl.BlockSpec((ROW_TILE, D_MODEL), lambda i: (i, 0)),
            _mod_spec(), _mod_spec(), _mod_spec(), _mod_spec(),
            full((D_MODEL, D_FF)), full((D_MODEL, D_FF)), full((D_FF, D_MODEL)),
        ],
        out_specs=pl.BlockSpec((ROW_TILE, D_MODEL), lambda i: (i, 0)),
        out_shape=jax.ShapeDtypeStruct((t, D_MODEL), F32),
        compiler_params=_cparams(("arbitrary",)),
        name="even_tail",
    )(a, y_proj, y_proj, ln_g.reshape(1, -1), ln_b.reshape(1, -1), sg_w, sg_b_t, w_out, x, gate1,
      sh, sc, gate2, wg, wu, wd)


def _qkv_kernel(x_ref, sh_ref, sc_ref, w_ref, rt_ref, ct_ref, q_ref, k_ref, vt_ref):
    h = _modulated(x_ref[...], sh_ref[...], sc_ref[...]).astype(BF16)
    grid_rows = ROW_TILE // GRID_W

    def table(i):
        by_row = jnp.concatenate(
            [rt_ref[i, r] for r in range(grid_rows) for _ in range(GRID_W // 8)], axis=0)
        return by_row + jnp.concatenate([ct_ref[i]] * grid_rows, axis=0)

    cos, sa, sb = table(0), table(1), table(2)

    def rope(z):
        return z * cos + pltpu.roll(z, LANES - 16, 1) * sa + pltpu.roll(z, 16, 1) * sb

    for n in range(0, D_MODEL, 512):
        q = jnp.dot(h, w_ref[:, n:n + 512], preferred_element_type=F32)
        k = jnp.dot(h, w_ref[:, D_MODEL + n:D_MODEL + n + 512], preferred_element_type=F32)
        for m in range(0, 512, LANES):
            q_ref[:, n + m:n + m + LANES] = (
                rope(q[:, m:m + LANES]) * (DA_DH ** -0.5 * LOG2E)).astype(BF16)
            k_ref[:, n + m:n + m + LANES] = rope(k[:, m:m + LANES]).astype(BF16)
        v = jnp.dot(h, w_ref[:, 2 * D_MODEL + n:2 * D_MODEL + n + 512], preferred_element_type=F32)
        vt_ref[n:n + 512, :] = v.T.astype(BF16)


def _qkv(x, sh, sc, w, row_tab, col_tab):
    t = x.shape[0]
    grid_rows = ROW_TILE // GRID_W
    row = lambda width: pl.BlockSpec((ROW_TILE, width), lambda i: (i, 0))
    return pl.pallas_call(
        _qkv_kernel,
        grid=(t // ROW_TILE,),
        in_specs=[row(D_MODEL), _mod_spec(), _mod_spec(),
                  pl.BlockSpec((D_MODEL, 3 * D_MODEL), lambda i: (0, 0)),
                  pl.BlockSpec((3, grid_rows, 8, LANES), lambda i: (0, i, 0, 0)),
                  pl.BlockSpec((None, 3, GRID_W, LANES), lambda i: (jnp.minimum(i, 1), 0, 0, 0))],
        out_specs=[row(D_MODEL), row(D_MODEL), pl.BlockSpec((D_MODEL, ROW_TILE), lambda i: (0, i))],
        out_shape=[jax.ShapeDtypeStruct((t, D_MODEL), BF16), jax.ShapeDtypeStruct((t, D_MODEL), BF16),
                   jax.ShapeDtypeStruct((D_MODEL, t), BF16)],
        compiler_params=_cparams(("arbitrary",)),
        name="qkv",
    )(x, sh, sc, w, row_tab, col_tab)


def _attn_kernel(q_ref, k_ref, vt_ref, lam_ref, g_ref, o_ref, m_ref, l_ref, acc_ref, kmax_ref, *,
                 lam_init, n_keys):
    tq = q_ref.shape[0]
    q = q_ref[...]
    lane = lax.broadcasted_iota(jnp.int32, q.shape, 1)
    zero = jnp.zeros_like(q)
    qq = jnp.concatenate([jnp.where(lane < DA_DH, q, zero), jnp.where(lane >= DA_DH, q, zero)], axis=0)

    tk = min(ATT_KV_TILE, n_keys)
    n_chunks = n_keys // tk
    ones8 = jnp.ones((8, DA_DV), BF16)
    nt = (((1,), (1,)), ((), ()))

    @pl.when(pl.program_id(1) == 0)
    def _():
        lane8 = lax.broadcasted_iota(jnp.int32, (8, DA_DV), 1)
        row8 = lax.broadcasted_iota(jnp.int32, (8, DA_DV), 0)
        sel = jnp.where((lane8 >= DA_DH) == (row8 == 1), 1.0, 0.0).astype(BF16)

        def body(c, best):
            kc = k_ref[pl.ds(pl.multiple_of(c * tk, tk), tk), :].astype(F32)
            n2 = lax.dot_general(sel, (kc * kc).astype(BF16), nt, preferred_element_type=F32)
            return jnp.maximum(best, n2)
        best = lax.fori_loop(0, n_chunks, body, jnp.zeros((8, tk), F32))
        kmax_ref[...] = jnp.broadcast_to(jnp.max(best, axis=1, keepdims=True), kmax_ref.shape)

    qf = qq.astype(F32)
    qn2 = lax.dot_general(ones8, (qf * qf).astype(BF16), nt, preferred_element_type=F32)[0:1]
    col = lax.broadcasted_iota(jnp.int32, (1, 2 * tq), 1)
    kn2 = jnp.where(col < tq, kmax_ref[0:1, 0:1], kmax_ref[1:2, 0:1])
    m_bound = jnp.sqrt(qn2 * kn2)

    def scores(off, size):
        k = k_ref[pl.ds(off, size), :]
        return lax.dot_general(k, qq, nt, preferred_element_type=F32)

    def fast_chunk(off, size):
        p = jnp.exp2(scores(off, size) - m_bound)
        l_ref[...] += jnp.sum(p, axis=0, keepdims=True)
        acc_ref[...] += jnp.dot(vt_ref[:, pl.ds(off, size)], p.astype(BF16), preferred_element_type=F32)

    def online_chunk(off, size):
        s = scores(off, size)
        m_old = m_ref[...]
        m_new = jnp.maximum(m_old, jnp.max(s, axis=0, keepdims=True))
        alpha = jnp.exp2(m_old - m_new)
        p = jnp.exp2(s - m_new)
        l_ref[...] = alpha * l_ref[...] + jnp.sum(p, axis=0, keepdims=True)
        acc_ref[...] = alpha * acc_ref[...] + jnp.dot(vt_ref[:, pl.ds(off, size)], p.astype(BF16),
                                                      preferred_element_type=F32)
        m_ref[...] = m_new

    def all_keys(chunk, unroll):
        def body(jj, carry):
            for u in range(unroll):
                chunk(pl.multiple_of((unroll * jj + u) * tk, tk), tk)
            return carry
        if n_chunks // unroll > 0:
            lax.fori_loop(0, n_chunks // unroll, body, 0)
        for c in range(n_chunks - n_chunks % unroll, n_chunks):
            chunk(c * tk, tk)

    l_ref[...] = jnp.zeros_like(l_ref)
    acc_ref[...] = jnp.zeros_like(acc_ref)
    all_keys(fast_chunk, ATT_UNROLL)

    @pl.when(jnp.logical_not(jnp.min(l_ref[...]) >= ATT_MIN_SUM))
    def _():
        m_ref[...] = jnp.full_like(m_ref, -jnp.inf)
        l_ref[...] = jnp.zeros_like(l_ref)
        acc_ref[...] = jnp.zeros_like(acc_ref)
        all_keys(online_chunk, 1)

    lam = lam_ref[...]
    lam_full = (jnp.exp(jnp.sum(lam[0:1] * lam[1:2], axis=-1, keepdims=True))
                - jnp.exp(jnp.sum(lam[2:3] * lam[3:4], axis=-1, keepdims=True)) + lam_init)
    o_all = acc_ref[...] / l_ref[...]
    o = (o_all[:, 0:tq] - lam_full * o_all[:, tq:2 * tq]).T
    ms = jnp.mean(o * o, axis=-1, keepdims=True)
    o = o * lax.rsqrt(ms + 1e-5) * g_ref[...] * (1.0 - lam_init)
    o_ref[...] = o.astype(o_ref.dtype)


def _attention(q, k, vt, lam, subln_g, lam_init, tq):
    n_q, n_keys = q.shape[0], k.shape[0]
    assert n_q % tq == 0 and n_keys % min(ATT_KV_TILE, n_keys) == 0
    stat = pltpu.VMEM((1, 2 * tq), F32)
    return pl.pallas_call(
        functools.partial(_attn_kernel, lam_init=lam_init, n_keys=n_keys),
        grid=(DA_HEADS, n_q // tq),
        in_specs=[
            pl.BlockSpec((tq, DA_DV), lambda h, i: (i, h)),
            pl.BlockSpec((n_keys, DA_DV), lambda h, i: (0, h)),
            pl.BlockSpec((DA_DV, n_keys), lambda h, i: (h, 0)),
            pl.BlockSpec((4, DA_DH), lambda h, i: (0, 0)),
            pl.BlockSpec((1, DA_DV), lambda h, i: (0, 0)),
        ],
        out_specs=pl.BlockSpec((tq, DA_DV), lambda h, i: (i, h)),
        out_shape=jax.ShapeDtypeStruct((n_q, D_MODEL), BF16),
        scratch_shapes=[stat, stat, pltpu.VMEM((DA_DV, 2 * tq), F32), pltpu.VMEM((8, LANES), F32)],
        compiler_params=_cparams(("arbitrary", "arbitrary")),
        name="attention",
    )(q, k, vt, lam, subln_g.reshape(1, DA_DV))


def _proj_res_kernel(a_ref, w_ref, x_ref, gate_ref, o_ref):
    y = jnp.dot(a_ref[...], w_ref[...], preferred_element_type=F32)
    o_ref[...] = x_ref[...] + gate_ref[...] * y


def _proj_res(a, w, x, gate):
    t = x.shape[0]
    row = pl.BlockSpec((ROW_TILE, D_MODEL), lambda i: (i, 0))
    return pl.pallas_call(
        _proj_res_kernel,
        grid=(t // ROW_TILE,),
        in_specs=[row, pl.BlockSpec((D_MODEL, D_MODEL), lambda i: (0, 0)), row, _mod_spec()],
        out_specs=row,
        out_shape=jax.ShapeDtypeStruct((t, D_MODEL), F32),
        compiler_params=_cparams(("arbitrary",)),
        name="proj_res",
    )(a, w, x, gate)


HALF = D_MODEL // 2


def _pack_bf16_pairs(h):
    bits = pltpu.bitcast(h.astype(BF16).astype(F32), jnp.uint32)
    return (bits[:, HALF:] & jnp.uint32(0xFFFF0000)) | (bits[:, :HALF] >> 16)


def _unpack_bf16_pairs(w):
    lo = pltpu.bitcast(w << 16, F32)
    hi = pltpu.bitcast(w & jnp.uint32(0xFFFF0000), F32)
    return jnp.concatenate([lo, hi], axis=1).astype(BF16)


def _moe_in_kernel(x_ref, sh_ref, sc_ref, wr_ref, h_ref, info_ref, w_ref, cnt_ref, run_ref):
    @pl.when(pl.program_id(0) == 0)
    def _():
        run_ref[...] = jnp.zeros_like(run_ref)

    h = _modulated(x_ref[...], sh_ref[...], sc_ref[...])
    h_ref[...] = _pack_bf16_pairs(h)
    logits = jnp.dot(h, wr_ref[...], preferred_element_type=F32, precision=lax.Precision.HIGHEST)
    rows = logits.shape[0]
    lane = lax.broadcasted_iota(jnp.int32, logits.shape, 1)
    masked = jnp.where(lane < N_EXPERTS, logits, -jnp.inf)

    def top(l):
        m = jnp.max(l, axis=-1, keepdims=True)
        return m, jnp.min(jnp.where(l == m, lane, LANES), axis=-1, keepdims=True)

    m1, e1 = top(masked)
    m2, e2 = top(jnp.where(lane == e1, -jnp.inf, masked))
    ratio = jnp.exp(m2 - m1)
    w1 = 1.0 / (1.0 + ratio)
    w_ref[...] = jnp.where(lane == 0, w1, jnp.where(lane == 1, ratio * w1, 0.0))

    hit1 = lane == e1
    hit2 = lane == e2
    hits = jnp.where(hit1, 1.0, 0.0) + jnp.where(hit2, 1.0, 0.0)
    earlier = (lax.broadcasted_iota(jnp.int32, (rows, rows), 1)
               < lax.broadcasted_iota(jnp.int32, (rows, rows), 0))
    before = run_ref[...] + jnp.dot(jnp.where(earlier, 1.0, 0.0).astype(BF16), hits.astype(BF16),
                                    preferred_element_type=F32)
    r1 = jnp.sum(jnp.where(hit1, before, 0.0), axis=-1, keepdims=True).astype(jnp.int32)
    r2 = jnp.sum(jnp.where(hit2, before, 0.0), axis=-1, keepdims=True).astype(jnp.int32)
    info_ref[...] = jnp.where(lane == 0, e1, jnp.where(lane == 1, e2, jnp.where(
        lane == 2, r1, jnp.where(lane == 3, r2, 0))))
    run_ref[...] += jnp.sum(hits, axis=0, keepdims=True)
    cnt_ref[...] = run_ref[...]


def _moe_in(x, sh, sc, w_router_padded):
    t = x.shape[0]
    row = lambda width: pl.BlockSpec((ROW_TILE, width), lambda i: (i, 0))
    return pl.pallas_call(
        _moe_in_kernel,
        grid=(t // ROW_TILE,),
        in_specs=[row(D_MODEL), _mod_spec(), _mod_spec(),
                  pl.BlockSpec((D_MODEL, LANES), lambda i: (0, 0))],
        out_specs=[row(HALF), row(LANES), row(LANES), pl.BlockSpec((1, LANES), lambda i: (0, 0))],
        out_shape=[jax.ShapeDtypeStruct((t, HALF), jnp.uint32), jax.ShapeDtypeStruct((t, LANES), jnp.int32),
                   jax.ShapeDtypeStruct((t, LANES), F32), jax.ShapeDtypeStruct((1, LANES), F32)],
        scratch_shapes=[pltpu.VMEM((1, LANES), F32)],
        compiler_params=_cparams(("arbitrary",)),
        name="moe_in",
    )(x, sh, sc, w_router_padded)


def _dispatch_kernel(tok_ref, h_ref, o_ref):
    def body(g, carry):
        base = pl.multiple_of(g * 8, 8)
        rows = [h_ref[pl.ds(tok_ref[0, 0, base + u], 1), :] for u in range(8)]
        o_ref[pl.ds(base, 8), :] = jnp.concatenate(rows, axis=0)
        return carry
    lax.fori_loop(0, o_ref.shape[0] // 8, body, 0)


def _dispatch(h_packed, row_tok):
    n_rows = row_tok.shape[0]
    n_blocks = n_rows // MOE_TILE
    return pl.pallas_call(
        _dispatch_kernel,
        grid=(n_blocks,),
        in_specs=[
            pl.BlockSpec((1, 1, MOE_TILE), lambda b: (b, 0, 0), memory_space=pltpu.SMEM),
            pl.BlockSpec(memory_space=pltpu.VMEM),
        ],
        out_specs=pl.BlockSpec((MOE_TILE, HALF), lambda b: (b, 0)),
        out_shape=jax.ShapeDtypeStruct((n_rows, HALF), jnp.uint32),
        compiler_params=_cparams(("arbitrary",)),
        name="moe_dispatch",
    )(row_tok.reshape(n_blocks, 1, MOE_TILE), h_packed)


def _experts_kernel(blk_e_ref, n_used_ref, x_ref, wg_ref, wu_ref, wd_ref, o_ref, xb_ref):
    b = pl.program_id(0)
    f = pl.program_id(1)

    @pl.when(f == 0)
    def _():
        o_ref[...] = jnp.zeros_like(o_ref)
        xb_ref[...] = _unpack_bf16_pairs(x_ref[...])

    @pl.when(b < n_used_ref[0])
    def _():
        x = xb_ref[...]
        g = jnp.dot(x, wg_ref[...].astype(BF16), preferred_element_type=F32)
        u = jnp.dot(x, wu_ref[...].astype(BF16), preferred_element_type=F32)
        a = (_silu(g) * u).astype(BF16)
        o_ref[...] += jnp.dot(a, wd_ref[...].astype(BF16), preferred_element_type=F32)


def _experts(xg, blk_e, n_used, wg, wu, wd, j):
    n_rows = xg.shape[0]
    n_blocks = n_rows // MOE_TILE
    n_ff = E_FF // MOE_FF_TILE
    ff = lambda b, f, nu: jnp.where(b < nu[0], f, n_ff - 1)
    grid_spec = pltpu.PrefetchScalarGridSpec(
        num_scalar_prefetch=2,
        grid=(n_blocks, n_ff),
        in_specs=[
            pl.BlockSpec((MOE_TILE, HALF), lambda b, f, be, nu: (b, 0)),
            pl.BlockSpec((None, None, D_MODEL, MOE_FF_TILE),
                         lambda b, f, be, nu: (j, be[b], 0, ff(b, f, nu))),
            pl.BlockSpec((None, None, D_MODEL, MOE_FF_TILE),
                         lambda b, f, be, nu: (j, be[b], 0, ff(b, f, nu))),
            pl.BlockSpec((None, None, MOE_FF_TILE, D_MODEL),
                         lambda b, f, be, nu: (j, be[b], ff(b, f, nu), 0)),
        ],
        out_specs=pl.BlockSpec((MOE_TILE, D_MODEL), lambda b, f, be, nu: (b, 0)),
        scratch_shapes=[pltpu.VMEM((MOE_TILE, D_MODEL), BF16)],
    )
    return pl.pallas_call(
        _experts_kernel,
        grid_spec=grid_spec,
        out_shape=jax.ShapeDtypeStruct((n_rows, D_MODEL), F32),
        compiler_params=_cparams(("arbitrary", "arbitrary")),
        name="experts",
    )(blk_e, n_used, xg, wg, wu, wd)


def _combine_kernel(x_ref, gate_ref, y0_ref, y1_ref, w_ref, *rest):
    o_ref = rest[-1]
    w = w_ref[...]
    y = w[:, 0:1] * y0_ref[...] + w[:, 1:2] * y1_ref[...]
    x = x_ref[...] + gate_ref[...] * y
    if len(rest) == 2:
        ms = jnp.mean(x * x, axis=-1, keepdims=True)
        x = x * lax.rsqrt(ms + EPS) * rest[0][...]
    o_ref[...] = x


def _combine(x, gate, y0, y1, w, final_g=None):
    t = x.shape[0]
    skip = 0 if final_g is None else CTX_LEN // ROW_TILE
    row = pl.BlockSpec((ROW_TILE, D_MODEL), lambda i: (i + skip, 0))
    in_specs = [row, pl.BlockSpec((None, 1, D_MODEL), lambda i: (jnp.minimum(i + skip, 1), 0, 0)),
                row, row, pl.BlockSpec((ROW_TILE, LANES), lambda i: (i + skip, 0))]
    args = [x, gate, y0, y1, w]
    if final_g is not None:
        in_specs.append(pl.BlockSpec((1, D_MODEL), lambda i: (0, 0)))
        args.append(final_g.reshape(1, D_MODEL))
    return pl.pallas_call(
        _combine_kernel,
        grid=(t // ROW_TILE - skip,),
        in_specs=in_specs,
        out_specs=pl.BlockSpec((ROW_TILE, D_MODEL), lambda i: (i, 0)),
        out_shape=jax.ShapeDtypeStruct((t - skip * ROW_TILE, D_MODEL), F32),
        compiler_params=_cparams(("arbitrary",)),
        name="moe_combine",
    )(*args)


def _moe(x, sh, sc, gate, w_router, wg, wu, wd, j, final_g=None):
    t = x.shape[0]
    wr = jnp.zeros((D_MODEL, LANES), F32).at[:, :N_EXPERTS].set(w_router)
    h, info, top_w, counts = _moe_in(x, sh, sc, wr)
    counts = counts[0, :N_EXPERTS].astype(jnp.int32)
    flat_e = info[:, 0:TOP_K].reshape(-1)
    rank = info[:, TOP_K:2 * TOP_K].reshape(-1)
    padded = ((counts + MOE_TILE - 1) // MOE_TILE) * MOE_TILE
    pad_end = jnp.cumsum(padded)
    pad_start = pad_end - padded
    dest = (pad_start[flat_e] + rank).astype(jnp.int32)
    n_blocks = (t * TOP_K + N_EXPERTS * (MOE_TILE - 1)) // MOE_TILE
    n_rows = n_blocks * MOE_TILE
    flat_tok = jnp.repeat(jnp.arange(t, dtype=jnp.int32), TOP_K)
    row_tok = jnp.zeros((n_rows,), jnp.int32).at[dest].set(
        flat_tok, mode="promise_in_bounds", unique_indices=True)
    blk_start = jnp.arange(n_blocks, dtype=jnp.int32) * MOE_TILE
    blk_e = jnp.minimum(jnp.sum(pad_end[None, :] <= blk_start[:, None], axis=1), N_EXPERTS - 1)
    n_used = (pad_end[-1:] // MOE_TILE).astype(jnp.int32)
    xg = _dispatch(h, row_tok)
    out = _experts(xg, blk_e.astype(jnp.int32), n_used, wg, wu, wd, j)
    pos = dest.reshape(t, TOP_K)
    y0 = out.at[pos[:, 0]].get(mode="promise_in_bounds")
    y1 = out.at[pos[:, 1]].get(mode="promise_in_bounds")
    return _combine(x, gate, y0, y1, top_w, final_g)


def _rope_tables(n):
    n_rows = n // GRID_W
    inv = 1.0 / (ROPE_BASE ** (jnp.arange(0, ROPE_AXIS, 2, dtype=F32) / ROPE_AXIS))
    ar = jnp.arange(n_rows, dtype=F32)[:, None] * inv
    ac = jnp.arange(GRID_W, dtype=F32)[:, None] * inv
    lane = jnp.arange(LANES)
    by_row = (lane % DA_DH) < ROPE_AXIS
    first = (lane % ROPE_AXIS) < (ROPE_AXIS // 2)
    freq = lane % (ROPE_AXIS // 2)

    def lanes(ang):
        c, s = jnp.cos(ang)[:, freq], jnp.sin(ang)[:, freq]
        return jnp.stack([c, jnp.where(first, -s, 0.0), jnp.where(first, 0.0, s)])

    ident = jnp.stack([jnp.ones((1, LANES), F32), jnp.zeros((1, LANES), F32), jnp.zeros((1, LANES), F32)])
    ctx_rows = jnp.broadcast_to(jnp.where(by_row, ident, 0.0), (3, CTX_LEN // GRID_W, LANES))
    row_tab = jnp.concatenate([ctx_rows, jnp.where(by_row, lanes(ar), 0.0)], axis=1)
    row_tab = jnp.broadcast_to(row_tab[:, :, None, :], (3, row_tab.shape[1], 8, LANES))
    ctx_cols = jnp.broadcast_to(jnp.where(by_row, 0.0, ident), (3, GRID_W, LANES))
    col_tab = jnp.stack([ctx_cols, jnp.where(by_row, 0.0, lanes(ac))])
    return row_tab, col_tab


def kernel(x, c, ctx, c_ctx, mod_w, mod_b, hg_lb, ev_w_in, hg_norm_g, sg_ln_g, sg_ln_b, sg_w, sg_b, ev_w_out, da_w_qkv, da_lam, da_subln_g, da_w_out, ffn_w_gate, ffn_w_up, ffn_w_down, moe_w_router, moe_w_gate, moe_w_up, moe_w_down, final_g):
    n = x.shape[1]
    xs = jnp.concatenate([ctx[0], x[0]], axis=0)
    cc_t = jnp.stack([c_ctx, c[0]], axis=1)
    mods = _ada_all(cc_t, mod_w, mod_b).reshape(DEPTH, 2, 6, 1, D_MODEL)
    row_tab, col_tab = _rope_tables(n)
    lb_sm = jax.nn.softmax(hg_lb.astype(F32), axis=1)
    lb_all = jnp.cumsum(lb_sm, axis=1) - lb_sm[:, :1]

    for layer in range(DEPTH):
        j = layer // 2
        sh1, sc1, g1, sh2, sc2, g2 = (mods[layer, :, i] for i in range(6))
        if layer % 2 == 0:
            y = _even_in(xs, sh1, sc1, ev_w_in[j].astype(BF16))
            o_f = _hgrn(y, lb_all[0, j], rev=False)
            a = _hgrn(y, lb_all[1, j], rev=True, o_fwd=o_f, gain=hg_norm_g[j])
            xs = _even_tail(a, y, sg_ln_g[j], sg_ln_b[j], sg_w[j].astype(BF16), sg_b[j].T,
                            ev_w_out[j].astype(BF16), xs, g1, sh2, sc2, g2,
                            ffn_w_gate[j].astype(BF16), ffn_w_up[j].astype(BF16),
                            ffn_w_down[j].astype(BF16))
        else:
            lam_init = 0.8 - 0.6 * math.exp(-0.3 * layer)
            q, k, vt = _qkv(xs, sh1, sc1, da_w_qkv[j].astype(BF16), row_tab, col_tab)
            a_ctx = _attention(q[:CTX_LEN], k[:CTX_LEN], vt[:, :CTX_LEN], da_lam[j], da_subln_g[j],
                               lam_init, CTX_LEN)
            a_lat = _attention(q[CTX_LEN:], k, vt, da_lam[j], da_subln_g[j], lam_init, ATT_Q_TILE)
            a = jnp.concatenate([a_ctx, a_lat], axis=0)
            xs = _proj_res(a, da_w_out[j].astype(BF16), xs, g1)
            xs = _moe(xs, sh2, sc2, g2, moe_w_router[j], moe_w_gate, moe_w_up, moe_w_down, j,
                      final_g if layer == DEPTH - 1 else None)
    return xs[None]
```
